```python
import math
import jax, jax.numpy as jnp
from jax import lax
import numpy as np

D_MODEL = 1024
BATCH = 2
SEQ = 16384
DEPTH = 2
DEC_BATCH = 8
DEC_SEQ = 16
PAST_LEN = 2048

CHUNK = 64
MLP_CHUNK = 128
N_HEADS_A = 4
HEAD_A = D_MODEL // 16
W_A = N_HEADS_A * HEAD_A
W_B = D_MODEL // 4
CONV_W = 31
N_HEADS_C = 8
QK_NOPE = 64
QK_ROPE = 32
V_DIM = 64
W_C = N_HEADS_C * V_DIM
Q_LORA = 384
KV_LORA = 256
ROPE_THETA = 10000.0
Q_BLOCK = 128
ATTN_SCALE = (QK_NOPE + QK_ROPE) ** -0.5
D_MIX = W_A + W_B + W_C
_COLS = (W_A, W_A, W_A, 2 * W_B, W_B, Q_LORA, KV_LORA, QK_ROPE, W_C)
D_IN = sum(_COLS)
SPLIT_POINTS = tuple(sum(_COLS[: i + 1]) for i in range(len(_COLS) - 1))
ALPHA = (2 * DEPTH) ** 0.25
BETA = (8 * DEPTH) ** -0.25
LN_EPS = 1e-5
RMS_EPS = 1e-6

kernel_name = "hybrid_streaming_encoder_step"


def _layernorm(x, g=None, b=None):
    xf = x.astype(jnp.float32)
    mu = jnp.mean(xf, axis=-1, keepdims=True)
    var = jnp.mean(jnp.square(xf - mu), axis=-1, keepdims=True)
    y = (xf - mu) * lax.rsqrt(var + LN_EPS)
    if g is not None:
        y = y * g.astype(jnp.float32) + b.astype(jnp.float32)
    return y.astype(x.dtype)


def _rmsnorm(x, g):
    xf = x.astype(jnp.float32)
    y = xf * lax.rsqrt(jnp.mean(jnp.square(xf), axis=-1, keepdims=True) + RMS_EPS)
    return (y * g.astype(jnp.float32)).astype(x.dtype)


def _rope_angles(pos):
    inv = ROPE_THETA ** (-jnp.arange(0, QK_ROPE, 2, dtype=jnp.float32) / QK_ROPE)
    ang = pos.astype(jnp.float32)[:, None] * inv[None, :]
    return jnp.cos(ang), jnp.sin(ang)


def _apply_rope(x, cos, sin):
    half = QK_ROPE // 2
    x1, x2 = x[..., :half], x[..., half:]
    cos = cos.astype(x.dtype)
    sin = sin.astype(x.dtype)
    return jnp.concatenate([x1 * cos - x2 * sin, x1 * sin + x2 * cos], axis=-1)


def _spatial_gating(u, v, ln_g, ln_b, w_s, b_s):
    bsz, t, _ = u.shape
    u = jax.nn.gelu(u)
    v = _layernorm(jax.nn.gelu(v), ln_g, ln_b)
    L = min(t, MLP_CHUNK)
    n = t // L
    idx = jnp.arange(L)
    mask = (idx[None, :] // CHUNK) <= (idx[:, None] // CHUNK)
    ws = w_s[:, :L, :L]
    ws = jnp.where(mask[None], ws, jnp.zeros_like(ws))
    vh = v.reshape(bsz, n, L, N_HEADS_A, HEAD_A)
    s = jnp.einsum('hij,bnjhd->bnihd', ws, vh) + b_s[:, :L].T[None, None, :, :, None]
    return u * s.reshape(bsz, t, W_A), v


def _conv_module(a, past, dw_k, dw_b, ln_g, ln_b, w_pw, b_pw):
    g = a[..., :W_B] * jax.nn.sigmoid(a[..., W_B:])
    xp = jnp.concatenate([past, g], axis=1)
    y = lax.conv_general_dilated(xp, dw_k[:, None, :], (1,), 'VALID',
                                 dimension_numbers=('NWC', 'WIO', 'NWC'),
                                 feature_group_count=W_B) + dw_b
    y = jax.nn.silu(_layernorm(y, ln_g, ln_b))
    y = y @ w_pw + b_pw
    return y, xp[:, -(CONV_W - 1):]


def _attend(q_n, q_r, k_n, k_r, v, q_pos, k_pos):
    s = jnp.einsum('bqhd,bkhd->bhqk', q_n, k_n) + jnp.einsum('bqhr,bkr->bhqk', q_r, k_r)
    s = s.astype(jnp.float32) * ATTN_SCALE
    mask = (k_pos[None, :] // CHUNK) <= (q_pos[:, None] // CHUNK)
    s = jnp.where(mask[None, None], s, -jnp.inf)
    p = jax.nn.softmax(s, axis=-1).astype(v.dtype)
    return jnp.einsum('bhqk,bkhd->bqhd', p, v)


def _layer(x, c, pos0, conv_past, lat_past, kr_past, lp):
    bsz, t, _ = x.shape
    mod = jax.nn.silu(c) @ lp['w_ada'] + lp['b_ada']
    shift, scale, gate = jnp.split(mod[:, None, :], 3, axis=-1)
    h = _layernorm(x) * (1 + scale) + shift
    z = h @ lp['w_in']
    u_a, v_a, g_a, glu_b, g_b, c_q, c_kv, k_r, g_c = jnp.split(z, SPLIT_POINTS, axis=-1)

    y_a, v_state = _spatial_gating(u_a, v_a, lp['gmlp_ln_g'], lp['gmlp_ln_b'], lp['gmlp_ws'], lp['gmlp_bs'])
    y_a = y_a * jax.nn.silu(g_a)

    if conv_past is None:
        conv_past = jnp.zeros((bsz, CONV_W - 1, W_B), x.dtype)
    y_b, conv_state = _conv_module(glu_b, conv_past, lp['conv_dw_k'], lp['conv_dw_b'], lp['conv_ln_g'],
                                   lp['conv_ln_b'], lp['conv_w_pw'], lp['conv_b_pw'])
    y_b = y_b * jax.nn.silu(g_b)

    pos = pos0 + jnp.arange(t, dtype=jnp.int32)
    cos, sin = _rope_angles(pos)
    q = (_rmsnorm(c_q, lp['mla_q_norm']) @ lp['mla_w_uq']).reshape(bsz, t, N_HEADS_C, QK_NOPE + QK_ROPE)
    q_n = q[..., :QK_NOPE]
    q_r = _apply_rope(q[..., QK_NOPE:], cos[:, None, :], sin[:, None, :])
    lat = _rmsnorm(c_kv, lp['mla_kv_norm'])
    k_r = _apply_rope(k_r, cos, sin)
    if lat_past is None:
        lat_all, kr_all, k_pos = lat, k_r, pos
    else:
        lat_all = jnp.concatenate([lat_past, lat], axis=1)
        kr_all = jnp.concatenate([kr_past, k_r], axis=1)
        k_pos = jnp.arange(lat_all.shape[1], dtype=jnp.int32)
    kv = (lat_all @ lp['mla_w_ukv']).reshape(bsz, -1, N_HEADS_C, QK_NOPE + V_DIM)
    k_n, v = kv[..., :QK_NOPE], kv[..., QK_NOPE:]
    if t % Q_BLOCK == 0:
        nblk = t // Q_BLOCK
        qn_b = q_n.reshape(bsz, nblk, Q_BLOCK, N_HEADS_C, QK_NOPE).swapaxes(0, 1)
        qr_b = q_r.reshape(bsz, nblk, Q_BLOCK, N_HEADS_C, QK_ROPE).swapaxes(0, 1)
        pos_b = pos.reshape(nblk, Q_BLOCK)
        o = lax.map(lambda a: _attend(a[0], a[1], k_n, kr_all, v, a[2], k_pos), (qn_b, qr_b, pos_b))
        o = o.swapaxes(0, 1).reshape(bsz, t, W_C)
    else:
        o = _attend(q_n, q_r, k_n, kr_all, v, pos, k_pos).reshape(bsz, t, W_C)
    y_c = o * jax.nn.silu(g_c)

    y = jnp.concatenate([y_a, y_b, y_c], axis=-1) @ lp['w_out']
    x = _layernorm(ALPHA * x + gate * y, lp['post_ln_g'], lp['post_ln_b'])
    return x, v_state, conv_state, lat, k_r


def setup_inputs(seed: int = 0) -> dict:
    key = jax.random.key(seed)
    ks = jax.random.split(key, 32)
    f32 = jnp.float32

    def nrm(k, shape, s):
        return jax.random.normal(k, shape, f32) * s

    return {
        'x_prompt': nrm(ks[0], (BATCH, SEQ, D_MODEL), 1.0),
        'x_sample': nrm(ks[1], (DEC_BATCH, DEC_SEQ, D_MODEL), 1.0),
        'cache_latent': nrm(ks[2], (DEPTH, DEC_BATCH, PAST_LEN, KV_LORA), 1.0),
        'cache_krope': nrm(ks[3], (DEPTH, DEC_BATCH, PAST_LEN, QK_ROPE), 1.0),
        'state_conv': nrm(ks[4], (DEPTH, DEC_BATCH, CONV_W - 1, W_B), 0.5),
        'c_prompt': nrm(ks[5], (BATCH, D_MODEL), 1.0),
        'c_sample': nrm(ks[6], (DEC_BATCH, D_MODEL), 1.0),
        'w_ada': nrm(ks[7], (DEPTH, D_MODEL, 3 * D_MODEL), 0.5 * D_MODEL ** -0.5),
        'b_ada': nrm(ks[8], (DEPTH, 3 * D_MODEL), 0.02),
        'w_in': nrm(ks[9], (DEPTH, D_MODEL, D_IN), D_MODEL ** -0.5),
        'gmlp_ln_g': 1.0 + nrm(ks[10], (DEPTH, W_A), 0.02),
        'gmlp_ln_b': nrm(ks[11], (DEPTH, W_A), 0.02),
        'gmlp_ws': nrm(ks[12], (DEPTH, N_HEADS_A, MLP_CHUNK, MLP_CHUNK), MLP_CHUNK ** -0.5),
        'gmlp_bs': 1.0 + nrm(ks[13], (DEPTH, N_HEADS_A, MLP_CHUNK), 0.01),
        'conv_dw_k': nrm(ks[14], (DEPTH, CONV_W, W_B), CONV_W ** -0.5),
        'conv_dw_b': nrm(ks[15], (DEPTH, W_B), 0.02),
        'conv_ln_g': 1.0 + nrm(ks[16], (DEPTH, W_B), 0.02),
        'conv_ln_b': nrm(ks[17], (DEPTH, W_B), 0.02),
        'conv_w_pw': nrm(ks[18], (DEPTH, W_B, W_B), W_B ** -0.5),
        'conv_b_pw': nrm(ks[19], (DEPTH, W_B), 0.02),
        'mla_q_norm': 1.0 + nrm(ks[20], (DEPTH, Q_LORA), 0.02),
        'mla_w_uq': nrm(ks[21], (DEPTH, Q_LORA, N_HEADS_C * (QK_NOPE + QK_ROPE)), Q_LORA ** -0.5),
        'mla_kv_norm': 1.0 + nrm(ks[22], (DEPTH, KV_LORA), 0.02),
        'mla_w_ukv': nrm(ks[23], (DEPTH, KV_LORA, N_HEADS_C * (QK_NOPE + V_DIM)), KV_LORA ** -0.5),
        'w_out': nrm(ks[24], (DEPTH, D_MIX, D_MODEL), BETA * D_MIX ** -0.5),
        'post_ln_g': 1.0 + nrm(ks[25], (DEPTH, D_MODEL), 0.02),
        'post_ln_b': nrm(ks[26], (DEPTH, D_MODEL), 0.02),
    }


def reference(x_prompt, x_sample, cache_latent, cache_krope, state_conv, c_prompt, c_sample,
              w_ada, b_ada, w_in, gmlp_ln_g, gmlp_ln_b, gmlp_ws, gmlp_bs,
              conv_dw_k, conv_dw_b, conv_ln_g, conv_ln_b, conv_w_pw, conv_b_pw,
              mla_q_norm, mla_w_uq, mla_kv_norm, mla_w_ukv, w_out, post_ln_g, post_ln_b):
    past_len = cache_latent.shape[2]
    xp, xs = x_prompt, x_sample
    p_conv, p_lat, p_kr = [], [], []
    s_conv, s_lat, s_kr, s_v = [], [], [], []
    for l in range(DEPTH):
        lp = dict(w_ada=w_ada[l], b_ada=b_ada[l], w_in=w_in[l],
                  gmlp_ln_g=gmlp_ln_g[l], gmlp_ln_b=gmlp_ln_b[l], gmlp_ws=gmlp_ws[l], gmlp_bs=gmlp_bs[l],
                  conv_dw_k=conv_dw_k[l], conv_dw_b=conv_dw_b[l], conv_ln_g=conv_ln_g[l],
                  conv_ln_b=conv_ln_b[l], conv_w_pw=conv_w_pw[l], conv_b_pw=conv_b_pw[l],
                  mla_q_norm=mla_q_norm[l], mla_w_uq=mla_w_uq[l], mla_kv_norm=mla_kv_norm[l],
                  mla_w_ukv=mla_w_ukv[l], w_out=w_out[l], post_ln_g=post_ln_g[l], post_ln_b=post_ln_b[l])
        xp, _, cp, lp_lat, lp_kr = _layer(xp, c_prompt, 0, None, None, None, lp)
        p_conv.append(cp); p_lat.append(lp_lat); p_kr.append(lp_kr)
        xs, vs, cs, ls_lat, ls_kr = _layer(xs, c_sample, past_len, state_conv[l], cache_latent[l],
                                           cache_krope[l], lp)
        s_conv.append(cs); s_lat.append(ls_lat); s_kr.append(ls_kr); s_v.append(vs)
    return (xp, xs,
            jnp.stack(p_conv), jnp.stack(p_lat), jnp.stack(p_kr),
            jnp.stack(s_conv), jnp.stack(s_lat), jnp.stack(s_kr), jnp.stack(s_v))
```

```python
import functools
import math

import jax
import jax.numpy as jnp
from jax import lax
from jax.experimental import pallas as pl
from jax.experimental.pallas import tpu as pltpu

F32 = jnp.float32
BF16 = jnp.bfloat16

D_MODEL = 1024
N_HEADS_A = 4
HEAD_A = 64
W_A = 256
W_B = 256
CONV_W = 31
N_HEADS = 8
QK_NOPE = 64
QK_ROPE = 32
V_DIM = 64
W_C = N_HEADS * V_DIM
Q_LORA = 384
KV_LORA = 256
CHUNK = 64
MLP_CHUNK = 128
ROPE_THETA = 10000.0
ATTN_SCALE = (QK_NOPE + QK_ROPE) ** -0.5
LN_EPS = 1e-5
RMS_EPS = 1e-6

HEAD_LANES = 128
V_ROWS = 80
HIST = 32
ATT_TILE = 256
Q_BLOCK_ROWS = 1024
IN_TILE = 512
VMEM_LIMIT = 56 * 1024 * 1024

_C_A, _C_B, _C_Q, _C_KV, _C_KR, _C_GC, _C_END = 0, 768, 1536, 1920, 2176, 2432, 2944
_Q_SCALE = ATTN_SCALE * math.log2(math.e)


def _dot(a, b):
    return jnp.dot(a, b, preferred_element_type=F32)


def _dot_nt(a, b):
    return lax.dot_general(a, b, (((1,), (1,)), ((), ())), preferred_element_type=F32)


def _norm_rows(x, eps):
    mu = jnp.mean(x, axis=-1, keepdims=True)
    xc = x - mu
    var = jnp.mean(xc * xc, axis=-1, keepdims=True)
    return xc * lax.rsqrt(var + eps)


def _gelu_tanh(x):
    return x * (0.5 * (1.0 + jnp.tanh(math.sqrt(2.0 / math.pi) * (x + 0.044715 * (x * x * x)))))


def _silu(x):
    return x * jax.nn.sigmoid(x)


def _ada_kernel(c_ref, w_ref, b_ref, o_ref):
    c = c_ref[...]
    a = _silu(c)
    w = w_ref[0]
    a_hi = a.astype(BF16)
    a_lo = (a - a_hi.astype(F32)).astype(BF16)
    w_hi = w.astype(BF16)
    w_lo = (w - w_hi.astype(F32)).astype(BF16)
    o_ref[0] = _dot(a_hi, w_hi) + (_dot(a_hi, w_lo) + _dot(a_lo, w_hi)) + b_ref[0]


def _ada(c_all, w_ada, b_ada):
    depth = w_ada.shape[0]
    rows = c_all.shape[0]
    n_col = w_ada.shape[2] // D_MODEL
    return pl.pallas_call(
        _ada_kernel,
        grid=(depth, n_col),
        in_specs=[
            pl.BlockSpec((rows, D_MODEL), lambda l, j: (0, 0)),
            pl.BlockSpec((1, D_MODEL, D_MODEL), lambda l, j: (l, 0, j)),
            pl.BlockSpec((1, 1, D_MODEL), lambda l, j: (l, 0, j)),
        ],
        out_specs=pl.BlockSpec((1, rows, D_MODEL), lambda l, j: (l, 0, j)),
        out_shape=jax.ShapeDtypeStruct((depth, rows, w_ada.shape[2]), F32),
        compiler_params=pltpu.CompilerParams(vmem_limit_bytes=VMEM_LIMIT),
        name="ada_mod",
    )(c_all, w_ada, b_ada.reshape(depth, 1, -1))


def _mixer_in_kernel(x_ref, mod_ref, ct_ref, st_ref, past_ref, w_in_ref, vec_ref, qn_ref, ws_ref, bs_ref,
                     dwk_ref, w_pw_ref, w_uq_ref, *rest, tile, chunk, prompt):
    if prompt:
        (w_uk_ref, w_uvt_ref, q_ref, lat_ref, kr_ref, yab_ref, gc_ref, conv_ref, k_ref, vt_ref, hist) = rest
    else:
        (q_ref, lat_ref, kr_ref, yab_ref, gc_ref, conv_ref, vst_ref, hist) = rest
    t = pl.program_id(1)

    mod = mod_ref[0]
    shift = mod[:, 0:D_MODEL]
    scale = mod[:, D_MODEL:2 * D_MODEL]
    h = _norm_rows(x_ref[0], LN_EPS) * (1.0 + scale) + shift
    hb = h.astype(BF16)

    vec = vec_ref[...]
    gmlp_g, gmlp_b = vec[0:1], vec[1:2]
    dw_b, conv_g, conv_b, b_pw, kv_norm = vec[2:3], vec[3:4], vec[4:5], vec[5:6], vec[6:7]

    za = _dot(hb, w_in_ref[:, _C_A:_C_B])
    u = _gelu_tanh(za[:, 0:W_A])
    v = _norm_rows(_gelu_tanh(za[:, W_A:2 * W_A]), LN_EPS) * gmlp_g + gmlp_b
    if not prompt:
        vst_ref[0] = v
    vb = v.astype(BF16)
    rows = lax.broadcasted_iota(jnp.int32, (N_HEADS_A * chunk, W_A), 0)
    cols = lax.broadcasted_iota(jnp.int32, (N_HEADS_A * chunk, W_A), 1)
    own_head = (rows // chunk) == (cols // HEAD_A)
    ws = ws_ref[...]
    bs = bs_ref[...]
    mixed = []
    for c in range(tile // chunk):
        vc = vb[c * chunk:(c + 1) * chunk]
        vbd = jnp.where(own_head, jnp.concatenate([vc] * N_HEADS_A, axis=0), jnp.zeros((), BF16))
        mixed.append(_dot(ws, vbd) + bs)
    s = mixed[0] if len(mixed) == 1 else jnp.concatenate(mixed, axis=0)
    y_a = u * s * _silu(za[:, 2 * W_A:3 * W_A])
    yab_ref[0, :, 0:W_A] = y_a.astype(BF16)

    zb = _dot(hb, w_in_ref[:, _C_B:_C_Q])
    g = zb[:, 0:W_B] * jax.nn.sigmoid(zb[:, W_B:2 * W_B])

    @pl.when(t == 0)
    def _():
        hist[0:HIST] = past_ref[0]

    hist[HIST:HIST + tile] = g
    dwk = dwk_ref[...]
    y = jnp.broadcast_to(dw_b, (tile, W_B))
    for k in range(CONV_W):
        y = y + dwk[k:k + 1] * hist[HIST - (CONV_W - 1) + k:HIST - (CONV_W - 1) + k + tile]
    new_hist = hist[tile:tile + HIST]
    hist[0:HIST] = new_hist
    conv_ref[0] = new_hist
    y = _silu(_norm_rows(y, LN_EPS) * conv_g + conv_b)
    y = _dot(y.astype(BF16), w_pw_ref[...]) + b_pw
    y_b = y * _silu(zb[:, 2 * W_B:3 * W_B])
    yab_ref[0, :, W_A:W_A + W_B] = y_b.astype(BF16)

    ct = ct_ref[...]
    st = st_ref[...]
    zq = _dot(hb, w_in_ref[:, _C_Q:_C_KV])
    cq = zq * lax.rsqrt(jnp.mean(zq * zq, axis=-1, keepdims=True) + RMS_EPS) * qn_ref[...]
    q2 = _dot(cq.astype(BF16), w_uq_ref[...])
    ctq = ct * _Q_SCALE
    stq = st * _Q_SCALE
    for hd in range(N_HEADS):
        lo = hd * HEAD_LANES
        qh = q2[:, lo:lo + HEAD_LANES] * ctq + q2[:, N_HEADS * HEAD_LANES + lo:N_HEADS * HEAD_LANES + lo + HEAD_LANES] * stq
        q_ref[0, :, lo:lo + HEAD_LANES] = qh.astype(BF16)

    zkv = _dot(hb, w_in_ref[:, _C_KV:_C_KR])
    lat = zkv * lax.rsqrt(jnp.mean(zkv * zkv, axis=-1, keepdims=True) + RMS_EPS) * kv_norm
    lat_ref[0] = lat
    zkr = _dot(hb, w_in_ref[:, _C_KR:_C_GC])
    krot = zkr[:, 0:HEAD_LANES] * ct + zkr[:, HEAD_LANES:2 * HEAD_LANES] * st
    kr_ref[0] = krot[:, QK_NOPE:QK_NOPE + QK_ROPE]

    if prompt:
        latb = lat.astype(BF16)
        kn = _dot(latb, w_uk_ref[...])
        for hd in range(N_HEADS):
            lo = hd * HEAD_LANES
            k_ref[0, :, lo:lo + HEAD_LANES] = (kn[:, lo:lo + HEAD_LANES] + krot).astype(BF16)
        vt = _dot_nt(w_uvt_ref[...], latb)
        vrow = lax.broadcasted_iota(jnp.int32, vt.shape, 0)
        vt = jnp.where((vrow % V_ROWS) == V_DIM, 1.0, vt)
        for c in range(tile // ATT_TILE):
            blk = vt[:, c * ATT_TILE:(c + 1) * ATT_TILE].reshape(N_HEADS, V_ROWS, ATT_TILE)
            vt_ref[0, :, c] = blk.astype(BF16)

    zg = _dot(hb, w_in_ref[:, _C_GC:_C_END])
    gc_ref[0] = _silu(zg).astype(BF16)


def _mixer_in(x, mod, ctab, stab, past, wts, *, tile, chunk, prompt):
    bsz, seq, _ = x.shape
    n_t = seq // tile
    kern = functools.partial(_mixer_in_kernel, tile=tile, chunk=chunk, prompt=prompt)

    def full(a):
        return pl.BlockSpec(a.shape, lambda b, t, _n=a.ndim: (0,) * _n)

    weights = [wts["w_in"], wts["vec"], wts["q_norm"], wts["ws"], wts["bs"], wts["dwk"], wts["w_pw"], wts["w_uq"]]
    if prompt:
        weights += [wts["w_uk"], wts["w_uvt"]]
    in_specs = [
        pl.BlockSpec((1, tile, D_MODEL), lambda b, t: (b, t, 0)),
        pl.BlockSpec((1, 1, 3 * D_MODEL), lambda b, t: (b, 0, 0)),
        pl.BlockSpec((tile, HEAD_LANES), lambda b, t: (t, 0)),
        pl.BlockSpec((tile, HEAD_LANES), lambda b, t: (t, 0)),
        pl.BlockSpec((1, HIST, W_B), lambda b, t: (b, 0, 0)),
    ] + [full(w) for w in weights]
    tok = lambda width: pl.BlockSpec((1, tile, width), lambda b, t: (b, t, 0))
    out_specs = [tok(N_HEADS * HEAD_LANES), tok(KV_LORA), tok(QK_ROPE), tok(W_A + W_B), tok(W_C),
                 pl.BlockSpec((1, HIST, W_B), lambda b, t: (b, 0, 0))]
    out_shape = [
        jax.ShapeDtypeStruct((bsz, seq, N_HEADS * HEAD_LANES), BF16),
        jax.ShapeDtypeStruct((bsz, seq, KV_LORA), F32),
        jax.ShapeDtypeStruct((bsz, seq, QK_ROPE), F32),
        jax.ShapeDtypeStruct((bsz, seq, W_A + W_B), BF16),
        jax.ShapeDtypeStruct((bsz, seq, W_C), BF16),
        jax.ShapeDtypeStruct((bsz, HIST, W_B), F32),
    ]
    if prompt:
        n_kt = tile // ATT_TILE
        out_specs += [tok(N_HEADS * HEAD_LANES),
                      pl.BlockSpec((1, N_HEADS, n_kt, V_ROWS, ATT_TILE), lambda b, t: (b, 0, t, 0, 0))]
        out_shape += [jax.ShapeDtypeStruct((bsz, seq, N_HEADS * HEAD_LANES), BF16),
                      jax.ShapeDtypeStruct((bsz, N_HEADS, seq // ATT_TILE, V_ROWS, ATT_TILE), BF16)]
    else:
        out_specs += [tok(W_A)]
        out_shape += [jax.ShapeDtypeStruct((bsz, seq, W_A), F32)]
    return pl.pallas_call(
        kern,
        grid=(bsz, n_t),
        in_specs=in_specs,
        out_specs=out_specs,
        out_shape=out_shape,
        scratch_shapes=[pltpu.VMEM((HIST + tile, W_B), F32)],
        compiler_params=pltpu.CompilerParams(dimension_semantics=("arbitrary", "arbitrary"),
                                             vmem_limit_bytes=VMEM_LIMIT),
        name="mixer_in_prompt" if prompt else "mixer_in_sample",
    )(x, mod, ctab, stab, past, *weights)


def _attn_prompt_kernel(q_ref, k_ref, vt_ref, o_ref, m_sc, acc_sc, *, n_sub):
    qi = pl.program_id(2)
    kk = lax.broadcasted_iota(jnp.int32, (ATT_TILE, ATT_TILE), 0)
    qq = lax.broadcasted_iota(jnp.int32, (ATT_TILE, ATT_TILE), 1)
    diag_ok = (kk // CHUNK) <= (qq // CHUNK)

    def sub_body(sub, carry):
        row0 = pl.multiple_of(sub * ATT_TILE, ATT_TILE)
        g = qi * n_sub + sub
        qs = [q_ref[0, pl.ds(row0, ATT_TILE), hd * HEAD_LANES:(hd + 1) * HEAD_LANES] for hd in range(2)]
        m_sc[...] = jnp.full(m_sc.shape, -jnp.inf, F32)
        acc_sc[...] = jnp.zeros(acc_sc.shape, F32)

        def k_step(ki, masked):
            k0 = pl.multiple_of(ki * ATT_TILE, ATT_TILE)
            for hd in range(2):
                kt = k_ref[0, pl.ds(k0, ATT_TILE), hd * HEAD_LANES:(hd + 1) * HEAD_LANES]
                s = _dot_nt(kt, qs[hd])
                if masked:
                    s = jnp.where(diag_ok, s, -jnp.inf)
                m_prev = m_sc[hd]
                m_new = jnp.maximum(m_prev, jnp.max(s, axis=0, keepdims=True))
                p = jnp.exp2(s - m_new).astype(BF16)
                alpha = jnp.exp2(m_prev - m_new)
                acc_sc[hd] = alpha * acc_sc[hd] + _dot(vt_ref[0, hd, ki], p)
                m_sc[hd] = m_new

        def k_body(ki, c):
            k_step(ki, False)
            return c

        lax.fori_loop(0, g, k_body, 0)
        k_step(g, True)

        outs = []
        for hd in range(2):
            acc = acc_sc[hd]
            outs.append(acc[0:V_DIM] * (1.0 / acc[V_DIM:V_DIM + 1]))
        o_t = jnp.concatenate(outs, axis=0)
        o_ref[0, pl.ds(row0, ATT_TILE), :] = o_t.T.astype(BF16)
        return carry

    lax.fori_loop(0, n_sub, sub_body, 0)


def _attn_prompt(q, k, vt):
    bsz, seq, _ = q.shape
    qb = min(Q_BLOCK_ROWS, seq)
    n_sub = qb // ATT_TILE
    n_kt = seq // ATT_TILE
    kern = functools.partial(_attn_prompt_kernel, n_sub=n_sub)
    return pl.pallas_call(
        kern,
        grid=(bsz, N_HEADS // 2, seq // qb),
        in_specs=[
            pl.BlockSpec((1, qb, 2 * HEAD_LANES), lambda b, hp, i: (b, i, hp)),
            pl.BlockSpec((1, seq, 2 * HEAD_LANES), lambda b, hp, i: (b, 0, hp)),
            pl.BlockSpec((1, 2, n_kt, V_ROWS, ATT_TILE), lambda b, hp, i: (b, hp, 0, 0, 0)),
        ],
        out_specs=pl.BlockSpec((1, qb, 2 * V_DIM), lambda b, hp, i: (b, i, hp)),
        out_shape=jax.ShapeDtypeStruct((bsz, seq, W_C), BF16),
        scratch_shapes=[pltpu.VMEM((2, 1, ATT_TILE), F32), pltpu.VMEM((2, V_ROWS, ATT_TILE), F32)],
        compiler_params=pltpu.CompilerParams(dimension_semantics=("arbitrary", "arbitrary", "arbitrary"),
                                             vmem_limit_bytes=VMEM_LIMIT),
        name="attn_prompt",
    )(q, k, vt)


def _attn_sample_kernel(q_ref, latc_ref, krc_ref, latn_ref, krn_ref, wabs_ref, fold_ref, wuv_ref, o_ref, *, t_new):
    n_rows = N_HEADS * t_new
    q = q_ref[0]
    qrep = jnp.concatenate([q] * N_HEADS, axis=0)
    r1 = lax.broadcasted_iota(jnp.int32, qrep.shape, 0)
    c1 = lax.broadcasted_iota(jnp.int32, qrep.shape, 1)
    qrep = jnp.where((r1 // t_new) == (c1 // HEAD_LANES), qrep, jnp.zeros((), BF16))
    qlat = _dot(qrep, wabs_ref[...]).astype(BF16)
    qr = _dot(qrep, fold_ref[...])[:, 0:QK_ROPE].astype(BF16)
    latc = latc_ref[0, 0].astype(BF16)
    krc = krc_ref[0, 0].astype(BF16)
    latn = latn_ref[0].astype(BF16)
    krn = krn_ref[0].astype(BF16)
    s_c = _dot_nt(qlat, latc) + _dot_nt(qr, krc)
    s_n = _dot_nt(qlat, latn) + _dot_nt(qr, krn)
    m = jnp.maximum(jnp.max(s_c, axis=-1, keepdims=True), jnp.max(s_n, axis=-1, keepdims=True))
    p_c = jnp.exp2(s_c - m)
    p_n = jnp.exp2(s_n - m)
    den = jnp.sum(p_c, axis=-1, keepdims=True) + jnp.sum(p_n, axis=-1, keepdims=True)
    olat = (_dot(p_c.astype(BF16), latc) + _dot(p_n.astype(BF16), latn)) * (1.0 / den)
    of = _dot(olat.astype(BF16), wuv_ref[...])
    r2 = lax.broadcasted_iota(jnp.int32, of.shape, 0)
    c2 = lax.broadcasted_iota(jnp.int32, of.shape, 1)
    of = jnp.where((r2 // t_new) == (c2 // V_DIM), of, 0.0)
    o = of[0:t_new]
    for hd in range(1, N_HEADS):
        o = o + of[hd * t_new:(hd + 1) * t_new]
    o_ref[0] = o.astype(BF16)
    del n_rows


def _attn_sample(q, lat_cache, kr_cache, layer, lat_new, kr_new, wts):
    bsz, t_new, _ = q.shape
    past = lat_cache.shape[2]
    kern = functools.partial(_attn_sample_kernel, t_new=t_new)

    def full(a):
        return pl.BlockSpec(a.shape, lambda b, _n=a.ndim: (0,) * _n)

    return pl.pallas_call(
        kern,
        grid=(bsz,),
        in_specs=[
            pl.BlockSpec((1, t_new, N_HEADS * HEAD_LANES), lambda b: (b, 0, 0)),
            pl.BlockSpec((1, 1, past, KV_LORA), lambda b: (layer, b, 0, 0)),
            pl.BlockSpec((1, 1, past, QK_ROPE), lambda b: (layer, b, 0, 0)),
            pl.BlockSpec((1, t_new, KV_LORA), lambda b: (b, 0, 0)),
            pl.BlockSpec((1, t_new, QK_ROPE), lambda b: (b, 0, 0)),
            full(wts["wabs"]), full(wts["fold"]), full(wts["w_uv"]),
        ],
        out_specs=pl.BlockSpec((1, t_new, W_C), lambda b: (b, 0, 0)),
        out_shape=jax.ShapeDtypeStruct((bsz, t_new, W_C), BF16),
        compiler_params=pltpu.CompilerParams(dimension_semantics=("arbitrary",), vmem_limit_bytes=VMEM_LIMIT),
        name="attn_sample",
    )(q, lat_cache, kr_cache, lat_new, kr_new, wts["wabs"], wts["fold"], wts["w_uv"])


def _mixer_out_kernel(x_ref, mod_ref, yab_ref, o_ref, gc_ref, w_out_ref, ln_ref, out_ref, *, alpha):
    gate = mod_ref[0][:, 2 * D_MODEL:3 * D_MODEL]
    yc = (o_ref[0].astype(F32) * gc_ref[0].astype(F32)).astype(BF16)
    y = _dot(yab_ref[0], w_out_ref[0:W_A + W_B]) + _dot(yc, w_out_ref[W_A + W_B:W_A + W_B + W_C])
    r = alpha * x_ref[0] + gate * y
    ln = ln_ref[...]
    out_ref[0] = _norm_rows(r, LN_EPS) * ln[0:1] + ln[1:2]


def _mixer_out(x, mod, yab, o, gc, w_out, ln, *, tile, alpha):
    bsz, seq, _ = x.shape
    tok = lambda width: pl.BlockSpec((1, tile, width), lambda b, t: (b, t, 0))
    return pl.pallas_call(
        functools.partial(_mixer_out_kernel, alpha=alpha),
        grid=(bsz, seq // tile),
        in_specs=[tok(D_MODEL), pl.BlockSpec((1, 1, 3 * D_MODEL), lambda b, t: (b, 0, 0)),
                  tok(W_A + W_B), tok(W_C), tok(W_C),
                  pl.BlockSpec(w_out.shape, lambda b, t: (0, 0)),
                  pl.BlockSpec(ln.shape, lambda b, t: (0, 0))],
        out_specs=tok(D_MODEL),
        out_shape=jax.ShapeDtypeStruct(x.shape, F32),
        compiler_params=pltpu.CompilerParams(dimension_semantics=("arbitrary", "arbitrary"),
                                             vmem_limit_bytes=VMEM_LIMIT),
        name="mixer_out",
    )(x, mod, yab, o, gc, w_out, ln)


def _rope_tables(pos0, n):
    inv = ROPE_THETA ** (-jnp.arange(0, QK_ROPE, 2, dtype=F32) / QK_ROPE)
    ang = (pos0 + jnp.arange(n, dtype=jnp.int32)).astype(F32)[:, None] * inv[None, :]
    cos, sin = jnp.cos(ang), jnp.sin(ang)
    ones = jnp.ones((n, QK_NOPE), F32)
    z64 = jnp.zeros((n, QK_NOPE), F32)
    z32 = jnp.zeros((n, HEAD_LANES - QK_NOPE - QK_ROPE), F32)
    return jnp.concatenate([ones, cos, cos, z32], axis=1), jnp.concatenate([z64, sin, sin, z32], axis=1)


def _gmlp_weights(ws, b_s, chunk):
    idx = jnp.arange(chunk)
    mask = (idx[None, :] // CHUNK) <= (idx[:, None] // CHUNK)
    wsm = jnp.where(mask[None], ws[:, :chunk, :chunk], 0.0)
    ws_all = jnp.transpose(wsm, (1, 0, 2)).reshape(chunk, N_HEADS_A * chunk)
    bs_tab = jnp.repeat(b_s[:, :chunk].T, HEAD_A, axis=1)
    return ws_all.astype(BF16), bs_tab.astype(F32)


def _layer_weights(l, p, chunk_p, chunk_s):
    half = QK_ROPE // 2
    w_in = p["w_in"][l]
    krw = w_in[:, 2176:2208]
    x1, x2 = krw[:, :half], krw[:, half:]
    z64 = jnp.zeros((D_MODEL, QK_NOPE), F32)
    z32 = jnp.zeros((D_MODEL, HEAD_LANES - QK_NOPE - QK_ROPE), F32)
    w_in2 = jnp.concatenate([w_in[:, :2176], z64, x1, x2, z32, z64, -x2, x1, z32, w_in[:, 2208:]], axis=1)

    wq = p["mla_w_uq"][l].reshape(Q_LORA, N_HEADS, QK_NOPE + QK_ROPE)
    qn, q1, q2 = wq[..., :QK_NOPE], wq[..., QK_NOPE:QK_NOPE + half], wq[..., QK_NOPE + half:]
    zq32 = jnp.zeros((Q_LORA, N_HEADS, HEAD_LANES - QK_NOPE - QK_ROPE), F32)
    zq64 = jnp.zeros((Q_LORA, N_HEADS, QK_NOPE), F32)
    w_uq = jnp.concatenate([
        jnp.concatenate([qn, q1, q2, zq32], axis=-1).reshape(Q_LORA, -1),
        jnp.concatenate([zq64, -q2, q1, zq32], axis=-1).reshape(Q_LORA, -1)], axis=1)

    wkv = p["mla_w_ukv"][l].reshape(KV_LORA, N_HEADS, QK_NOPE + V_DIM)
    wk, wv = wkv[..., :QK_NOPE], wkv[..., QK_NOPE:]
    w_uk = jnp.concatenate([wk, jnp.zeros_like(wk)], axis=-1).reshape(KV_LORA, -1)
    wvt = jnp.transpose(wv, (1, 2, 0))
    w_uvt = jnp.concatenate([wvt, jnp.zeros((N_HEADS, V_ROWS - V_DIM, KV_LORA), F32)], axis=1).reshape(-1, KV_LORA)
    wkt = jnp.transpose(wk, (1, 2, 0))
    wabs = jnp.concatenate([wkt, jnp.zeros((N_HEADS, HEAD_LANES - QK_NOPE, KV_LORA), F32)], axis=1).reshape(-1, KV_LORA)
    eye = jnp.eye(QK_ROPE, HEAD_LANES, dtype=F32)
    fold_h = jnp.concatenate([jnp.zeros((QK_NOPE, HEAD_LANES), F32), eye,
                              jnp.zeros((HEAD_LANES - QK_NOPE - QK_ROPE, HEAD_LANES), F32)], axis=0)
    fold = jnp.tile(fold_h, (N_HEADS, 1))

    zrow = jnp.zeros((1, W_B), F32)
    vec = jnp.stack([p["gmlp_ln_g"][l], p["gmlp_ln_b"][l], p["conv_dw_b"][l], p["conv_ln_g"][l],
                     p["conv_ln_b"][l], p["conv_b_pw"][l], p["mla_kv_norm"][l], zrow[0]], axis=0)
    common = dict(
        w_in=w_in2.astype(BF16), vec=vec, q_norm=p["mla_q_norm"][l].reshape(1, Q_LORA),
        dwk=jnp.concatenate([p["conv_dw_k"][l], zrow], axis=0), w_pw=p["conv_w_pw"][l].astype(BF16),
        w_uq=w_uq.astype(BF16), w_uk=w_uk.astype(BF16), w_uvt=w_uvt.astype(BF16),
        wabs=wabs.astype(BF16), fold=fold.astype(BF16), w_uv=wv.reshape(KV_LORA, W_C).astype(BF16),
        w_out=p["w_out"][l].astype(BF16), ln=jnp.stack([p["post_ln_g"][l], p["post_ln_b"][l]], axis=0))
    ws_p, bs_p = _gmlp_weights(p["gmlp_ws"][l], p["gmlp_bs"][l], chunk_p)
    ws_s, bs_s = _gmlp_weights(p["gmlp_ws"][l], p["gmlp_bs"][l], chunk_s)
    return dict(common, ws=ws_p, bs=bs_p), dict(common, ws=ws_s, bs=bs_s)


def kernel(x_prompt, x_sample, cache_latent, cache_krope, state_conv, c_prompt, c_sample,
           w_ada, b_ada, w_in, gmlp_ln_g, gmlp_ln_b, gmlp_ws, gmlp_bs,
           conv_dw_k, conv_dw_b, conv_ln_g, conv_ln_b, conv_w_pw, conv_b_pw,
           mla_q_norm, mla_w_uq, mla_kv_norm, mla_w_ukv, w_out, post_ln_g, post_ln_b):
    p = dict(w_in=w_in, gmlp_ln_g=gmlp_ln_g, gmlp_ln_b=gmlp_ln_b, gmlp_ws=gmlp_ws, gmlp_bs=gmlp_bs,
             conv_dw_k=conv_dw_k, conv_dw_b=conv_dw_b, conv_ln_g=conv_ln_g, conv_ln_b=conv_ln_b,
             conv_w_pw=conv_w_pw, conv_b_pw=conv_b_pw, mla_q_norm=mla_q_norm, mla_w_uq=mla_w_uq,
             mla_kv_norm=mla_kv_norm, mla_w_ukv=mla_w_ukv, w_out=w_out, post_ln_g=post_ln_g, post_ln_b=post_ln_b)
    depth = w_ada.shape[0]
    bp, seq, _ = x_prompt.shape
    bs, t_new, _ = x_sample.shape
    past_len = cache_latent.shape[2]
    alpha = (2 * depth) ** 0.25
    tile_p = min(IN_TILE, seq)
    chunk_p = min(seq, MLP_CHUNK)
    chunk_s = min(t_new, MLP_CHUNK)

    n_c = bp + bs
    c_all = jnp.concatenate([c_prompt, c_sample, jnp.zeros((-n_c % 8, D_MODEL), F32)], axis=0)
    mod_all = _ada(c_all, w_ada, b_ada)

    ct_p, st_p = _rope_tables(0, seq)
    ct_s, st_s = _rope_tables(past_len, t_new)
    zero_past = jnp.zeros((bp, HIST, W_B), F32)

    xp, xs = x_prompt, x_sample
    p_conv, p_lat, p_kr, s_conv, s_lat, s_kr, s_v = [], [], [], [], [], [], []
    for l in range(depth):
        wp, wsm = _layer_weights(l, p, chunk_p, chunk_s)
        mod_p = mod_all[l, 0:bp].reshape(bp, 1, -1)
        mod_s = mod_all[l, bp:n_c].reshape(bs, 1, -1)

        q, lat, kr, yab, gc, conv, k, vt = _mixer_in(xp, mod_p, ct_p, st_p, zero_past, wp,
                                                      tile=tile_p, chunk=chunk_p, prompt=True)
        o = _attn_prompt(q, k, vt)
        xp = _mixer_out(xp, mod_p, yab, o, gc, wp["w_out"], wp["ln"], tile=tile_p, alpha=alpha)
        p_conv.append(conv[:, HIST - (CONV_W - 1):])
        p_lat.append(lat)
        p_kr.append(kr)

        past = jnp.concatenate([jnp.zeros((bs, HIST - (CONV_W - 1), W_B), F32), state_conv[l]], axis=1)
        q, lat, kr, yab, gc, conv, vst = _mixer_in(xs, mod_s, ct_s, st_s, past, wsm,
                                                   tile=t_new, chunk=chunk_s, prompt=False)
        o = _attn_sample(q, cache_latent, cache_krope, l, lat, kr, wsm)
        xs = _mixer_out(xs, mod_s, yab, o, gc, wsm["w_out"], wsm["ln"], tile=t_new, alpha=alpha)
        s_conv.append(conv[:, HIST - (CONV_W - 1):])
        s_lat.append(lat)
        s_kr.append(kr)
        s_v.append(vst)

    return (xp, xs, jnp.stack(p_conv), jnp.stack(p_lat), jnp.stack(p_kr),
            jnp.stack(s_conv), jnp.stack(s_lat), jnp.stack(s_kr), jnp.stack(s_v))
```

```python
import functools
import math

import jax
import jax.numpy as jnp
from jax import lax
from jax.experimental import pallas as pl
from jax.experimental.pallas import tpu as pltpu

F32 = jnp.float32
BF16 = jnp.bfloat16

D_MODEL = 1024
N_HEADS_A = 4
HEAD_A = 64
W_A = 256
W_B = 256
CONV_W = 31
N_HEADS = 8
QK_NOPE = 64
QK_ROPE = 32
V_DIM = 64
W_C = N_HEADS * V_DIM
Q_LORA = 384
KV_LORA = 256
CHUNK = 64
MLP_CHUNK = 128
ROPE_THETA = 10000.0
ATTN_SCALE = (QK_NOPE + QK_ROPE) ** -0.5
LN_EPS = 1e-5
RMS_EPS = 1e-6

HEAD_LANES = 128
V_ROWS = 80
HIST = 32
ATT_TILE = 256
Q_BLOCK_ROWS = 1024
IN_TILE = 512
VMEM_LIMIT = 56 * 1024 * 1024

_C_A, _C_B, _C_Q, _C_KV, _C_KR, _C_GC, _C_END = 0, 768, 1536, 1920, 2176, 2432, 2944
_Q_SCALE = ATTN_SCALE * math.log2(math.e)


def _dot(a, b):
    return jnp.dot(a, b, preferred_element_type=F32)


def _dot_nt(a, b):
    return lax.dot_general(a, b, (((1,), (1,)), ((), ())), preferred_element_type=F32)


def _norm_rows(x, eps):
    mu = jnp.mean(x, axis=-1, keepdims=True)
    xc = x - mu
    var = jnp.mean(xc * xc, axis=-1, keepdims=True)
    return xc * lax.rsqrt(var + eps)


def _gelu_tanh(x):
    return x * (0.5 * (1.0 + jnp.tanh(math.sqrt(2.0 / math.pi) * (x + 0.044715 * (x * x * x)))))


def _silu(x):
    return x * jax.nn.sigmoid(x)


def _ada_kernel(c_ref, w_ref, b_ref, o_ref):
    c = c_ref[...]
    a = _silu(c)
    w = w_ref[0]
    a_hi = a.astype(BF16)
    a_lo = (a - a_hi.astype(F32)).astype(BF16)
    w_hi = w.astype(BF16)
    w_lo = (w - w_hi.astype(F32)).astype(BF16)
    o_ref[0] = _dot(a_hi, w_hi) + (_dot(a_hi, w_lo) + _dot(a_lo, w_hi)) + b_ref[0]


def _ada(c_all, w_ada, b_ada):
    depth = w_ada.shape[0]
    rows = c_all.shape[0]
    n_col = w_ada.shape[2] // D_MODEL
    return pl.pallas_call(
        _ada_kernel,
        grid=(depth, n_col),
        in_specs=[
            pl.BlockSpec((rows, D_MODEL), lambda l, j: (0, 0)),
            pl.BlockSpec((1, D_MODEL, D_MODEL), lambda l, j: (l, 0, j)),
            pl.BlockSpec((1, 1, D_MODEL), lambda l, j: (l, 0, j)),
        ],
        out_specs=pl.BlockSpec((1, rows, D_MODEL), lambda l, j: (l, 0, j)),
        out_shape=jax.ShapeDtypeStruct((depth, rows, w_ada.shape[2]), F32),
        compiler_params=pltpu.CompilerParams(vmem_limit_bytes=VMEM_LIMIT),
        name="ada_mod",
    )(c_all, w_ada, b_ada.reshape(depth, 1, -1))


def _mixer_in_kernel(x_ref, mod_ref, ct_ref, st_ref, past_ref, w_in_ref, vec_ref, qn_ref, ws_ref, bs_ref,
                     dwk_ref, w_pw_ref, w_uq_ref, *rest, tile, chunk, prompt):
    if prompt:
        (w_uk_ref, w_uvt_ref, q_ref, lat_ref, kr_ref, yab_ref, gc_ref, conv_ref, k_ref, vt_ref, hist) = rest
    else:
        (q_ref, lat_ref, kr_ref, yab_ref, gc_ref, conv_ref, vst_ref, hist) = rest
    t = pl.program_id(1)

    mod = mod_ref[0]
    shift = mod[:, 0:D_MODEL]
    scale = mod[:, D_MODEL:2 * D_MODEL]
    h = _norm_rows(x_ref[0], LN_EPS) * (1.0 + scale) + shift
    hb = h.astype(BF16)

    vec = vec_ref[...]
    gmlp_g, gmlp_b = vec[0:1], vec[1:2]
    dw_b, conv_g, conv_b, b_pw, kv_norm = vec[2:3], vec[3:4], vec[4:5], vec[5:6], vec[6:7]

    za = _dot(hb, w_in_ref[:, _C_A:_C_B])
    u = _gelu_tanh(za[:, 0:W_A])
    v = _norm_rows(_gelu_tanh(za[:, W_A:2 * W_A]), LN_EPS) * gmlp_g + gmlp_b
    if not prompt:
        vst_ref[0] = v
    vb = v.astype(BF16)
    rows = lax.broadcasted_iota(jnp.int32, (N_HEADS_A * chunk, W_A), 0)
    cols = lax.broadcasted_iota(jnp.int32, (N_HEADS_A * chunk, W_A), 1)
    own_head = (rows // chunk) == (cols // HEAD_A)
    ws = ws_ref[...]
    bs = bs_ref[...]
    mixed = []
    for c in range(tile // chunk):
        vc = vb[c * chunk:(c + 1) * chunk]
        vbd = jnp.where(own_head, jnp.concatenate([vc] * N_HEADS_A, axis=0), jnp.zeros((), BF16))
        mixed.append(_dot(ws, vbd) + bs)
    s = mixed[0] if len(mixed) == 1 else jnp.concatenate(mixed, axis=0)
    y_a = u * s * _silu(za[:, 2 * W_A:3 * W_A])
    yab_ref[0, :, 0:W_A] = y_a.astype(BF16)

    zb = _dot(hb, w_in_ref[:, _C_B:_C_Q])
    g = zb[:, 0:W_B] * jax.nn.sigmoid(zb[:, W_B:2 * W_B])

    @pl.when(t == 0)
    def _():
        hist[0:HIST] = past_ref[0]

    hist[HIST:HIST + tile] = g
    dwk = dwk_ref[...]
    y = jnp.broadcast_to(dw_b, (tile, W_B))
    for k in range(CONV_W):
        y = y + dwk[k:k + 1] * hist[HIST - (CONV_W - 1) + k:HIST - (CONV_W - 1) + k + tile]
    new_hist = hist[tile:tile + HIST]
    hist[0:HIST] = new_hist
    conv_ref[0] = new_hist
    y = _silu(_norm_rows(y, LN_EPS) * conv_g + conv_b)
    y = _dot(y.astype(BF16), w_pw_ref[...]) + b_pw
    y_b = y * _silu(zb[:, 2 * W_B:3 * W_B])
    yab_ref[0, :, W_A:W_A + W_B] = y_b.astype(BF16)

    ct = ct_ref[...]
    st = st_ref[...]
    zq = _dot(hb, w_in_ref[:, _C_Q:_C_KV])
    cq = zq * lax.rsqrt(jnp.mean(zq * zq, axis=-1, keepdims=True) + RMS_EPS) * qn_ref[...]
    q2 = _dot(cq.astype(BF16), w_uq_ref[...])
    ctq = ct * _Q_SCALE
    stq = st * _Q_SCALE
    for hd in range(N_HEADS):
        lo = hd * HEAD_LANES
        qh = q2[:, lo:lo + HEAD_LANES] * ctq + q2[:, N_HEADS * HEAD_LANES + lo:N_HEADS * HEAD_LANES + lo + HEAD_LANES] * stq
        q_ref[0, :, lo:lo + HEAD_LANES] = qh.astype(BF16)

    zkv = _dot(hb, w_in_ref[:, _C_KV:_C_KR])
    lat = zkv * lax.rsqrt(jnp.mean(zkv * zkv, axis=-1, keepdims=True) + RMS_EPS) * kv_norm
    lat_ref[0] = lat
    zkr = _dot(hb, w_in_ref[:, _C_KR:_C_GC])
    krot = zkr[:, 0:HEAD_LANES] * ct + zkr[:, HEAD_LANES:2 * HEAD_LANES] * st
    kr_ref[0] = krot[:, QK_NOPE:QK_NOPE + QK_ROPE]

    if prompt:
        latb = lat.astype(BF16)
        kn = _dot(latb, w_uk_ref[...])
        for hd in range(N_HEADS):
            lo = hd * HEAD_LANES
            k_ref[0, :, lo:lo + HEAD_LANES] = (kn[:, lo:lo + HEAD_LANES] + krot).astype(BF16)
        vt = _dot_nt(w_uvt_ref[...], latb)
        vrow = lax.broadcasted_iota(jnp.int32, vt.shape, 0)
        vt = jnp.where((vrow % V_ROWS) == V_DIM, 1.0, vt)
        for c in range(tile // ATT_TILE):
            blk = vt[:, c * ATT_TILE:(c + 1) * ATT_TILE].reshape(N_HEADS, V_ROWS, ATT_TILE)
            vt_ref[0, :, c] = blk.astype(BF16)

    zg = _dot(hb, w_in_ref[:, _C_GC:_C_END])
    gc_ref[0] = _silu(zg).astype(BF16)


def _mixer_in(x, mod, ctab, stab, past, wts, *, tile, chunk, prompt):
    bsz, seq, _ = x.shape
    n_t = seq // tile
    kern = functools.partial(_mixer_in_kernel, tile=tile, chunk=chunk, prompt=prompt)

    def full(a):
        return pl.BlockSpec(a.shape, lambda b, t, _n=a.ndim: (0,) * _n)

    weights = [wts["w_in"], wts["vec"], wts["q_norm"], wts["ws"], wts["bs"], wts["dwk"], wts["w_pw"], wts["w_uq"]]
    if prompt:
        weights += [wts["w_uk"], wts["w_uvt"]]
    in_specs = [
        pl.BlockSpec((1, tile, D_MODEL), lambda b, t: (b, t, 0)),
        pl.BlockSpec((1, 1, 3 * D_MODEL), lambda b, t: (b, 0, 0)),
        pl.BlockSpec((tile, HEAD_LANES), lambda b, t: (t, 0)),
        pl.BlockSpec((tile, HEAD_LANES), lambda b, t: (t, 0)),
        pl.BlockSpec((1, HIST, W_B), lambda b, t: (b, 0, 0)),
    ] + [full(w) for w in weights]
    tok = lambda width: pl.BlockSpec((1, tile, width), lambda b, t: (b, t, 0))
    out_specs = [tok(N_HEADS * HEAD_LANES), tok(KV_LORA), tok(QK_ROPE), tok(W_A + W_B), tok(W_C),
                 pl.BlockSpec((1, HIST, W_B), lambda b, t: (b, 0, 0))]
    out_shape = [
        jax.ShapeDtypeStruct((bsz, seq, N_HEADS * HEAD_LANES), BF16),
        jax.ShapeDtypeStruct((bsz, seq, KV_LORA), F32),
        jax.ShapeDtypeStruct((bsz, seq, QK_ROPE), F32),
        jax.ShapeDtypeStruct((bsz, seq, W_A + W_B), BF16),
        jax.ShapeDtypeStruct((bsz, seq, W_C), BF16),
        jax.ShapeDtypeStruct((bsz, HIST, W_B), F32),
    ]
    if prompt:
        n_kt = tile // ATT_TILE
        out_specs += [tok(N_HEADS * HEAD_LANES),
                      pl.BlockSpec((1, N_HEADS, n_kt, V_ROWS, ATT_TILE), lambda b, t: (b, 0, t, 0, 0))]
        out_shape += [jax.ShapeDtypeStruct((bsz, seq, N_HEADS * HEAD_LANES), BF16),
                      jax.ShapeDtypeStruct((bsz, N_HEADS, seq // ATT_TILE, V_ROWS, ATT_TILE), BF16)]
    else:
        out_specs += [tok(W_A)]
        out_shape += [jax.ShapeDtypeStruct((bsz, seq, W_A), F32)]
    return pl.pallas_call(
        kern,
        grid=(bsz, n_t),
        in_specs=in_specs,
        out_specs=out_specs,
        out_shape=out_shape,
        scratch_shapes=[pltpu.VMEM((HIST + tile, W_B), F32)],
        compiler_params=pltpu.CompilerParams(dimension_semantics=("arbitrary", "arbitrary"),
                                             vmem_limit_bytes=VMEM_LIMIT),
        name="mixer_in_prompt" if prompt else "mixer_in_sample",
    )(x, mod, ctab, stab, past, *weights)


def _attn_prompt_kernel(q_ref, k_ref, vt_ref, o_ref, m_sc, acc_sc, *, n_sub):
    qi = pl.program_id(2)
    kk = lax.broadcasted_iota(jnp.int32, (ATT_TILE, ATT_TILE), 0)
    qq = lax.broadcasted_iota(jnp.int32, (ATT_TILE, ATT_TILE), 1)
    diag_ok = (kk // CHUNK) <= (qq // CHUNK)

    def sub_body(sub, carry):
        row0 = pl.multiple_of(sub * ATT_TILE, ATT_TILE)
        g = qi * n_sub + sub
        qs = [q_ref[0, pl.ds(row0, ATT_TILE), hd * HEAD_LANES:(hd + 1) * HEAD_LANES] for hd in range(2)]
        m_sc[...] = jnp.full(m_sc.shape, -jnp.inf, F32)
        acc_sc[...] = jnp.zeros(acc_sc.shape, F32)

        def scores(ki):
            k0 = pl.multiple_of(ki * ATT_TILE, ATT_TILE)
            return tuple(_dot_nt(k_ref[0, pl.ds(k0, ATT_TILE), hd * HEAD_LANES:(hd + 1) * HEAD_LANES], qs[hd])
                         for hd in range(2))

        def accumulate(ki, s_pair, masked):
            for hd in range(2):
                s = s_pair[hd]
                if masked:
                    s = jnp.where(diag_ok, s, -jnp.inf)
                m_prev = m_sc[hd]
                m_new = jnp.maximum(m_prev, jnp.max(s, axis=0, keepdims=True))
                p = jnp.exp2(s - m_new).astype(BF16)
                alpha = jnp.exp2(m_prev - m_new)
                acc_sc[hd] = alpha * acc_sc[hd] + _dot(vt_ref[0, hd, ki], p)
                m_sc[hd] = m_new

        def k_body(ki, s_pair):
            s_next = scores(ki + 1)
            accumulate(ki, s_pair, False)
            return s_next

        s_last = lax.fori_loop(0, g, k_body, scores(0))
        accumulate(g, s_last, True)

        outs = []
        for hd in range(2):
            acc = acc_sc[hd]
            outs.append(acc[0:V_DIM] * (1.0 / acc[V_DIM:V_DIM + 1]))
        o_t = jnp.concatenate(outs, axis=0)
        o_ref[0, pl.ds(row0, ATT_TILE), :] = o_t.T.astype(BF16)
        return carry

    lax.fori_loop(0, n_sub, sub_body, 0)


def _attn_prompt(q, k, vt):
    bsz, seq, _ = q.shape
    qb = min(Q_BLOCK_ROWS, seq)
    n_sub = qb // ATT_TILE
    n_kt = seq // ATT_TILE
    kern = functools.partial(_attn_prompt_kernel, n_sub=n_sub)
    return pl.pallas_call(
        kern,
        grid=(bsz, N_HEADS // 2, seq // qb),
        in_specs=[
            pl.BlockSpec((1, qb, 2 * HEAD_LANES), lambda b, hp, i: (b, i, hp)),
            pl.BlockSpec((1, seq, 2 * HEAD_LANES), lambda b, hp, i: (b, 0, hp)),
            pl.BlockSpec((1, 2, n_kt, V_ROWS, ATT_TILE), lambda b, hp, i: (b, hp, 0, 0, 0)),
        ],
        out_specs=pl.BlockSpec((1, qb, 2 * V_DIM), lambda b, hp, i: (b, i, hp)),
        out_shape=jax.ShapeDtypeStruct((bsz, seq, W_C), BF16),
        scratch_shapes=[pltpu.VMEM((2, 1, ATT_TILE), F32), pltpu.VMEM((2, V_ROWS, ATT_TILE), F32)],
        compiler_params=pltpu.CompilerParams(dimension_semantics=("arbitrary", "arbitrary", "arbitrary"),
                                             vmem_limit_bytes=VMEM_LIMIT),
        name="attn_prompt",
    )(q, k, vt)


def _attn_sample_kernel(q_ref, latc_ref, krc_ref, latn_ref, krn_ref, wabs_ref, fold_ref, wuv_ref, o_ref, *, t_new):
    n_rows = N_HEADS * t_new
    q = q_ref[0]
    qrep = jnp.concatenate([q] * N_HEADS, axis=0)
    r1 = lax.broadcasted_iota(jnp.int32, qrep.shape, 0)
    c1 = lax.broadcasted_iota(jnp.int32, qrep.shape, 1)
    qrep = jnp.where((r1 // t_new) == (c1 // HEAD_LANES), qrep, jnp.zeros((), BF16))
    qlat = _dot(qrep, wabs_ref[...]).astype(BF16)
    qr = _dot(qrep, fold_ref[...])[:, 0:QK_ROPE].astype(BF16)
    latc = latc_ref[0, 0].astype(BF16)
    krc = krc_ref[0, 0].astype(BF16)
    latn = latn_ref[0].astype(BF16)
    krn = krn_ref[0].astype(BF16)
    s_c = _dot_nt(qlat, latc) + _dot_nt(qr, krc)
    s_n = _dot_nt(qlat, latn) + _dot_nt(qr, krn)
    m = jnp.maximum(jnp.max(s_c, axis=-1, keepdims=True), jnp.max(s_n, axis=-1, keepdims=True))
    p_c = jnp.exp2(s_c - m)
    p_n = jnp.exp2(s_n - m)
    den = jnp.sum(p_c, axis=-1, keepdims=True) + jnp.sum(p_n, axis=-1, keepdims=True)
    olat = (_dot(p_c.astype(BF16), latc) + _dot(p_n.astype(BF16), latn)) * (1.0 / den)
    of = _dot(olat.astype(BF16), wuv_ref[...])
    r2 = lax.broadcasted_iota(jnp.int32, of.shape, 0)
    c2 = lax.broadcasted_iota(jnp.int32, of.shape, 1)
    of = jnp.where((r2 // t_new) == (c2 // V_DIM), of, 0.0)
    o = of[0:t_new]
    for hd in range(1, N_HEADS):
        o = o + of[hd * t_new:(hd + 1) * t_new]
    o_ref[0] = o.astype(BF16)
    del n_rows


def _attn_sample(q, lat_cache, kr_cache, layer, lat_new, kr_new, wts):
    bsz, t_new, _ = q.shape
    past = lat_cache.shape[2]
    kern = functools.partial(_attn_sample_kernel, t_new=t_new)

    def full(a):
        return pl.BlockSpec(a.shape, lambda b, _n=a.ndim: (0,) * _n)

    return pl.pallas_call(
        kern,
        grid=(bsz,),
        in_specs=[
            pl.BlockSpec((1, t_new, N_HEADS * HEAD_LANES), lambda b: (b, 0, 0)),
            pl.BlockSpec((1, 1, past, KV_LORA), lambda b: (layer, b, 0, 0)),
            pl.BlockSpec((1, 1, past, QK_ROPE), lambda b: (layer, b, 0, 0)),
            pl.BlockSpec((1, t_new, KV_LORA), lambda b: (b, 0, 0)),
            pl.BlockSpec((1, t_new, QK_ROPE), lambda b: (b, 0, 0)),
            full(wts["wabs"]), full(wts["fold"]), full(wts["w_uv"]),
        ],
        out_specs=pl.BlockSpec((1, t_new, W_C), lambda b: (b, 0, 0)),
        out_shape=jax.ShapeDtypeStruct((bsz, t_new, W_C), BF16),
        compiler_params=pltpu.CompilerParams(dimension_semantics=("arbitrary",), vmem_limit_bytes=VMEM_LIMIT),
        name="attn_sample",
    )(q, lat_cache, kr_cache, lat_new, kr_new, wts["wabs"], wts["fold"], wts["w_uv"])


def _mixer_out_kernel(x_ref, mod_ref, yab_ref, o_ref, gc_ref, w_out_ref, ln_ref, out_ref, *, alpha):
    gate = mod_ref[0][:, 2 * D_MODEL:3 * D_MODEL]
    yc = (o_ref[0].astype(F32) * gc_ref[0].astype(F32)).astype(BF16)
    y = _dot(yab_ref[0], w_out_ref[0:W_A + W_B]) + _dot(yc, w_out_ref[W_A + W_B:W_A + W_B + W_C])
    r = alpha * x_ref[0] + gate * y
    ln = ln_ref[...]
    out_ref[0] = _norm_rows(r, LN_EPS) * ln[0:1] + ln[1:2]


def _mixer_out(x, mod, yab, o, gc, w_out, ln, *, tile, alpha):
    bsz, seq, _ = x.shape
    tok = lambda width: pl.BlockSpec((1, tile, width), lambda b, t: (b, t, 0))
    return pl.pallas_call(
        functools.partial(_mixer_out_kernel, alpha=alpha),
        grid=(bsz, seq // tile),
        in_specs=[tok(D_MODEL), pl.BlockSpec((1, 1, 3 * D_MODEL), lambda b, t: (b, 0, 0)),
                  tok(W_A + W_B), tok(W_C), tok(W_C),
                  pl.BlockSpec(w_out.shape, lambda b, t: (0, 0)),
                  pl.BlockSpec(ln.shape, lambda b, t: (0, 0))],
        out_specs=tok(D_MODEL),
        out_shape=jax.ShapeDtypeStruct(x.shape, F32),
        compiler_params=pltpu.CompilerParams(dimension_semantics=("arbitrary", "arbitrary"),
                                             vmem_limit_bytes=VMEM_LIMIT),
        name="mixer_out",
    )(x, mod, yab, o, gc, w_out, ln)


def _rope_tables(pos0, n):
    inv = ROPE_THETA ** (-jnp.arange(0, QK_ROPE, 2, dtype=F32) / QK_ROPE)
    ang = (pos0 + jnp.arange(n, dtype=jnp.int32)).astype(F32)[:, None] * inv[None, :]
    cos, sin = jnp.cos(ang), jnp.sin(ang)
    ones = jnp.ones((n, QK_NOPE), F32)
    z64 = jnp.zeros((n, QK_NOPE), F32)
    z32 = jnp.zeros((n, HEAD_LANES - QK_NOPE - QK_ROPE), F32)
    return jnp.concatenate([ones, cos, cos, z32], axis=1), jnp.concatenate([z64, sin, sin, z32], axis=1)


def _gmlp_weights(ws, b_s, chunk):
    idx = jnp.arange(chunk)
    mask = (idx[None, :] // CHUNK) <= (idx[:, None] // CHUNK)
    wsm = jnp.where(mask[None], ws[:, :chunk, :chunk], 0.0)
    ws_all = jnp.transpose(wsm, (1, 0, 2)).reshape(chunk, N_HEADS_A * chunk)
    bs_tab = jnp.repeat(b_s[:, :chunk].T, HEAD_A, axis=1)
    return ws_all.astype(BF16), bs_tab.astype(F32)


def _layer_weights(l, p, chunk_p, chunk_s):
    half = QK_ROPE // 2
    w_in = p["w_in"][l]
    krw = w_in[:, 2176:2208]
    x1, x2 = krw[:, :half], krw[:, half:]
    z64 = jnp.zeros((D_MODEL, QK_NOPE), F32)
    z32 = jnp.zeros((D_MODEL, HEAD_LANES - QK_NOPE - QK_ROPE), F32)
    w_in2 = jnp.concatenate([w_in[:, :2176], z64, x1, x2, z32, z64, -x2, x1, z32, w_in[:, 2208:]], axis=1)

    wq = p["mla_w_uq"][l].reshape(Q_LORA, N_HEADS, QK_NOPE + QK_ROPE)
    qn, q1, q2 = wq[..., :QK_NOPE], wq[..., QK_NOPE:QK_NOPE + half], wq[..., QK_NOPE + half:]
    zq32 = jnp.zeros((Q_LORA, N_HEADS, HEAD_LANES - QK_NOPE - QK_ROPE), F32)
    zq64 = jnp.zeros((Q_LORA, N_HEADS, QK_NOPE), F32)
    w_uq = jnp.concatenate([
        jnp.concatenate([qn, q1, q2, zq32], axis=-1).reshape(Q_LORA, -1),
        jnp.concatenate([zq64, -q2, q1, zq32], axis=-1).reshape(Q_LORA, -1)], axis=1)

    wkv = p["mla_w_ukv"][l].reshape(KV_LORA, N_HEADS, QK_NOPE + V_DIM)
    wk, wv = wkv[..., :QK_NOPE], wkv[..., QK_NOPE:]
    w_uk = jnp.concatenate([wk, jnp.zeros_like(wk)], axis=-1).reshape(KV_LORA, -1)
    wvt = jnp.transpose(wv, (1, 2, 0))
    w_uvt = jnp.concatenate([wvt, jnp.zeros((N_HEADS, V_ROWS - V_DIM, KV_LORA), F32)], axis=1).reshape(-1, KV_LORA)
    wkt = jnp.transpose(wk, (1, 2, 0))
    wabs = jnp.concatenate([wkt, jnp.zeros((N_HEADS, HEAD_LANES - QK_NOPE, KV_LORA), F32)], axis=1).reshape(-1, KV_LORA)
    eye = jnp.eye(QK_ROPE, HEAD_LANES, dtype=F32)
    fold_h = jnp.concatenate([jnp.zeros((QK_NOPE, HEAD_LANES), F32), eye,
                              jnp.zeros((HEAD_LANES - QK_NOPE - QK_ROPE, HEAD_LANES), F32)], axis=0)
    fold = jnp.tile(fold_h, (N_HEADS, 1))

    zrow = jnp.zeros((1, W_B), F32)
    vec = jnp.stack([p["gmlp_ln_g"][l], p["gmlp_ln_b"][l], p["conv_dw_b"][l], p["conv_ln_g"][l],
                     p["conv_ln_b"][l], p["conv_b_pw"][l], p["mla_kv_norm"][l], zrow[0]], axis=0)
    common = dict(
        w_in=w_in2.astype(BF16), vec=vec, q_norm=p["mla_q_norm"][l].reshape(1, Q_LORA),
        dwk=jnp.concatenate([p["conv_dw_k"][l], zrow], axis=0), w_pw=p["conv_w_pw"][l].astype(BF16),
        w_uq=w_uq.astype(BF16), w_uk=w_uk.astype(BF16), w_uvt=w_uvt.astype(BF16),
        wabs=wabs.astype(BF16), fold=fold.astype(BF16), w_uv=wv.reshape(KV_LORA, W_C).astype(BF16),
        w_out=p["w_out"][l].astype(BF16), ln=jnp.stack([p["post_ln_g"][l], p["post_ln_b"][l]], axis=0))
    ws_p, bs_p = _gmlp_weights(p["gmlp_ws"][l], p["gmlp_bs"][l], chunk_p)
    ws_s, bs_s = _gmlp_weights(p["gmlp_ws"][l], p["gmlp_bs"][l], chunk_s)
    return dict(common, ws=ws_p, bs=bs_p), dict(common, ws=ws_s, bs=bs_s)


def kernel(x_prompt, x_sample, cache_latent, cache_krope, state_conv, c_prompt, c_sample,
           w_ada, b_ada, w_in, gmlp_ln_g, gmlp_ln_b, gmlp_ws, gmlp_bs,
           conv_dw_k, conv_dw_b, conv_ln_g, conv_ln_b, conv_w_pw, conv_b_pw,
           mla_q_norm, mla_w_uq, mla_kv_norm, mla_w_ukv, w_out, post_ln_g, post_ln_b):
    p = dict(w_in=w_in, gmlp_ln_g=gmlp_ln_g, gmlp_ln_b=gmlp_ln_b, gmlp_ws=gmlp_ws, gmlp_bs=gmlp_bs,
             conv_dw_k=conv_dw_k, conv_dw_b=conv_dw_b, conv_ln_g=conv_ln_g, conv_ln_b=conv_ln_b,
             conv_w_pw=conv_w_pw, conv_b_pw=conv_b_pw, mla_q_norm=mla_q_norm, mla_w_uq=mla_w_uq,
             mla_kv_norm=mla_kv_norm, mla_w_ukv=mla_w_ukv, w_out=w_out, post_ln_g=post_ln_g, post_ln_b=post_ln_b)
    depth = w_ada.shape[0]
    bp, seq, _ = x_prompt.shape
    bs, t_new, _ = x_sample.shape
    past_len = cache_latent.shape[2]
    alpha = (2 * depth) ** 0.25
    tile_p = min(IN_TILE, seq)
    chunk_p = min(seq, MLP_CHUNK)
    chunk_s = min(t_new, MLP_CHUNK)

    n_c = bp + bs
    c_all = jnp.concatenate([c_prompt, c_sample, jnp.zeros((-n_c % 8, D_MODEL), F32)], axis=0)
    mod_all = _ada(c_all, w_ada, b_ada)

    ct_p, st_p = _rope_tables(0, seq)
    ct_s, st_s = _rope_tables(past_len, t_new)
    zero_past = jnp.zeros((bp, HIST, W_B), F32)

    xp, xs = x_prompt, x_sample
    p_conv, p_lat, p_kr, s_conv, s_lat, s_kr, s_v = [], [], [], [], [], [], []
    for l in range(depth):
        wp, wsm = _layer_weights(l, p, chunk_p, chunk_s)
        mod_p = mod_all[l, 0:bp].reshape(bp, 1, -1)
        mod_s = mod_all[l, bp:n_c].reshape(bs, 1, -1)

        q, lat, kr, yab, gc, conv, k, vt = _mixer_in(xp, mod_p, ct_p, st_p, zero_past, wp,
                                                      tile=tile_p, chunk=chunk_p, prompt=True)
        o = _attn_prompt(q, k, vt)
        xp = _mixer_out(xp, mod_p, yab, o, gc, wp["w_out"], wp["ln"], tile=tile_p, alpha=alpha)
        p_conv.append(conv[:, HIST - (CONV_W - 1):])
        p_lat.append(lat)
        p_kr.append(kr)

        past = jnp.concatenate([jnp.zeros((bs, HIST - (CONV_W - 1), W_B), F32), state_conv[l]], axis=1)
        q, lat, kr, yab, gc, conv, vst = _mixer_in(xs, mod_s, ct_s, st_s, past, wsm,
                                                   tile=t_new, chunk=chunk_s, prompt=False)
        o = _attn_sample(q, cache_latent, cache_krope, l, lat, kr, wsm)
        xs = _mixer_out(xs, mod_s, yab, o, gc, wsm["w_out"], wsm["ln"], tile=t_new, alpha=alpha)
        s_conv.append(conv[:, HIST - (CONV_W - 1):])
        s_lat.append(lat)
        s_kr.append(kr)
        s_v.append(vst)

    return (xp, xs, jnp.stack(p_conv), jnp.stack(p_lat), jnp.stack(p_kr),
            jnp.stack(s_conv), jnp.stack(s_lat), jnp.stack(s_kr), jnp.stack(s_v))
```

```python
import functools
import math

import jax
import jax.numpy as jnp
from jax import lax
from jax.experimental import pallas as pl
from jax.experimental.pallas import tpu as pltpu

F32 = jnp.float32
BF16 = jnp.bfloat16

D_MODEL = 1024
N_HEADS_A = 4
HEAD_A = 64
W_A = 256
W_B = 256
CONV_W = 31
N_HEADS = 8
QK_NOPE = 64
QK_ROPE = 32
V_DIM = 64
W_C = N_HEADS * V_DIM
Q_LORA = 384
KV_LORA = 256
CHUNK = 64
MLP_CHUNK = 128
ROPE_THETA = 10000.0
ATTN_SCALE = (QK_NOPE + QK_ROPE) ** -0.5
LN_EPS = 1e-5
RMS_EPS = 1e-6

HEAD_LANES = 128
V_ROWS = 80
HIST = 32
ATT_TILE = 256
Q_BLOCK_ROWS = 1024
IN_TILE = 512
VMEM_LIMIT = 56 * 1024 * 1024

_C_A, _C_B, _C_Q, _C_KV, _C_KR, _C_GC, _C_END = 0, 768, 1536, 1920, 2176, 2432, 2944
_Q_SCALE = ATTN_SCALE * math.log2(math.e)


def _dot(a, b):
    return jnp.dot(a, b, preferred_element_type=F32)


def _dot_nt(a, b):
    return lax.dot_general(a, b, (((1,), (1,)), ((), ())), preferred_element_type=F32)


def _norm_rows(x, eps):
    mu = jnp.mean(x, axis=-1, keepdims=True)
    xc = x - mu
    var = jnp.mean(xc * xc, axis=-1, keepdims=True)
    return xc * lax.rsqrt(var + eps)


def _gelu_tanh(x):
    return x * (0.5 * (1.0 + jnp.tanh(math.sqrt(2.0 / math.pi) * (x + 0.044715 * (x * x * x)))))


def _silu(x):
    return x * jax.nn.sigmoid(x)


def _ada_kernel(c_ref, w_ref, b_ref, o_ref):
    c = c_ref[...]
    a = _silu(c)
    w = w_ref[0]
    a_hi = a.astype(BF16)
    a_lo = (a - a_hi.astype(F32)).astype(BF16)
    w_hi = w.astype(BF16)
    w_lo = (w - w_hi.astype(F32)).astype(BF16)
    o_ref[0] = _dot(a_hi, w_hi) + (_dot(a_hi, w_lo) + _dot(a_lo, w_hi)) + b_ref[0]


def _ada(c_all, w_ada, b_ada):
    depth = w_ada.shape[0]
    rows = c_all.shape[0]
    n_col = w_ada.shape[2] // D_MODEL
    return pl.pallas_call(
        _ada_kernel,
        grid=(depth, n_col),
        in_specs=[
            pl.BlockSpec((rows, D_MODEL), lambda l, j: (0, 0)),
            pl.BlockSpec((1, D_MODEL, D_MODEL), lambda l, j: (l, 0, j)),
            pl.BlockSpec((1, 1, D_MODEL), lambda l, j: (l, 0, j)),
        ],
        out_specs=pl.BlockSpec((1, rows, D_MODEL), lambda l, j: (l, 0, j)),
        out_shape=jax.ShapeDtypeStruct((depth, rows, w_ada.shape[2]), F32),
        compiler_params=pltpu.CompilerParams(vmem_limit_bytes=VMEM_LIMIT),
        name="ada_mod",
    )(c_all, w_ada, b_ada.reshape(depth, 1, -1))


def _mixer_in_kernel(x_ref, mod_ref, ct_ref, st_ref, past_ref, w_in_ref, vec_ref, qn_ref, ws_ref, bs_ref,
                     dwk_ref, w_pw_ref, w_uq_ref, *rest, tile, chunk, prompt):
    if prompt:
        (w_uk_ref, w_uvt_ref, q_ref, lat_ref, kr_ref, yab_ref, gc_ref, conv_ref, k_ref, vt_ref, hist) = rest
    else:
        (q_ref, lat_ref, kr_ref, yab_ref, gc_ref, conv_ref, vst_ref, hist) = rest
    t = pl.program_id(1)

    mod = mod_ref[0]
    shift = mod[:, 0:D_MODEL]
    scale = mod[:, D_MODEL:2 * D_MODEL]
    h = _norm_rows(x_ref[0], LN_EPS) * (1.0 + scale) + shift
    hb = h.astype(BF16)

    vec = vec_ref[...]
    gmlp_g, gmlp_b = vec[0:1], vec[1:2]
    dw_b, conv_g, conv_b, b_pw, kv_norm = vec[2:3], vec[3:4], vec[4:5], vec[5:6], vec[6:7]

    za = _dot(hb, w_in_ref[:, _C_A:_C_B])
    u = _gelu_tanh(za[:, 0:W_A])
    v = _norm_rows(_gelu_tanh(za[:, W_A:2 * W_A]), LN_EPS) * gmlp_g + gmlp_b
    if not prompt:
        vst_ref[0] = v
    vb = v.astype(BF16)
    rows = lax.broadcasted_iota(jnp.int32, (N_HEADS_A * chunk, W_A), 0)
    cols = lax.broadcasted_iota(jnp.int32, (N_HEADS_A * chunk, W_A), 1)
    own_head = (rows // chunk) == (cols // HEAD_A)
    ws = ws_ref[...]
    bs = bs_ref[...]
    mixed = []
    for c in range(tile // chunk):
        vc = vb[c * chunk:(c + 1) * chunk]
        vbd = jnp.where(own_head, jnp.concatenate([vc] * N_HEADS_A, axis=0), jnp.zeros((), BF16))
        mixed.append(_dot(ws, vbd) + bs)
    s = mixed[0] if len(mixed) == 1 else jnp.concatenate(mixed, axis=0)
    y_a = u * s * _silu(za[:, 2 * W_A:3 * W_A])
    yab_ref[0, :, 0:W_A] = y_a.astype(BF16)

    zb = _dot(hb, w_in_ref[:, _C_B:_C_Q])
    g = zb[:, 0:W_B] * jax.nn.sigmoid(zb[:, W_B:2 * W_B])

    @pl.when(t == 0)
    def _():
        hist[0:HIST] = past_ref[0]

    hist[HIST:HIST + tile] = g
    dwk = dwk_ref[...]
    y = jnp.broadcast_to(dw_b, (tile, W_B))
    for k in range(CONV_W):
        y = y + dwk[k:k + 1] * hist[HIST - (CONV_W - 1) + k:HIST - (CONV_W - 1) + k + tile]
    new_hist = hist[tile:tile + HIST]
    hist[0:HIST] = new_hist
    conv_ref[0] = new_hist
    y = _silu(_norm_rows(y, LN_EPS) * conv_g + conv_b)
    y = _dot(y.astype(BF16), w_pw_ref[...]) + b_pw
    y_b = y * _silu(zb[:, 2 * W_B:3 * W_B])
    yab_ref[0, :, W_A:W_A + W_B] = y_b.astype(BF16)

    ct = ct_ref[...]
    st = st_ref[...]
    zq = _dot(hb, w_in_ref[:, _C_Q:_C_KV])
    cq = zq * lax.rsqrt(jnp.mean(zq * zq, axis=-1, keepdims=True) + RMS_EPS) * qn_ref[...]
    q2 = _dot(cq.astype(BF16), w_uq_ref[...])
    ctq = ct * _Q_SCALE
    stq = st * _Q_SCALE
    for hd in range(N_HEADS):
        lo = hd * HEAD_LANES
        qh = q2[:, lo:lo + HEAD_LANES] * ctq + q2[:, N_HEADS * HEAD_LANES + lo:N_HEADS * HEAD_LANES + lo + HEAD_LANES] * stq
        q_ref[0, :, lo:lo + HEAD_LANES] = qh.astype(BF16)

    zkv = _dot(hb, w_in_ref[:, _C_KV:_C_KR])
    lat = zkv * lax.rsqrt(jnp.mean(zkv * zkv, axis=-1, keepdims=True) + RMS_EPS) * kv_norm
    lat_ref[0] = lat
    zkr = _dot(hb, w_in_ref[:, _C_KR:_C_GC])
    krot = zkr[:, 0:HEAD_LANES] * ct + zkr[:, HEAD_LANES:2 * HEAD_LANES] * st
    kr_ref[0] = krot[:, QK_NOPE:QK_NOPE + QK_ROPE]

    if prompt:
        latb = lat.astype(BF16)
        kn = _dot(latb, w_uk_ref[...])
        for hd in range(N_HEADS):
            lo = hd * HEAD_LANES
            k_ref[0, :, lo:lo + HEAD_LANES] = (kn[:, lo:lo + HEAD_LANES] + krot).astype(BF16)
        vt = _dot_nt(w_uvt_ref[...], latb)
        vrow = lax.broadcasted_iota(jnp.int32, vt.shape, 0)
        vt = jnp.where((vrow % V_ROWS) == V_DIM, 1.0, vt)
        for c in range(tile // ATT_TILE):
            blk = vt[:, c * ATT_TILE:(c + 1) * ATT_TILE].reshape(N_HEADS, V_ROWS, ATT_TILE)
            vt_ref[0, :, c] = blk.astype(BF16)

    zg = _dot(hb, w_in_ref[:, _C_GC:_C_END])
    gc_ref[0] = _silu(zg).astype(BF16)


def _mixer_in(x, mod, ctab, stab, past, wts, *, tile, chunk, prompt):
    bsz, seq, _ = x.shape
    n_t = seq // tile
    kern = functools.partial(_mixer_in_kernel, tile=tile, chunk=chunk, prompt=prompt)

    def full(a):
        return pl.BlockSpec(a.shape, lambda b, t, _n=a.ndim: (0,) * _n)

    weights = [wts["w_in"], wts["vec"], wts["q_norm"], wts["ws"], wts["bs"], wts["dwk"], wts["w_pw"], wts["w_uq"]]
    if prompt:
        weights += [wts["w_uk"], wts["w_uvt"]]
    in_specs = [
        pl.BlockSpec((1, tile, D_MODEL), lambda b, t: (b, t, 0)),
        pl.BlockSpec((1, 1, 3 * D_MODEL), lambda b, t: (b, 0, 0)),
        pl.BlockSpec((tile, HEAD_LANES), lambda b, t: (t, 0)),
        pl.BlockSpec((tile, HEAD_LANES), lambda b, t: (t, 0)),
        pl.BlockSpec((1, HIST, W_B), lambda b, t: (b, 0, 0)),
    ] + [full(w) for w in weights]
    tok = lambda width: pl.BlockSpec((1, tile, width), lambda b, t: (b, t, 0))
    out_specs = [tok(N_HEADS * HEAD_LANES), tok(KV_LORA), tok(QK_ROPE), tok(W_A + W_B), tok(W_C),
                 pl.BlockSpec((1, HIST, W_B), lambda b, t: (b, 0, 0))]
    out_shape = [
        jax.ShapeDtypeStruct((bsz, seq, N_HEADS * HEAD_LANES), BF16),
        jax.ShapeDtypeStruct((bsz, seq, KV_LORA), F32),
        jax.ShapeDtypeStruct((bsz, seq, QK_ROPE), F32),
        jax.ShapeDtypeStruct((bsz, seq, W_A + W_B), BF16),
        jax.ShapeDtypeStruct((bsz, seq, W_C), BF16),
        jax.ShapeDtypeStruct((bsz, HIST, W_B), F32),
    ]
    if prompt:
        n_kt = tile // ATT_TILE
        out_specs += [tok(N_HEADS * HEAD_LANES),
                      pl.BlockSpec((1, N_HEADS, n_kt, V_ROWS, ATT_TILE), lambda b, t: (b, 0, t, 0, 0))]
        out_shape += [jax.ShapeDtypeStruct((bsz, seq, N_HEADS * HEAD_LANES), BF16),
                      jax.ShapeDtypeStruct((bsz, N_HEADS, seq // ATT_TILE, V_ROWS, ATT_TILE), BF16)]
    else:
        out_specs += [tok(W_A)]
        out_shape += [jax.ShapeDtypeStruct((bsz, seq, W_A), F32)]
    return pl.pallas_call(
        kern,
        grid=(bsz, n_t),
        in_specs=in_specs,
        out_specs=out_specs,
        out_shape=out_shape,
        scratch_shapes=[pltpu.VMEM((HIST + tile, W_B), F32)],
        compiler_params=pltpu.CompilerParams(dimension_semantics=("arbitrary", "arbitrary"),
                                             vmem_limit_bytes=VMEM_LIMIT),
        name="mixer_in_prompt" if prompt else "mixer_in_sample",
    )(x, mod, ctab, stab, past, *weights)


_S_ADDR = (0, 64)
_O_ADDR = (128, 148)
_FIXED_TILES = 5


def _attn_prompt_kernel(q_ref, k_ref, vt_ref, o_ref, qt_sc, s_sc, acc_sc, *, n_sub, n_kt):
    qi = pl.program_id(2)
    kk = lax.broadcasted_iota(jnp.int32, (ATT_TILE, ATT_TILE), 0)
    qq = lax.broadcasted_iota(jnp.int32, (ATT_TILE, ATT_TILE), 1)
    chunk_lead = qq // CHUNK - kk // CHUNK
    dim_row = lax.broadcasted_iota(jnp.int32, (2 * HEAD_LANES, ATT_TILE), 0)

    def sub_body(sub, carry_unused):
        row0 = pl.multiple_of(sub * ATT_TILE, ATT_TILE)
        g = qi * n_sub + sub
        n_steady = jnp.maximum((g - 3) // 2, 0)
        n_tiles = _FIXED_TILES + 2 * n_steady

        q_t = q_ref[0, pl.ds(row0, ATT_TILE), :].astype(F32).T
        for hd in range(2):
            qt_sc[hd] = jnp.where((dim_row // HEAD_LANES) == hd, q_t, 0.0).astype(BF16)
        acc_sc[...] = jnp.zeros(acc_sc.shape, F32)

        def tile_start(t):
            return pl.multiple_of(jnp.minimum(t, n_kt - 1) * ATT_TILE, ATT_TILE)

        def stage_a(t, par, restage):
            k_tile = k_ref[0, pl.ds(tile_start(t), ATT_TILE), :]
            for hd in range(2):
                pltpu.matmul_acc_lhs(_S_ADDR[par], k_tile, mxu_index=hd, load_staged_rhs=0)
                if restage:
                    pltpu.matmul_push_rhs(qt_sc[hd], staging_register=0, mxu_index=hd)

        def stage_b(t, par, masked, m_old):
            m_new, alpha = [], []
            if masked:
                lead = jnp.where(t < g, 4, jnp.where(t == g, 0, -4))
                visible = (chunk_lead + lead) >= 0
            for hd in range(2):
                s = pltpu.matmul_pop(_S_ADDR[par], (ATT_TILE, ATT_TILE), F32, mxu_index=hd)
                if masked:
                    s = jnp.where(visible, s, -jnp.inf)
                m_hd = jnp.maximum(m_old[hd], jnp.max(s, axis=0, keepdims=True))
                alpha.append(jnp.exp2(m_old[hd] - m_hd))
                m_new.append(m_hd)
                s_sc[hd, par] = s
            return m_new, alpha

        def stage_c(t, par, m_t):
            kt = jnp.minimum(t, n_kt - 1)
            for hd in range(2):
                p = jnp.exp2(s_sc[hd, par] - m_t[hd]).astype(BF16)
                pltpu.matmul_push_rhs(p, staging_register=1, mxu_index=hd)
                pltpu.matmul_acc_lhs(_O_ADDR[par], vt_ref[0, hd, kt], mxu_index=hd, load_staged_rhs=1)

        def stage_d(par, alpha_t):
            for hd in range(2):
                o = pltpu.matmul_pop(_O_ADDR[par], (V_ROWS, ATT_TILE), F32, mxu_index=hd)
                acc_sc[hd] = alpha_t[hd] * acc_sc[hd] + o

        def iteration(j, par, state, *, a=True, b=True, c=True, d=True, masked=False, restage=True):
            m, al1, al2, al3 = state
            if a:
                stage_a(j + 2, par, restage)
            if d:
                stage_d(par, al3)
            al0 = al1
            m_next = m
            if b:
                m_next, al0 = stage_b(j + 1, 1 - par, masked, m)
            if c:
                stage_c(j, par, m)
            return (m_next, al0, al1, al2)

        neg = jnp.full((1, ATT_TILE), -jnp.inf, F32)
        one = jnp.ones((1, ATT_TILE), F32)
        state = ([neg, neg], [one, one], [one, one], [one, one])

        for hd in range(2):
            pltpu.matmul_push_rhs(qt_sc[hd], staging_register=0, mxu_index=hd)
        state = iteration(-2, 0, state, b=False, c=False, d=False)
        state = iteration(-1, 1, state, c=False, d=False, masked=True)
        state = iteration(0, 0, state, d=False, masked=True)
        state = iteration(1, 1, state, d=False, masked=True)

        def steady(u, st):
            j = 2 + 2 * u
            st = iteration(j, 0, st)
            return iteration(j + 1, 1, st)

        state = lax.fori_loop(0, n_steady, steady, state)

        j0 = n_tiles - 3
        state = iteration(j0, 0, state, masked=True, restage=False)
        state = iteration(j0 + 1, 1, state, a=False, masked=True)
        state = iteration(j0 + 2, 0, state, a=False, b=False)
        state = iteration(j0 + 3, 1, state, a=False, b=False, c=False)
        state = iteration(j0 + 4, 0, state, a=False, b=False, c=False)

        outs = []
        for hd in range(2):
            acc = acc_sc[hd]
            outs.append(acc[0:V_DIM] * (1.0 / acc[V_DIM:V_DIM + 1]))
        o_t = jnp.concatenate(outs, axis=0)
        o_ref[0, pl.ds(row0, ATT_TILE), :] = o_t.T.astype(BF16)
        return carry_unused

    lax.fori_loop(0, n_sub, sub_body, 0)


def _attn_prompt(q, k, vt):
    bsz, seq, _ = q.shape
    qb = min(Q_BLOCK_ROWS, seq)
    n_sub = qb // ATT_TILE
    n_kt = seq // ATT_TILE
    kern = functools.partial(_attn_prompt_kernel, n_sub=n_sub, n_kt=n_kt)
    return pl.pallas_call(
        kern,
        grid=(bsz, N_HEADS // 2, seq // qb),
        in_specs=[
            pl.BlockSpec((1, qb, 2 * HEAD_LANES), lambda b, hp, i: (b, i, hp)),
            pl.BlockSpec((1, seq, 2 * HEAD_LANES), lambda b, hp, i: (b, 0, hp)),
            pl.BlockSpec((1, 2, n_kt, V_ROWS, ATT_TILE), lambda b, hp, i: (b, hp, 0, 0, 0)),
        ],
        out_specs=pl.BlockSpec((1, qb, 2 * V_DIM), lambda b, hp, i: (b, i, hp)),
        out_shape=jax.ShapeDtypeStruct((bsz, seq, W_C), BF16),
        scratch_shapes=[pltpu.VMEM((2, 2 * HEAD_LANES, ATT_TILE), BF16),
                        pltpu.VMEM((2, 2, ATT_TILE, ATT_TILE), F32),
                        pltpu.VMEM((2, V_ROWS, ATT_TILE), F32)],
        compiler_params=pltpu.CompilerParams(dimension_semantics=("arbitrary", "arbitrary", "arbitrary"),
                                             vmem_limit_bytes=VMEM_LIMIT),
        name="attn_prompt",
    )(q, k, vt)


def _attn_sample_kernel(q_ref, latc_ref, krc_ref, latn_ref, krn_ref, wabs_ref, fold_ref, wuv_ref, o_ref, *, t_new):
    n_rows = N_HEADS * t_new
    q = q_ref[0]
    qrep = jnp.concatenate([q] * N_HEADS, axis=0)
    r1 = lax.broadcasted_iota(jnp.int32, qrep.shape, 0)
    c1 = lax.broadcasted_iota(jnp.int32, qrep.shape, 1)
    qrep = jnp.where((r1 // t_new) == (c1 // HEAD_LANES), qrep, jnp.zeros((), BF16))
    qlat = _dot(qrep, wabs_ref[...]).astype(BF16)
    qr = _dot(qrep, fold_ref[...])[:, 0:QK_ROPE].astype(BF16)
    latc = latc_ref[0, 0].astype(BF16)
    krc = krc_ref[0, 0].astype(BF16)
    latn = latn_ref[0].astype(BF16)
    krn = krn_ref[0].astype(BF16)
    s_c = _dot_nt(qlat, latc) + _dot_nt(qr, krc)
    s_n = _dot_nt(qlat, latn) + _dot_nt(qr, krn)
    m = jnp.maximum(jnp.max(s_c, axis=-1, keepdims=True), jnp.max(s_n, axis=-1, keepdims=True))
    p_c = jnp.exp2(s_c - m)
    p_n = jnp.exp2(s_n - m)
    den = jnp.sum(p_c, axis=-1, keepdims=True) + jnp.sum(p_n, axis=-1, keepdims=True)
    olat = (_dot(p_c.astype(BF16), latc) + _dot(p_n.astype(BF16), latn)) * (1.0 / den)
    of = _dot(olat.astype(BF16), wuv_ref[...])
    r2 = lax.broadcasted_iota(jnp.int32, of.shape, 0)
    c2 = lax.broadcasted_iota(jnp.int32, of.shape, 1)
    of = jnp.where((r2 // t_new) == (c2 // V_DIM), of, 0.0)
    o = of[0:t_new]
    for hd in range(1, N_HEADS):
        o = o + of[hd * t_new:(hd + 1) * t_new]
    o_ref[0] = o.astype(BF16)
    del n_rows


def _attn_sample(q, lat_cache, kr_cache, layer, lat_new, kr_new, wts):
    bsz, t_new, _ = q.shape
    past = lat_cache.shape[2]
    kern = functools.partial(_attn_sample_kernel, t_new=t_new)

    def full(a):
        return pl.BlockSpec(a.shape, lambda b, _n=a.ndim: (0,) * _n)

    return pl.pallas_call(
        kern,
        grid=(bsz,),
        in_specs=[
            pl.BlockSpec((1, t_new, N_HEADS * HEAD_LANES), lambda b: (b, 0, 0)),
            pl.BlockSpec((1, 1, past, KV_LORA), lambda b: (layer, b, 0, 0)),
            pl.BlockSpec((1, 1, past, QK_ROPE), lambda b: (layer, b, 0, 0)),
            pl.BlockSpec((1, t_new, KV_LORA), lambda b: (b, 0, 0)),
            pl.BlockSpec((1, t_new, QK_ROPE), lambda b: (b, 0, 0)),
            full(wts["wabs"]), full(wts["fold"]), full(wts["w_uv"]),
        ],
        out_specs=pl.BlockSpec((1, t_new, W_C), lambda b: (b, 0, 0)),
        out_shape=jax.ShapeDtypeStruct((bsz, t_new, W_C), BF16),
        compiler_params=pltpu.CompilerParams(dimension_semantics=("arbitrary",), vmem_limit_bytes=VMEM_LIMIT),
        name="attn_sample",
    )(q, lat_cache, kr_cache, lat_new, kr_new, wts["wabs"], wts["fold"], wts["w_uv"])


def _mixer_out_kernel(x_ref, mod_ref, yab_ref, o_ref, gc_ref, w_out_ref, ln_ref, out_ref, *, alpha):
    gate = mod_ref[0][:, 2 * D_MODEL:3 * D_MODEL]
    yc = (o_ref[0].astype(F32) * gc_ref[0].astype(F32)).astype(BF16)
    y = _dot(yab_ref[0], w_out_ref[0:W_A + W_B]) + _dot(yc, w_out_ref[W_A + W_B:W_A + W_B + W_C])
    r = alpha * x_ref[0] + gate * y
    ln = ln_ref[...]
    out_ref[0] = _norm_rows(r, LN_EPS) * ln[0:1] + ln[1:2]


def _mixer_out(x, mod, yab, o, gc, w_out, ln, *, tile, alpha):
    bsz, seq, _ = x.shape
    tok = lambda width: pl.BlockSpec((1, tile, width), lambda b, t: (b, t, 0))
    return pl.pallas_call(
        functools.partial(_mixer_out_kernel, alpha=alpha),
        grid=(bsz, seq // tile),
        in_specs=[tok(D_MODEL), pl.BlockSpec((1, 1, 3 * D_MODEL), lambda b, t: (b, 0, 0)),
                  tok(W_A + W_B), tok(W_C), tok(W_C),
                  pl.BlockSpec(w_out.shape, lambda b, t: (0, 0)),
                  pl.BlockSpec(ln.shape, lambda b, t: (0, 0))],
        out_specs=tok(D_MODEL),
        out_shape=jax.ShapeDtypeStruct(x.shape, F32),
        compiler_params=pltpu.CompilerParams(dimension_semantics=("arbitrary", "arbitrary"),
                                             vmem_limit_bytes=VMEM_LIMIT),
        name="mixer_out",
    )(x, mod, yab, o, gc, w_out, ln)


def _rope_tables(pos0, n):
    inv = ROPE_THETA ** (-jnp.arange(0, QK_ROPE, 2, dtype=F32) / QK_ROPE)
    ang = (pos0 + jnp.arange(n, dtype=jnp.int32)).astype(F32)[:, None] * inv[None, :]
    cos, sin = jnp.cos(ang), jnp.sin(ang)
    ones = jnp.ones((n, QK_NOPE), F32)
    z64 = jnp.zeros((n, QK_NOPE), F32)
    z32 = jnp.zeros((n, HEAD_LANES - QK_NOPE - QK_ROPE), F32)
    return jnp.concatenate([ones, cos, cos, z32], axis=1), jnp.concatenate([z64, sin, sin, z32], axis=1)


def _gmlp_weights(ws, b_s, chunk):
    idx = jnp.arange(chunk)
    mask = (idx[None, :] // CHUNK) <= (idx[:, None] // CHUNK)
    wsm = jnp.where(mask[None], ws[:, :chunk, :chunk], 0.0)
    ws_all = jnp.transpose(wsm, (1, 0, 2)).reshape(chunk, N_HEADS_A * chunk)
    bs_tab = jnp.repeat(b_s[:, :chunk].T, HEAD_A, axis=1)
    return ws_all.astype(BF16), bs_tab.astype(F32)


def _layer_weights(l, p, chunk_p, chunk_s):
    half = QK_ROPE // 2
    w_in = p["w_in"][l]
    krw = w_in[:, 2176:2208]
    x1, x2 = krw[:, :half], krw[:, half:]
    z64 = jnp.zeros((D_MODEL, QK_NOPE), F32)
    z32 = jnp.zeros((D_MODEL, HEAD_LANES - QK_NOPE - QK_ROPE), F32)
    w_in2 = jnp.concatenate([w_in[:, :2176], z64, x1, x2, z32, z64, -x2, x1, z32, w_in[:, 2208:]], axis=1)

    wq = p["mla_w_uq"][l].reshape(Q_LORA, N_HEADS, QK_NOPE + QK_ROPE)
    qn, q1, q2 = wq[..., :QK_NOPE], wq[..., QK_NOPE:QK_NOPE + half], wq[..., QK_NOPE + half:]
    zq32 = jnp.zeros((Q_LORA, N_HEADS, HEAD_LANES - QK_NOPE - QK_ROPE), F32)
    zq64 = jnp.zeros((Q_LORA, N_HEADS, QK_NOPE), F32)
    w_uq = jnp.concatenate([
        jnp.concatenate([qn, q1, q2, zq32], axis=-1).reshape(Q_LORA, -1),
        jnp.concatenate([zq64, -q2, q1, zq32], axis=-1).reshape(Q_LORA, -1)], axis=1)

    wkv = p["mla_w_ukv"][l].reshape(KV_LORA, N_HEADS, QK_NOPE + V_DIM)
    wk, wv = wkv[..., :QK_NOPE], wkv[..., QK_NOPE:]
    w_uk = jnp.concatenate([wk, jnp.zeros_like(wk)], axis=-1).reshape(KV_LORA, -1)
    wvt = jnp.transpose(wv, (1, 2, 0))
    w_uvt = jnp.concatenate([wvt, jnp.zeros((N_HEADS, V_ROWS - V_DIM, KV_LORA), F32)], axis=1).reshape(-1, KV_LORA)
    wkt = jnp.transpose(wk, (1, 2, 0))
    wabs = jnp.concatenate([wkt, jnp.zeros((N_HEADS, HEAD_LANES - QK_NOPE, KV_LORA), F32)], axis=1).reshape(-1, KV_LORA)
    eye = jnp.eye(QK_ROPE, HEAD_LANES, dtype=F32)
    fold_h = jnp.concatenate([jnp.zeros((QK_NOPE, HEAD_LANES), F32), eye,
                              jnp.zeros((HEAD_LANES - QK_NOPE - QK_ROPE, HEAD_LANES), F32)], axis=0)
    fold = jnp.tile(fold_h, (N_HEADS, 1))

    zrow = jnp.zeros((1, W_B), F32)
    vec = jnp.stack([p["gmlp_ln_g"][l], p["gmlp_ln_b"][l], p["conv_dw_b"][l], p["conv_ln_g"][l],
                     p["conv_ln_b"][l], p["conv_b_pw"][l], p["mla_kv_norm"][l], zrow[0]], axis=0)
    common = dict(
        w_in=w_in2.astype(BF16), vec=vec, q_norm=p["mla_q_norm"][l].reshape(1, Q_LORA),
        dwk=jnp.concatenate([p["conv_dw_k"][l], zrow], axis=0), w_pw=p["conv_w_pw"][l].astype(BF16),
        w_uq=w_uq.astype(BF16), w_uk=w_uk.astype(BF16), w_uvt=w_uvt.astype(BF16),
        wabs=wabs.astype(BF16), fold=fold.astype(BF16), w_uv=wv.reshape(KV_LORA, W_C).astype(BF16),
        w_out=p["w_out"][l].astype(BF16), ln=jnp.stack([p["post_ln_g"][l], p["post_ln_b"][l]], axis=0))
    ws_p, bs_p = _gmlp_weights(p["gmlp_ws"][l], p["gmlp_bs"][l], chunk_p)
    ws_s, bs_s = _gmlp_weights(p["gmlp_ws"][l], p["gmlp_bs"][l], chunk_s)
    return dict(common, ws=ws_p, bs=bs_p), dict(common, ws=ws_s, bs=bs_s)


def kernel(x_prompt, x_sample, cache_latent, cache_krope, state_conv, c_prompt, c_sample,
           w_ada, b_ada, w_in, gmlp_ln_g, gmlp_ln_b, gmlp_ws, gmlp_bs,
           conv_dw_k, conv_dw_b, conv_ln_g, conv_ln_b, conv_w_pw, conv_b_pw,
           mla_q_norm, mla_w_uq, mla_kv_norm, mla_w_ukv, w_out, post_ln_g, post_ln_b):
    p = dict(w_in=w_in, gmlp_ln_g=gmlp_ln_g, gmlp_ln_b=gmlp_ln_b, gmlp_ws=gmlp_ws, gmlp_bs=gmlp_bs,
             conv_dw_k=conv_dw_k, conv_dw_b=conv_dw_b, conv_ln_g=conv_ln_g, conv_ln_b=conv_ln_b,
             conv_w_pw=conv_w_pw, conv_b_pw=conv_b_pw, mla_q_norm=mla_q_norm, mla_w_uq=mla_w_uq,
             mla_kv_norm=mla_kv_norm, mla_w_ukv=mla_w_ukv, w_out=w_out, post_ln_g=post_ln_g, post_ln_b=post_ln_b)
    depth = w_ada.shape[0]
    bp, seq, _ = x_prompt.shape
    bs, t_new, _ = x_sample.shape
    past_len = cache_latent.shape[2]
    alpha = (2 * depth) ** 0.25
    tile_p = min(IN_TILE, seq)
    chunk_p = min(seq, MLP_CHUNK)
    chunk_s = min(t_new, MLP_CHUNK)

    n_c = bp + bs
    c_all = jnp.concatenate([c_prompt, c_sample, jnp.zeros((-n_c % 8, D_MODEL), F32)], axis=0)
    mod_all = _ada(c_all, w_ada, b_ada)

    ct_p, st_p = _rope_tables(0, seq)
    ct_s, st_s = _rope_tables(past_len, t_new)
    zero_past = jnp.zeros((bp, HIST, W_B), F32)

    xp, xs = x_prompt, x_sample
    p_conv, p_lat, p_kr, s_conv, s_lat, s_kr, s_v = [], [], [], [], [], [], []
    for l in range(depth):
        wp, wsm = _layer_weights(l, p, chunk_p, chunk_s)
        mod_p = mod_all[l, 0:bp].reshape(bp, 1, -1)
        mod_s = mod_all[l, bp:n_c].reshape(bs, 1, -1)

        q, lat, kr, yab, gc, conv, k, vt = _mixer_in(xp, mod_p, ct_p, st_p, zero_past, wp,
                                                      tile=tile_p, chunk=chunk_p, prompt=True)
        o = _attn_prompt(q, k, vt)
        xp = _mixer_out(xp, mod_p, yab, o, gc, wp["w_out"], wp["ln"], tile=tile_p, alpha=alpha)
        p_conv.append(conv[:, HIST - (CONV_W - 1):])
        p_lat.append(lat)
        p_kr.append(kr)

        past = jnp.concatenate([jnp.zeros((bs, HIST - (CONV_W - 1), W_B), F32), state_conv[l]], axis=1)
        q, lat, kr, yab, gc, conv, vst = _mixer_in(xs, mod_s, ct_s, st_s, past, wsm,
                                                   tile=t_new, chunk=chunk_s, prompt=False)
        o = _attn_sample(q, cache_latent, cache_krope, l, lat, kr, wsm)
        xs = _mixer_out(xs, mod_s, yab, o, gc, wsm["w_out"], wsm["ln"], tile=t_new, alpha=alpha)
        s_conv.append(conv[:, HIST - (CONV_W - 1):])
        s_lat.append(lat)
        s_kr.append(kr)
        s_v.append(vst)

    return (xp, xs, jnp.stack(p_conv), jnp.stack(p_lat), jnp.stack(p_kr),
            jnp.stack(s_conv), jnp.stack(s_lat), jnp.stack(s_kr), jnp.stack(s_v))
```

```python
import functools
import math

import jax
import jax.numpy as jnp
from jax import lax
from jax.experimental import pallas as pl
from jax.experimental.pallas import tpu as pltpu

F32 = jnp.float32
BF16 = jnp.bfloat16

D_MODEL = 1024
N_HEADS_A = 4
HEAD_A = 64
W_A = 256
W_B = 256
CONV_W = 31
N_HEADS = 8
QK_NOPE = 64
QK_ROPE = 32
V_DIM = 64
W_C = N_HEADS * V_DIM
Q_LORA = 384
KV_LORA = 256
CHUNK = 64
MLP_CHUNK = 128
ROPE_THETA = 10000.0
ATTN_SCALE = (QK_NOPE + QK_ROPE) ** -0.5
LN_EPS = 1e-5
RMS_EPS = 1e-6

HEAD_LANES = 128
V_ROWS = 80
HIST = 32
ATT_TILE = 256
Q_BLOCK_ROWS = 1024
IN_TILE = 512
VMEM_LIMIT = 56 * 1024 * 1024

_C_A, _C_B, _C_Q, _C_KV, _C_KR, _C_GC, _C_END = 0, 768, 1536, 1920, 2176, 2432, 2944
_Q_SCALE = ATTN_SCALE * math.log2(math.e)


def _dot(a, b):
    return jnp.dot(a, b, preferred_element_type=F32)


def _dot_nt(a, b):
    return lax.dot_general(a, b, (((1,), (1,)), ((), ())), preferred_element_type=F32)


def _norm_rows(x, eps):
    mu = jnp.mean(x, axis=-1, keepdims=True)
    xc = x - mu
    var = jnp.mean(xc * xc, axis=-1, keepdims=True)
    return xc * lax.rsqrt(var + eps)


def _gelu_tanh(x):
    return x * (0.5 * (1.0 + jnp.tanh(math.sqrt(2.0 / math.pi) * (x + 0.044715 * (x * x * x)))))


def _silu(x):
    return x * jax.nn.sigmoid(x)


def _ada_kernel(c_ref, w_ref, b_ref, o_ref):
    c = c_ref[...]
    a = _silu(c)
    w = w_ref[0]
    a_hi = a.astype(BF16)
    a_lo = (a - a_hi.astype(F32)).astype(BF16)
    w_hi = w.astype(BF16)
    w_lo = (w - w_hi.astype(F32)).astype(BF16)
    o_ref[0] = _dot(a_hi, w_hi) + (_dot(a_hi, w_lo) + _dot(a_lo, w_hi)) + b_ref[0]


def _ada(c_all, w_ada, b_ada):
    depth = w_ada.shape[0]
    rows = c_all.shape[0]
    n_col = w_ada.shape[2] // D_MODEL
    return pl.pallas_call(
        _ada_kernel,
        grid=(depth, n_col),
        in_specs=[
            pl.BlockSpec((rows, D_MODEL), lambda l, j: (0, 0)),
            pl.BlockSpec((1, D_MODEL, D_MODEL), lambda l, j: (l, 0, j)),
            pl.BlockSpec((1, 1, D_MODEL), lambda l, j: (l, 0, j)),
        ],
        out_specs=pl.BlockSpec((1, rows, D_MODEL), lambda l, j: (l, 0, j)),
        out_shape=jax.ShapeDtypeStruct((depth, rows, w_ada.shape[2]), F32),
        compiler_params=pltpu.CompilerParams(vmem_limit_bytes=VMEM_LIMIT),
        name="ada_mod",
    )(c_all, w_ada, b_ada.reshape(depth, 1, -1))


def _mixer_in_kernel(x_ref, mod_ref, ct_ref, st_ref, past_ref, w_in_ref, vec_ref, qn_ref, ws_ref, bs_ref,
                     dwk_ref, w_pw_ref, w_uq_ref, *rest, tile, chunk, prompt):
    if prompt:
        (w_uk_ref, w_uvt_ref, q_ref, lat_ref, kr_ref, yab_ref, gc_ref, conv_ref, k_ref, vt_ref, hist) = rest
    else:
        (q_ref, lat_ref, kr_ref, yab_ref, gc_ref, conv_ref, vst_ref, hist) = rest
    t = pl.program_id(1)

    mod = mod_ref[0]
    shift = mod[:, 0:D_MODEL]
    scale = mod[:, D_MODEL:2 * D_MODEL]
    h = _norm_rows(x_ref[0], LN_EPS) * (1.0 + scale) + shift
    hb = h.astype(BF16)

    vec = vec_ref[...]
    gmlp_g, gmlp_b = vec[0:1], vec[1:2]
    dw_b, conv_g, conv_b, b_pw, kv_norm = vec[2:3], vec[3:4], vec[4:5], vec[5:6], vec[6:7]

    za = _dot(hb, w_in_ref[:, _C_A:_C_B])
    u = _gelu_tanh(za[:, 0:W_A])
    v = _norm_rows(_gelu_tanh(za[:, W_A:2 * W_A]), LN_EPS) * gmlp_g + gmlp_b
    if not prompt:
        vst_ref[0] = v
    vb = v.astype(BF16)
    rows = lax.broadcasted_iota(jnp.int32, (N_HEADS_A * chunk, W_A), 0)
    cols = lax.broadcasted_iota(jnp.int32, (N_HEADS_A * chunk, W_A), 1)
    own_head = (rows // chunk) == (cols // HEAD_A)
    ws = ws_ref[...]
    bs = bs_ref[...]
    mixed = []
    for c in range(tile // chunk):
        vc = vb[c * chunk:(c + 1) * chunk]
        vbd = jnp.where(own_head, jnp.concatenate([vc] * N_HEADS_A, axis=0), jnp.zeros((), BF16))
        mixed.append(_dot(ws, vbd) + bs)
    s = mixed[0] if len(mixed) == 1 else jnp.concatenate(mixed, axis=0)
    y_a = u * s * _silu(za[:, 2 * W_A:3 * W_A])
    yab_ref[0, :, 0:W_A] = y_a.astype(BF16)

    zb = _dot(hb, w_in_ref[:, _C_B:_C_Q])
    g = zb[:, 0:W_B] * jax.nn.sigmoid(zb[:, W_B:2 * W_B])

    @pl.when(t == 0)
    def _():
        hist[0:HIST] = past_ref[0]

    hist[HIST:HIST + tile] = g
    dwk = dwk_ref[...]
    y = jnp.broadcast_to(dw_b, (tile, W_B))
    first = HIST - (CONV_W - 1)
    for r in range(8):
        rows = tile if r == 0 else tile + 8
        part = None
        for a in range(HIST // 8 + 1):
            k = 8 * a + r - first
            if 0 <= k < CONV_W:
                term = dwk[k:k + 1] * hist[8 * a:8 * a + rows]
                part = term if part is None else part + term
        y = y + (part if r == 0 else part[r:r + tile])
    new_hist = hist[tile:tile + HIST]
    hist[0:HIST] = new_hist
    conv_ref[0] = new_hist
    y = _silu(_norm_rows(y, LN_EPS) * conv_g + conv_b)
    y = _dot(y.astype(BF16), w_pw_ref[...]) + b_pw
    y_b = y * _silu(zb[:, 2 * W_B:3 * W_B])
    yab_ref[0, :, W_A:W_A + W_B] = y_b.astype(BF16)

    ct = ct_ref[...]
    st = st_ref[...]
    zq = _dot(hb, w_in_ref[:, _C_Q:_C_KV])
    cq = zq * lax.rsqrt(jnp.mean(zq * zq, axis=-1, keepdims=True) + RMS_EPS) * qn_ref[...]
    q2 = _dot(cq.astype(BF16), w_uq_ref[...])
    ctq = ct * _Q_SCALE
    stq = st * _Q_SCALE
    for hd in range(N_HEADS):
        lo = hd * HEAD_LANES
        qh = q2[:, lo:lo + HEAD_LANES] * ctq + q2[:, N_HEADS * HEAD_LANES + lo:N_HEADS * HEAD_LANES + lo + HEAD_LANES] * stq
        q_ref[0, :, lo:lo + HEAD_LANES] = qh.astype(BF16)

    zkv = _dot(hb, w_in_ref[:, _C_KV:_C_KR])
    lat = zkv * lax.rsqrt(jnp.mean(zkv * zkv, axis=-1, keepdims=True) + RMS_EPS) * kv_norm
    lat_ref[0] = lat
    zkr = _dot(hb, w_in_ref[:, _C_KR:_C_GC])
    krot = zkr[:, 0:HEAD_LANES] * ct + zkr[:, HEAD_LANES:2 * HEAD_LANES] * st
    kr_ref[0] = krot[:, QK_NOPE:QK_NOPE + QK_ROPE]

    if prompt:
        latb = lat.astype(BF16)
        kn = _dot(latb, w_uk_ref[...])
        for hd in range(N_HEADS):
            lo = hd * HEAD_LANES
            k_ref[0, :, lo:lo + HEAD_LANES] = (kn[:, lo:lo + HEAD_LANES] + krot).astype(BF16)
        vt = _dot_nt(w_uvt_ref[...], latb)
        vrow = lax.broadcasted_iota(jnp.int32, vt.shape, 0)
        vt = jnp.where((vrow % V_ROWS) == V_DIM, 1.0, vt)
        for c in range(tile // ATT_TILE):
            blk = vt[:, c * ATT_TILE:(c + 1) * ATT_TILE].reshape(N_HEADS, V_ROWS, ATT_TILE)
            vt_ref[0, :, c] = blk.astype(BF16)

    zg = _dot(hb, w_in_ref[:, _C_GC:_C_END])
    gc_ref[0] = _silu(zg).astype(BF16)


def _mixer_in(x, mod, ctab, stab, past, wts, *, tile, chunk, prompt):
    bsz, seq, _ = x.shape
    n_t = seq // tile
    kern = functools.partial(_mixer_in_kernel, tile=tile, chunk=chunk, prompt=prompt)

    def full(a):
        return pl.BlockSpec(a.shape, lambda b, t, _n=a.ndim: (0,) * _n)

    weights = [wts["w_in"], wts["vec"], wts["q_norm"], wts["ws"], wts["bs"], wts["dwk"], wts["w_pw"], wts["w_uq"]]
    if prompt:
        weights += [wts["w_uk"], wts["w_uvt"]]
    in_specs = [
        pl.BlockSpec((1, tile, D_MODEL), lambda b, t: (b, t, 0)),
        pl.BlockSpec((1, 1, 3 * D_MODEL), lambda b, t: (b, 0, 0)),
        pl.BlockSpec((tile, HEAD_LANES), lambda b, t: (t, 0)),
        pl.BlockSpec((tile, HEAD_LANES), lambda b, t: (t, 0)),
        pl.BlockSpec((1, HIST, W_B), lambda b, t: (b, 0, 0)),
    ] + [full(w) for w in weights]
    tok = lambda width: pl.BlockSpec((1, tile, width), lambda b, t: (b, t, 0))
    out_specs = [tok(N_HEADS * HEAD_LANES), tok(KV_LORA), tok(QK_ROPE), tok(W_A + W_B), tok(W_C),
                 pl.BlockSpec((1, HIST, W_B), lambda b, t: (b, 0, 0))]
    out_shape = [
        jax.ShapeDtypeStruct((bsz, seq, N_HEADS * HEAD_LANES), BF16),
        jax.ShapeDtypeStruct((bsz, seq, KV_LORA), F32),
        jax.ShapeDtypeStruct((bsz, seq, QK_ROPE), F32),
        jax.ShapeDtypeStruct((bsz, seq, W_A + W_B), BF16),
        jax.ShapeDtypeStruct((bsz, seq, W_C), BF16),
        jax.ShapeDtypeStruct((bsz, HIST, W_B), F32),
    ]
    if prompt:
        n_kt = tile // ATT_TILE
        out_specs += [tok(N_HEADS * HEAD_LANES),
                      pl.BlockSpec((1, N_HEADS, n_kt, V_ROWS, ATT_TILE), lambda b, t: (b, 0, t, 0, 0))]
        out_shape += [jax.ShapeDtypeStruct((bsz, seq, N_HEADS * HEAD_LANES), BF16),
                      jax.ShapeDtypeStruct((bsz, N_HEADS, seq // ATT_TILE, V_ROWS, ATT_TILE), BF16)]
    else:
        out_specs += [tok(W_A)]
        out_shape += [jax.ShapeDtypeStruct((bsz, seq, W_A), F32)]
    return pl.pallas_call(
        kern,
        grid=(bsz, n_t),
        in_specs=in_specs,
        out_specs=out_specs,
        out_shape=out_shape,
        scratch_shapes=[pltpu.VMEM((HIST + tile, W_B), F32)],
        compiler_params=pltpu.CompilerParams(dimension_semantics=("arbitrary", "arbitrary"),
                                             vmem_limit_bytes=VMEM_LIMIT),
        name="mixer_in_prompt" if prompt else "mixer_in_sample",
    )(x, mod, ctab, stab, past, *weights)


_S_ADDR = (0, 64)
_O_ADDR = ((128, 148), (168, 188))
_FIXED_TILES = 5


def _attn_prompt_kernel(q_ref, k_ref, vt_ref, bias_ref, o_ref, qt_sc, s_sc, acc_sc, *, n_sub, n_kt):
    qi = pl.program_id(2)

    def tile_index(t):
        return jnp.minimum(t, n_kt - 1)

    class SubTile:
        def __init__(self, sub):
            self.sp = sub % 2
            self.row0 = sub * ATT_TILE
            self.g = qi * n_sub + sub
            self.n_steady = jnp.maximum((self.g - 3) // 2, 0)
            self.j0 = _FIXED_TILES + 2 * self.n_steady - 3
            q_t = q_ref[0, self.row0:self.row0 + ATT_TILE, :].astype(F32).T
            zero = jnp.zeros((HEAD_LANES, ATT_TILE), BF16)
            for hd in range(2):
                own = slice(hd * HEAD_LANES, (hd + 1) * HEAD_LANES)
                other = slice((1 - hd) * HEAD_LANES, (2 - hd) * HEAD_LANES)
                qt_sc[self.sp, hd, own] = q_t[own].astype(BF16)
                qt_sc[self.sp, hd, other] = zero
            acc_sc[self.sp] = jnp.zeros(acc_sc.shape[1:], F32)
            neg = jnp.full((1, ATT_TILE), -jnp.inf, F32)
            one = jnp.ones((1, ATT_TILE), F32)
            self.state = ([neg, neg], [one, one], [one, one], [one, one])

        def stage_qt(self):
            for hd in range(2):
                pltpu.matmul_push_rhs(qt_sc[self.sp, hd], staging_register=0, mxu_index=hd)

        def stage_a(self, t, par, restage):
            k_tile = k_ref[0, pl.ds(pl.multiple_of(tile_index(t) * ATT_TILE, ATT_TILE), ATT_TILE), :]
            for hd in range(2):
                pltpu.matmul_acc_lhs(_S_ADDR[par], k_tile, mxu_index=hd, load_staged_rhs=0)
                if restage:
                    pltpu.matmul_push_rhs(qt_sc[self.sp, hd], staging_register=0, mxu_index=hd)

        def stage_b(self, t, par, masked):
            m_old = self.state[0]
            m_new, alpha = [], []
            if masked:
                bias = bias_ref[jnp.where(t < self.g, 0, jnp.where(t == self.g, 1, 2))]
            for hd in range(2):
                s = pltpu.matmul_pop(_S_ADDR[par], (ATT_TILE, ATT_TILE), F32, mxu_index=hd)
                if masked:
                    s = s + bias
                m_hd = jnp.maximum(m_old[hd], jnp.max(s, axis=0, keepdims=True))
                alpha.append(jnp.exp2(m_old[hd] - m_hd))
                m_new.append(m_hd)
                s_sc[self.sp, hd, par] = s
            return m_new, alpha

        def stage_c(self, t, par):
            m_t = self.state[0]
            for hd in range(2):
                p = jnp.exp2(s_sc[self.sp, hd, par] - m_t[hd]).astype(BF16)
                pltpu.matmul_push_rhs(p, staging_register=1, mxu_index=hd)
                pltpu.matmul_acc_lhs(_O_ADDR[self.sp][par], vt_ref[0, hd, tile_index(t)], mxu_index=hd,
                                     load_staged_rhs=1)

        def stage_d(self, par):
            alpha_t = self.state[3]
            for hd in range(2):
                o = pltpu.matmul_pop(_O_ADDR[self.sp][par], (V_ROWS, ATT_TILE), F32, mxu_index=hd)
                acc_sc[self.sp, hd] = alpha_t[hd] * acc_sc[self.sp, hd] + o

        def finish(self):
            outs = []
            for hd in range(2):
                acc = acc_sc[self.sp, hd]
                outs.append(acc[0:V_DIM] * (1.0 / acc[V_DIM:V_DIM + 1]))
            o_t = jnp.concatenate(outs, axis=0)
            o_ref[0, self.row0:self.row0 + ATT_TILE, :] = o_t.T.astype(BF16)

    def run(steps):
        for st, j, par, f in steps:
            if f.get("d", True):
                st.stage_d(par)
        new = []
        for st, j, par, f in steps:
            m, al1, al2, al3 = st.state
            if f.get("b", True):
                m_next, al0 = st.stage_b(j + 1, 1 - par, f.get("masked", False))
            else:
                m_next, al0 = m, al1
            new.append((m_next, al0, al1, al2))
        for st, j, par, f in steps:
            if f.get("a", True):
                st.stage_a(j + 2, par, f.get("restage", True))
            if f.get("qt_of") is not None:
                f["qt_of"].stage_qt()
        for st, j, par, f in steps:
            if f.get("c", True):
                st.stage_c(j, par)
        for (st, j, par, f), state in zip(steps, new):
            st.state = state

    fill = [dict(b=False, c=False, d=False), dict(c=False, d=False, masked=True),
            dict(d=False, masked=True), dict(d=False, masked=True)]

    def drain(nxt):
        return [dict(masked=True, restage=False, qt_of=nxt), dict(a=False, masked=True),
                dict(a=False, b=False), dict(a=False, b=False, c=False), dict(a=False, b=False, c=False)]

    cur = SubTile(0)
    cur.stage_qt()
    for i in range(4):
        run([(cur, i - 2, i % 2, fill[i])])
    for sub in range(n_sub):
        def steady(u, state, st=cur):
            st.state = state
            j = 2 + 2 * u
            run([(st, j, 0, {})])
            run([(st, j + 1, 1, {})])
            return st.state

        cur.state = lax.fori_loop(0, cur.n_steady, steady, cur.state)
        nxt = SubTile(sub + 1) if sub + 1 < n_sub else None
        dr = drain(nxt)
        run([(cur, cur.j0, 0, dr[0])])
        for i in range(4):
            steps = [(cur, cur.j0 + 1 + i, (i + 1) % 2, dr[i + 1])]
            if nxt is not None:
                steps.append((nxt, i - 2, i % 2, fill[i]))
            run(steps)
        cur.finish()
        cur = nxt


def _attn_prompt(q, k, vt):
    bsz, seq, _ = q.shape
    qb = min(Q_BLOCK_ROWS, seq)
    n_sub = qb // ATT_TILE
    n_kt = seq // ATT_TILE
    kern = functools.partial(_attn_prompt_kernel, n_sub=n_sub, n_kt=n_kt)
    idx = jnp.arange(ATT_TILE) // CHUNK
    diag = jnp.where(idx[:, None] <= idx[None, :], 0.0, -jnp.inf).astype(F32)
    bias = jnp.stack([jnp.zeros_like(diag), diag, jnp.full_like(diag, -jnp.inf)])
    return pl.pallas_call(
        kern,
        grid=(bsz, N_HEADS // 2, seq // qb),
        in_specs=[
            pl.BlockSpec((1, qb, 2 * HEAD_LANES), lambda b, hp, i: (b, i, hp)),
            pl.BlockSpec((1, seq, 2 * HEAD_LANES), lambda b, hp, i: (b, 0, hp)),
            pl.BlockSpec((1, 2, n_kt, V_ROWS, ATT_TILE), lambda b, hp, i: (b, hp, 0, 0, 0)),
            pl.BlockSpec((3, ATT_TILE, ATT_TILE), lambda b, hp, i: (0, 0, 0)),
        ],
        out_specs=pl.BlockSpec((1, qb, 2 * V_DIM), lambda b, hp, i: (b, i, hp)),
        out_shape=jax.ShapeDtypeStruct((bsz, seq, W_C), BF16),
        scratch_shapes=[pltpu.VMEM((2, 2, 2 * HEAD_LANES, ATT_TILE), BF16),
                        pltpu.VMEM((2, 2, 2, ATT_TILE, ATT_TILE), F32),
                        pltpu.VMEM((2, 2, V_ROWS, ATT_TILE), F32)],
        compiler_params=pltpu.CompilerParams(dimension_semantics=("arbitrary", "arbitrary", "arbitrary"),
                                             vmem_limit_bytes=VMEM_LIMIT),
        name="attn_prompt",
    )(q, k, vt, bias)


def _attn_sample_kernel(q_ref, latc_ref, krc_ref, latn_ref, krn_ref, wabs_ref, fold_ref, wuv_ref, o_ref, *, t_new):
    n_rows = N_HEADS * t_new
    q = q_ref[0]
    qrep = jnp.concatenate([q] * N_HEADS, axis=0)
    r1 = lax.broadcasted_iota(jnp.int32, qrep.shape, 0)
    c1 = lax.broadcasted_iota(jnp.int32, qrep.shape, 1)
    qrep = jnp.where((r1 // t_new) == (c1 // HEAD_LANES), qrep, jnp.zeros((), BF16))
    qlat = _dot(qrep, wabs_ref[...]).astype(BF16)
    qr = _dot(qrep, fold_ref[...])[:, 0:QK_ROPE].astype(BF16)
    latc = latc_ref[0, 0].astype(BF16)
    krc = krc_ref[0, 0].astype(BF16)
    latn = latn_ref[0].astype(BF16)
    krn = krn_ref[0].astype(BF16)
    s_c = _dot_nt(qlat, latc) + _dot_nt(qr, krc)
    s_n = _dot_nt(qlat, latn) + _dot_nt(qr, krn)
    m = jnp.maximum(jnp.max(s_c, axis=-1, keepdims=True), jnp.max(s_n, axis=-1, keepdims=True))
    p_c = jnp.exp2(s_c - m)
    p_n = jnp.exp2(s_n - m)
    den = jnp.sum(p_c, axis=-1, keepdims=True) + jnp.sum(p_n, axis=-1, keepdims=True)
    olat = (_dot(p_c.astype(BF16), latc) + _dot(p_n.astype(BF16), latn)) * (1.0 / den)
    of = _dot(olat.astype(BF16), wuv_ref[...])
    r2 = lax.broadcasted_iota(jnp.int32, of.shape, 0)
    c2 = lax.broadcasted_iota(jnp.int32, of.shape, 1)
    of = jnp.where((r2 // t_new) == (c2 // V_DIM), of, 0.0)
    o = of[0:t_new]
    for hd in range(1, N_HEADS):
        o = o + of[hd * t_new:(hd + 1) * t_new]
    o_ref[0] = o.astype(BF16)
    del n_rows


def _attn_sample(q, lat_cache, kr_cache, layer, lat_new, kr_new, wts):
    bsz, t_new, _ = q.shape
    past = lat_cache.shape[2]
    kern = functools.partial(_attn_sample_kernel, t_new=t_new)

    def full(a):
        return pl.BlockSpec(a.shape, lambda b, _n=a.ndim: (0,) * _n)

    return pl.pallas_call(
        kern,
        grid=(bsz,),
        in_specs=[
            pl.BlockSpec((1, t_new, N_HEADS * HEAD_LANES), lambda b: (b, 0, 0)),
            pl.BlockSpec((1, 1, past, KV_LORA), lambda b: (layer, b, 0, 0)),
            pl.BlockSpec((1, 1, past, QK_ROPE), lambda b: (layer, b, 0, 0)),
            pl.BlockSpec((1, t_new, KV_LORA), lambda b: (b, 0, 0)),
            pl.BlockSpec((1, t_new, QK_ROPE), lambda b: (b, 0, 0)),
            full(wts["wabs"]), full(wts["fold"]), full(wts["w_uv"]),
        ],
        out_specs=pl.BlockSpec((1, t_new, W_C), lambda b: (b, 0, 0)),
        out_shape=jax.ShapeDtypeStruct((bsz, t_new, W_C), BF16),
        compiler_params=pltpu.CompilerParams(dimension_semantics=("arbitrary",), vmem_limit_bytes=VMEM_LIMIT),
        name="attn_sample",
    )(q, lat_cache, kr_cache, lat_new, kr_new, wts["wabs"], wts["fold"], wts["w_uv"])


def _mixer_out_kernel(x_ref, mod_ref, yab_ref, o_ref, gc_ref, w_out_ref, ln_ref, out_ref, *, alpha):
    gate = mod_ref[0][:, 2 * D_MODEL:3 * D_MODEL]
    yc = (o_ref[0].astype(F32) * gc_ref[0].astype(F32)).astype(BF16)
    y = _dot(yab_ref[0], w_out_ref[0:W_A + W_B]) + _dot(yc, w_out_ref[W_A + W_B:W_A + W_B + W_C])
    r = alpha * x_ref[0] + gate * y
    ln = ln_ref[...]
    out_ref[0] = _norm_rows(r, LN_EPS) * ln[0:1] + ln[1:2]


def _mixer_out(x, mod, yab, o, gc, w_out, ln, *, tile, alpha):
    bsz, seq, _ = x.shape
    tok = lambda width: pl.BlockSpec((1, tile, width), lambda b, t: (b, t, 0))
    return pl.pallas_call(
        functools.partial(_mixer_out_kernel, alpha=alpha),
        grid=(bsz, seq // tile),
        in_specs=[tok(D_MODEL), pl.BlockSpec((1, 1, 3 * D_MODEL), lambda b, t: (b, 0, 0)),
                  tok(W_A + W_B), tok(W_C), tok(W_C),
                  pl.BlockSpec(w_out.shape, lambda b, t: (0, 0)),
                  pl.BlockSpec(ln.shape, lambda b, t: (0, 0))],
        out_specs=tok(D_MODEL),
        out_shape=jax.ShapeDtypeStruct(x.shape, F32),
        compiler_params=pltpu.CompilerParams(dimension_semantics=("arbitrary", "arbitrary"),
                                             vmem_limit_bytes=VMEM_LIMIT),
        name="mixer_out",
    )(x, mod, yab, o, gc, w_out, ln)


def _rope_tables(pos0, n):
    inv = ROPE_THETA ** (-jnp.arange(0, QK_ROPE, 2, dtype=F32) / QK_ROPE)
    ang = (pos0 + jnp.arange(n, dtype=jnp.int32)).astype(F32)[:, None] * inv[None, :]
    cos, sin = lax.optimization_barrier((jnp.cos(ang), jnp.sin(ang)))
    ones = jnp.ones((n, QK_NOPE), F32)
    z64 = jnp.zeros((n, QK_NOPE), F32)
    z32 = jnp.zeros((n, HEAD_LANES - QK_NOPE - QK_ROPE), F32)
    return jnp.concatenate([ones, cos, cos, z32], axis=1), jnp.concatenate([z64, sin, sin, z32], axis=1)


def _gmlp_weights(ws, b_s, chunk):
    idx = jnp.arange(chunk)
    mask = (idx[None, :] // CHUNK) <= (idx[:, None] // CHUNK)
    wsm = jnp.where(mask[None], ws[:, :chunk, :chunk], 0.0)
    ws_all = jnp.transpose(wsm, (1, 0, 2)).reshape(chunk, N_HEADS_A * chunk)
    bs_tab = jnp.repeat(b_s[:, :chunk].T, HEAD_A, axis=1)
    return ws_all.astype(BF16), bs_tab.astype(F32)


def _layer_weights(l, p, chunk_p, chunk_s):
    half = QK_ROPE // 2
    w_in = p["w_in"][l]
    krw = w_in[:, 2176:2208]
    x1, x2 = krw[:, :half], krw[:, half:]
    z64 = jnp.zeros((D_MODEL, QK_NOPE), F32)
    z32 = jnp.zeros((D_MODEL, HEAD_LANES - QK_NOPE - QK_ROPE), F32)
    w_in2 = jnp.concatenate([w_in[:, :2176], z64, x1, x2, z32, z64, -x2, x1, z32, w_in[:, 2208:]], axis=1)

    wq = p["mla_w_uq"][l].reshape(Q_LORA, N_HEADS, QK_NOPE + QK_ROPE)
    qn, q1, q2 = wq[..., :QK_NOPE], wq[..., QK_NOPE:QK_NOPE + half], wq[..., QK_NOPE + half:]
    zq32 = jnp.zeros((Q_LORA, N_HEADS, HEAD_LANES - QK_NOPE - QK_ROPE), F32)
    zq64 = jnp.zeros((Q_LORA, N_HEADS, QK_NOPE), F32)
    w_uq = jnp.concatenate([
        jnp.concatenate([qn, q1, q2, zq32], axis=-1).reshape(Q_LORA, -1),
        jnp.concatenate([zq64, -q2, q1, zq32], axis=-1).reshape(Q_LORA, -1)], axis=1)

    wkv = p["mla_w_ukv"][l].reshape(KV_LORA, N_HEADS, QK_NOPE + V_DIM)
    wk, wv = wkv[..., :QK_NOPE], wkv[..., QK_NOPE:]
    w_uk = jnp.concatenate([wk, jnp.zeros_like(wk)], axis=-1).reshape(KV_LORA, -1)
    wvt = jnp.transpose(wv, (1, 2, 0))
    w_uvt = jnp.concatenate([wvt, jnp.zeros((N_HEADS, V_ROWS - V_DIM, KV_LORA), F32)], axis=1).reshape(-1, KV_LORA)
    wkt = jnp.transpose(wk, (1, 2, 0))
    wabs = jnp.concatenate([wkt, jnp.zeros((N_HEADS, HEAD_LANES - QK_NOPE, KV_LORA), F32)], axis=1).reshape(-1, KV_LORA)
    eye = jnp.eye(QK_ROPE, HEAD_LANES, dtype=F32)
    fold_h = jnp.concatenate([jnp.zeros((QK_NOPE, HEAD_LANES), F32), eye,
                              jnp.zeros((HEAD_LANES - QK_NOPE - QK_ROPE, HEAD_LANES), F32)], axis=0)
    fold = jnp.tile(fold_h, (N_HEADS, 1))

    zrow = jnp.zeros((1, W_B), F32)
    vec = jnp.stack([p["gmlp_ln_g"][l], p["gmlp_ln_b"][l], p["conv_dw_b"][l], p["conv_ln_g"][l],
                     p["conv_ln_b"][l], p["conv_b_pw"][l], p["mla_kv_norm"][l], zrow[0]], axis=0)
    common = dict(
        w_in=w_in2.astype(BF16), vec=vec, q_norm=p["mla_q_norm"][l].reshape(1, Q_LORA),
        dwk=jnp.concatenate([p["conv_dw_k"][l], zrow], axis=0), w_pw=p["conv_w_pw"][l].astype(BF16),
        w_uq=w_uq.astype(BF16), w_uk=w_uk.astype(BF16), w_uvt=w_uvt.astype(BF16),
        wabs=wabs.astype(BF16), fold=fold.astype(BF16), w_uv=wv.reshape(KV_LORA, W_C).astype(BF16),
        w_out=p["w_out"][l].astype(BF16), ln=jnp.stack([p["post_ln_g"][l], p["post_ln_b"][l]], axis=0))
    ws_p, bs_p = _gmlp_weights(p["gmlp_ws"][l], p["gmlp_bs"][l], chunk_p)
    ws_s, bs_s = _gmlp_weights(p["gmlp_ws"][l], p["gmlp_bs"][l], chunk_s)
    return dict(common, ws=ws_p, bs=bs_p), dict(common, ws=ws_s, bs=bs_s)


def kernel(x_prompt, x_sample, cache_latent, cache_krope, state_conv, c_prompt, c_sample,
           w_ada, b_ada, w_in, gmlp_ln_g, gmlp_ln_b, gmlp_ws, gmlp_bs,
           conv_dw_k, conv_dw_b, conv_ln_g, conv_ln_b, conv_w_pw, conv_b_pw,
           mla_q_norm, mla_w_uq, mla_kv_norm, mla_w_ukv, w_out, post_ln_g, post_ln_b):
    p = dict(w_in=w_in, gmlp_ln_g=gmlp_ln_g, gmlp_ln_b=gmlp_ln_b, gmlp_ws=gmlp_ws, gmlp_bs=gmlp_bs,
             conv_dw_k=conv_dw_k, conv_dw_b=conv_dw_b, conv_ln_g=conv_ln_g, conv_ln_b=conv_ln_b,
             conv_w_pw=conv_w_pw, conv_b_pw=conv_b_pw, mla_q_norm=mla_q_norm, mla_w_uq=mla_w_uq,
             mla_kv_norm=mla_kv_norm, mla_w_ukv=mla_w_ukv, w_out=w_out, post_ln_g=post_ln_g, post_ln_b=post_ln_b)
    depth = w_ada.shape[0]
    bp, seq, _ = x_prompt.shape
    bs, t_new, _ = x_sample.shape
    past_len = cache_latent.shape[2]
    alpha = (2 * depth) ** 0.25
    tile_p = min(IN_TILE, seq)
    chunk_p = min(seq, MLP_CHUNK)
    chunk_s = min(t_new, MLP_CHUNK)

    n_c = bp + bs
    c_all = jnp.concatenate([c_prompt, c_sample, jnp.zeros((-n_c % 8, D_MODEL), F32)], axis=0)
    mod_all = _ada(c_all, w_ada, b_ada)

    ct_p, st_p = _rope_tables(0, seq)
    ct_s, st_s = _rope_tables(past_len, t_new)
    zero_past = jnp.zeros((bp, HIST, W_B), F32)

    xp, xs = x_prompt, x_sample
    p_conv, p_lat, p_kr, s_conv, s_lat, s_kr, s_v = [], [], [], [], [], [], []
    for l in range(depth):
        wp, wsm = _layer_weights(l, p, chunk_p, chunk_s)
        mod_p = mod_all[l, 0:bp].reshape(bp, 1, -1)
        mod_s = mod_all[l, bp:n_c].reshape(bs, 1, -1)

        q, lat, kr, yab, gc, conv, k, vt = _mixer_in(xp, mod_p, ct_p, st_p, zero_past, wp,
                                                      tile=tile_p, chunk=chunk_p, prompt=True)
        o = _attn_prompt(q, k, vt)
        xp = _mixer_out(xp, mod_p, yab, o, gc, wp["w_out"], wp["ln"], tile=tile_p, alpha=alpha)
        p_conv.append(conv[:, HIST - (CONV_W - 1):])
        p_lat.append(lat)
        p_kr.append(kr)

        past = jnp.concatenate([jnp.zeros((bs, HIST - (CONV_W - 1), W_B), F32), state_conv[l]], axis=1)
        q, lat, kr, yab, gc, conv, vst = _mixer_in(xs, mod_s, ct_s, st_s, past, wsm,
                                                   tile=t_new, chunk=chunk_s, prompt=False)
        o = _attn_sample(q, cache_latent, cache_krope, l, lat, kr, wsm)
        xs = _mixer_out(xs, mod_s, yab, o, gc, wsm["w_out"], wsm["ln"], tile=t_new, alpha=alpha)
        s_conv.append(conv[:, HIST - (CONV_W - 1):])
        s_lat.append(lat)
        s_kr.append(kr)
        s_v.append(vst)

    return (xp, xs, jnp.stack(p_conv), jnp.stack(p_lat), jnp.stack(p_kr),
            jnp.stack(s_conv), jnp.stack(s_lat), jnp.stack(s_kr), jnp.stack(s_v))
```

```python
import functools
import math

import jax
import jax.numpy as jnp
from jax import lax
from jax.experimental import pallas as pl
from jax.experimental.pallas import tpu as pltpu

F32 = jnp.float32
BF16 = jnp.bfloat16

D_MODEL = 1024
N_HEADS_A = 4
HEAD_A = 64
W_A = 256
W_B = 256
CONV_W = 31
N_HEADS = 8
QK_NOPE = 64
QK_ROPE = 32
V_DIM = 64
W_C = N_HEADS * V_DIM
Q_LORA = 384
KV_LORA = 256
CHUNK = 64
MLP_CHUNK = 128
ROPE_THETA = 10000.0
ATTN_SCALE = (QK_NOPE + QK_ROPE) ** -0.5
LN_EPS = 1e-5
RMS_EPS = 1e-6

HEAD_LANES = 128
V_ROWS = 80
HIST = 32
ATT_TILE = 256
Q_BLOCK_ROWS = 1024
IN_TILE = 1024
VMEM_LIMIT = 56 * 1024 * 1024

_C_A, _C_B, _C_Q, _C_KV, _C_KR, _C_GC, _C_END = 0, 768, 1536, 1920, 2176, 2432, 2944
_Q_SCALE = ATTN_SCALE * math.log2(math.e)


def _dot(a, b):
    return jnp.dot(a, b, preferred_element_type=F32)


def _dot_nt(a, b):
    return lax.dot_general(a, b, (((1,), (1,)), ((), ())), preferred_element_type=F32)


def _norm_rows(x, eps):
    mu = jnp.mean(x, axis=-1, keepdims=True)
    xc = x - mu
    var = jnp.mean(xc * xc, axis=-1, keepdims=True)
    return xc * lax.rsqrt(var + eps)


def _gelu_tanh(x):
    return x * (0.5 * (1.0 + jnp.tanh(math.sqrt(2.0 / math.pi) * (x + 0.044715 * (x * x * x)))))


def _silu(x):
    return x * jax.nn.sigmoid(x)


def _ada_kernel(c_ref, w_ref, b_ref, o_ref):
    c = c_ref[...]
    a = _silu(c)
    w = w_ref[0]
    a_hi = a.astype(BF16)
    a_lo = (a - a_hi.astype(F32)).astype(BF16)
    w_hi = w.astype(BF16)
    w_lo = (w - w_hi.astype(F32)).astype(BF16)
    o_ref[0] = _dot(a_hi, w_hi) + (_dot(a_hi, w_lo) + _dot(a_lo, w_hi)) + b_ref[0]


def _ada(c_all, w_ada, b_ada):
    depth = w_ada.shape[0]
    rows = c_all.shape[0]
    n_col = w_ada.shape[2] // D_MODEL
    return pl.pallas_call(
        _ada_kernel,
        grid=(depth, n_col),
        in_specs=[
            pl.BlockSpec((rows, D_MODEL), lambda l, j: (0, 0)),
            pl.BlockSpec((1, D_MODEL, D_MODEL), lambda l, j: (l, 0, j)),
            pl.BlockSpec((1, 1, D_MODEL), lambda l, j: (l, 0, j)),
        ],
        out_specs=pl.BlockSpec((1, rows, D_MODEL), lambda l, j: (l, 0, j)),
        out_shape=jax.ShapeDtypeStruct((depth, rows, w_ada.shape[2]), F32),
        compiler_params=pltpu.CompilerParams(vmem_limit_bytes=VMEM_LIMIT),
        name="ada_mod",
    )(c_all, w_ada, b_ada.reshape(depth, 1, -1))


def _mixer_in_kernel(x_ref, mod_ref, ct_ref, st_ref, past_ref, w_in_ref, vec_ref, qn_ref, ws_ref, bs_ref,
                     dwk_ref, w_pw_ref, w_uq_ref, *rest, tile, chunk, prompt):
    if prompt:
        (w_uk_ref, w_uvt_ref, q_ref, lat_ref, kr_ref, yab_ref, gc_ref, conv_ref, k_ref, vt_ref, hist) = rest
    else:
        (q_ref, lat_ref, kr_ref, yab_ref, gc_ref, conv_ref, vst_ref, hist) = rest
    t = pl.program_id(1)

    mod = mod_ref[0]
    shift = mod[:, 0:D_MODEL]
    scale = mod[:, D_MODEL:2 * D_MODEL]
    h = _norm_rows(x_ref[0], LN_EPS) * (1.0 + scale) + shift
    hb = h.astype(BF16)

    vec = vec_ref[...]
    gmlp_g, gmlp_b = vec[0:1], vec[1:2]
    dw_b, conv_g, conv_b, b_pw, kv_norm = vec[2:3], vec[3:4], vec[4:5], vec[5:6], vec[6:7]

    za = _dot(hb, w_in_ref[:, _C_A:_C_B])
    u = _gelu_tanh(za[:, 0:W_A])
    v = _norm_rows(_gelu_tanh(za[:, W_A:2 * W_A]), LN_EPS) * gmlp_g + gmlp_b
    if not prompt:
        vst_ref[0] = v
    vb = v.astype(BF16)
    rows = lax.broadcasted_iota(jnp.int32, (N_HEADS_A * chunk, W_A), 0)
    cols = lax.broadcasted_iota(jnp.int32, (N_HEADS_A * chunk, W_A), 1)
    own_head = (rows // chunk) == (cols // HEAD_A)
    ws = ws_ref[...]
    bs = bs_ref[...]
    mixed = []
    for c in range(tile // chunk):
        vc = vb[c * chunk:(c + 1) * chunk]
        vbd = jnp.where(own_head, jnp.concatenate([vc] * N_HEADS_A, axis=0), jnp.zeros((), BF16))
        mixed.append(_dot(ws, vbd) + bs)
    s = mixed[0] if len(mixed) == 1 else jnp.concatenate(mixed, axis=0)
    y_a = u * s * _silu(za[:, 2 * W_A:3 * W_A])
    yab_ref[0, :, 0:W_A] = y_a.astype(BF16)

    zb = _dot(hb, w_in_ref[:, _C_B:_C_Q])
    g = zb[:, 0:W_B] * jax.nn.sigmoid(zb[:, W_B:2 * W_B])

    @pl.when(t == 0)
    def _():
        hist[0:HIST] = past_ref[0]

    hist[HIST:HIST + tile] = g
    dwk = dwk_ref[...]
    y = jnp.broadcast_to(dw_b, (tile, W_B))
    first = HIST - (CONV_W - 1)
    grouped = tile >= 8 * 8
    for r in range(8 if grouped else 0):
        rows = tile if r == 0 else tile + 8
        part = None
        for a in range(HIST // 8 + 1):
            k = 8 * a + r - first
            if 0 <= k < CONV_W:
                term = dwk[k:k + 1] * hist[8 * a:8 * a + rows]
                part = term if part is None else part + term
        y = y + (part if r == 0 else part[r:r + tile])
    for k in range(0 if grouped else CONV_W):
        y = y + dwk[k:k + 1] * hist[first + k:first + k + tile]
    new_hist = hist[tile:tile + HIST]
    hist[0:HIST] = new_hist
    conv_ref[0] = new_hist
    y = _silu(_norm_rows(y, LN_EPS) * conv_g + conv_b)
    y = _dot(y.astype(BF16), w_pw_ref[...]) + b_pw
    y_b = y * _silu(zb[:, 2 * W_B:3 * W_B])
    yab_ref[0, :, W_A:W_A + W_B] = y_b.astype(BF16)

    ct = ct_ref[...]
    st = st_ref[...]
    zq = _dot(hb, w_in_ref[:, _C_Q:_C_KV])
    cq = zq * lax.rsqrt(jnp.mean(zq * zq, axis=-1, keepdims=True) + RMS_EPS) * qn_ref[...]
    q2 = _dot(cq.astype(BF16), w_uq_ref[...])
    ctq = ct * _Q_SCALE
    stq = st * _Q_SCALE
    for hd in range(N_HEADS):
        lo = hd * HEAD_LANES
        qh = q2[:, lo:lo + HEAD_LANES] * ctq + q2[:, N_HEADS * HEAD_LANES + lo:N_HEADS * HEAD_LANES + lo + HEAD_LANES] * stq
        q_ref[0, :, lo:lo + HEAD_LANES] = qh.astype(BF16)

    zkv = _dot(hb, w_in_ref[:, _C_KV:_C_KR])
    lat = zkv * lax.rsqrt(jnp.mean(zkv * zkv, axis=-1, keepdims=True) + RMS_EPS) * kv_norm
    lat_ref[0] = lat
    zkr = _dot(hb, w_in_ref[:, _C_KR:_C_GC])
    krot = zkr[:, 0:HEAD_LANES] * ct + zkr[:, HEAD_LANES:2 * HEAD_LANES] * st
    kr_ref[0] = krot[:, QK_NOPE:QK_NOPE + QK_ROPE]

    if prompt:
        latb = lat.astype(BF16)
        kn = _dot(latb, w_uk_ref[...])
        for hd in range(N_HEADS):
            lo = hd * HEAD_LANES
            k_ref[0, :, lo:lo + HEAD_LANES] = (kn[:, lo:lo + HEAD_LANES] + krot).astype(BF16)
        vt = _dot_nt(w_uvt_ref[...], latb)
        vrow = lax.broadcasted_iota(jnp.int32, vt.shape, 0)
        vt = jnp.where((vrow % V_ROWS) == V_DIM, 1.0, vt)
        for c in range(tile // ATT_TILE):
            blk = vt[:, c * ATT_TILE:(c + 1) * ATT_TILE].reshape(N_HEADS, V_ROWS, ATT_TILE)
            vt_ref[0, :, c] = blk.astype(BF16)

    zg = _dot(hb, w_in_ref[:, _C_GC:_C_END])
    gc_ref[0] = _silu(zg).astype(BF16)


def _mixer_in(x, mod, ctab, stab, past, wts, *, tile, chunk, prompt):
    bsz, seq, _ = x.shape
    n_t = seq // tile
    kern = functools.partial(_mixer_in_kernel, tile=tile, chunk=chunk, prompt=prompt)

    def full(a):
        return pl.BlockSpec(a.shape, lambda b, t, _n=a.ndim: (0,) * _n)

    weights = [wts["w_in"], wts["vec"], wts["q_norm"], wts["ws"], wts["bs"], wts["dwk"], wts["w_pw"], wts["w_uq"]]
    if prompt:
        weights += [wts["w_uk"], wts["w_uvt"]]
    in_specs = [
        pl.BlockSpec((1, tile, D_MODEL), lambda b, t: (b, t, 0)),
        pl.BlockSpec((1, 1, 3 * D_MODEL), lambda b, t: (b, 0, 0)),
        pl.BlockSpec((tile, HEAD_LANES), lambda b, t: (t, 0)),
        pl.BlockSpec((tile, HEAD_LANES), lambda b, t: (t, 0)),
        pl.BlockSpec((1, HIST, W_B), lambda b, t: (b, 0, 0)),
    ] + [full(w) for w in weights]
    tok = lambda width: pl.BlockSpec((1, tile, width), lambda b, t: (b, t, 0))
    out_specs = [tok(N_HEADS * HEAD_LANES), tok(KV_LORA), tok(QK_ROPE), tok(W_A + W_B), tok(W_C),
                 pl.BlockSpec((1, HIST, W_B), lambda b, t: (b, 0, 0))]
    out_shape = [
        jax.ShapeDtypeStruct((bsz, seq, N_HEADS * HEAD_LANES), BF16),
        jax.ShapeDtypeStruct((bsz, seq, KV_LORA), F32),
        jax.ShapeDtypeStruct((bsz, seq, QK_ROPE), F32),
        jax.ShapeDtypeStruct((bsz, seq, W_A + W_B), BF16),
        jax.ShapeDtypeStruct((bsz, seq, W_C), BF16),
        jax.ShapeDtypeStruct((bsz, HIST, W_B), F32),
    ]
    if prompt:
        n_kt = tile // ATT_TILE
        out_specs += [tok(N_HEADS * HEAD_LANES),
                      pl.BlockSpec((1, N_HEADS, n_kt, V_ROWS, ATT_TILE), lambda b, t: (b, 0, t, 0, 0))]
        out_shape += [jax.ShapeDtypeStruct((bsz, seq, N_HEADS * HEAD_LANES), BF16),
                      jax.ShapeDtypeStruct((bsz, N_HEADS, seq // ATT_TILE, V_ROWS, ATT_TILE), BF16)]
    else:
        out_specs += [tok(W_A)]
        out_shape += [jax.ShapeDtypeStruct((bsz, seq, W_A), F32)]
    return pl.pallas_call(
        kern,
        grid=(bsz, n_t),
        in_specs=in_specs,
        out_specs=out_specs,
        out_shape=out_shape,
        scratch_shapes=[pltpu.VMEM((HIST + tile, W_B), F32)],
        compiler_params=pltpu.CompilerParams(dimension_semantics=("arbitrary", "arbitrary"),
                                             vmem_limit_bytes=VMEM_LIMIT),
        name="mixer_in_prompt" if prompt else "mixer_in_sample",
    )(x, mod, ctab, stab, past, *weights)


_S_ADDR = (0, 64)
_O_ADDR = ((128, 148), (168, 188))
_GUARD_ADDR = 208
_FIXED_TILES = 5


def _attn_prompt_kernel(q_ref, k_ref, vt_ref, bias_ref, o_ref, qt_sc, s_sc, acc_sc, *, n_sub, n_kt):
    qi = pl.program_id(2)

    def tile_index(t):
        return jnp.minimum(t, n_kt - 1)

    class SubTile:
        def __init__(self, sub):
            self.sp = sub % 2
            self.row0 = sub * ATT_TILE
            self.g = qi * n_sub + sub
            self.n_steady = jnp.maximum((self.g - 3) // 2, 0)
            self.j0 = _FIXED_TILES + 2 * self.n_steady - 3
            q_t = q_ref[0, self.row0:self.row0 + ATT_TILE, :].astype(F32).T
            zero = jnp.zeros((HEAD_LANES, ATT_TILE), BF16)
            for hd in range(2):
                own = slice(hd * HEAD_LANES, (hd + 1) * HEAD_LANES)
                other = slice((1 - hd) * HEAD_LANES, (2 - hd) * HEAD_LANES)
                qt_sc[self.sp, hd, own] = q_t[own].astype(BF16)
                qt_sc[self.sp, hd, other] = zero
            acc_sc[self.sp] = jnp.zeros(acc_sc.shape[1:], F32)
            neg = jnp.full((1, ATT_TILE), -jnp.inf, F32)
            one = jnp.ones((1, ATT_TILE), F32)
            self.state = ([neg, neg], [one, one], [one, one], [one, one])

        def stage_qt(self):
            for hd in range(2):
                pltpu.matmul_push_rhs(qt_sc[self.sp, hd], staging_register=0, mxu_index=hd)

        def stage_a(self, t, par, restage):
            k_tile = k_ref[0, pl.ds(pl.multiple_of(tile_index(t) * ATT_TILE, ATT_TILE), ATT_TILE), :]
            for hd in range(2):
                pltpu.matmul_acc_lhs(_S_ADDR[par], k_tile, mxu_index=hd, load_staged_rhs=0)
                if restage:
                    pltpu.matmul_push_rhs(qt_sc[self.sp, hd], staging_register=0, mxu_index=hd)

        def stage_b(self, t, par, masked, m_old=None):
            m_old = self.state[0] if m_old is None else m_old
            m_new, alpha = [], []
            if masked:
                bias = bias_ref[jnp.where(t < self.g, 0, jnp.where(t == self.g, 1, 2))]
            for hd in range(2):
                s = pltpu.matmul_pop(_S_ADDR[par], (ATT_TILE, ATT_TILE), F32, mxu_index=hd)
                if masked:
                    s = s + bias
                m_hd = jnp.maximum(m_old[hd], jnp.max(s, axis=0, keepdims=True))
                alpha.append(jnp.exp2(m_old[hd] - m_hd))
                m_new.append(m_hd)
                s_sc[self.sp, hd, par] = s
            return m_new, alpha

        def stage_c(self, t, par, m_t=None):
            m_t = self.state[0] if m_t is None else m_t
            for hd in range(2):
                p = jnp.exp2(s_sc[self.sp, hd, par] - m_t[hd]).astype(BF16)
                pltpu.matmul_push_rhs(p, staging_register=1, mxu_index=hd)
                pltpu.matmul_acc_lhs(_O_ADDR[self.sp][par], vt_ref[0, hd, tile_index(t)], mxu_index=hd,
                                     load_staged_rhs=1)

        def stage_d(self, par, alpha_t=None):
            alpha_t = self.state[3] if alpha_t is None else alpha_t
            for hd in range(2):
                o = pltpu.matmul_pop(_O_ADDR[self.sp][par], (V_ROWS, ATT_TILE), F32, mxu_index=hd)
                acc_sc[self.sp, hd] = alpha_t[hd] * acc_sc[self.sp, hd] + o

        def finish(self):
            outs = []
            for hd in range(2):
                acc = acc_sc[self.sp, hd]
                outs.append(acc[0:V_DIM] * (1.0 / acc[V_DIM:V_DIM + 1]))
            o_t = jnp.concatenate(outs, axis=0)
            o_ref[0, self.row0:self.row0 + ATT_TILE, :] = o_t.T.astype(BF16)

    def run(steps):
        for st, j, par, f in steps:
            if f.get("d", True):
                st.stage_d(par)
        new = []
        for st, j, par, f in steps:
            m, al1, al2, al3 = st.state
            if f.get("b", True):
                m_next, al0 = st.stage_b(j + 1, 1 - par, f.get("masked", False))
            else:
                m_next, al0 = m, al1
            new.append((m_next, al0, al1, al2))
        for st, j, par, f in steps:
            if f.get("a", True):
                st.stage_a(j + 2, par, f.get("restage", True))
            if f.get("qt_of") is not None:
                f["qt_of"].stage_qt()
        for st, j, par, f in steps:
            if f.get("c", True):
                st.stage_c(j, par)
        for (st, j, par, f), state in zip(steps, new):
            st.state = state

    def steady_trip(st, j):
        m0, al1, al2, al3 = st.state
        for hd in range(2):
            pltpu.matmul_acc_lhs(_GUARD_ADDR, jnp.zeros((16, 2 * HEAD_LANES), BF16), mxu_index=hd)
        m1, a1 = st.stage_b(j + 1, 1, False, m0)
        st.stage_a(j + 2, 0, True)
        st.stage_a(j + 3, 1, True)
        for hd in range(2):
            pltpu.matmul_pop(_GUARD_ADDR, (16, ATT_TILE), F32, mxu_index=hd)
        st.stage_d(0, al3)
        st.stage_c(j, 0, m0)
        st.stage_d(1, al2)
        m2, a2 = st.stage_b(j + 2, 0, False, m1)
        st.stage_c(j + 1, 1, m1)
        st.state = (m2, a2, a1, al1)

    fill = [dict(b=False, c=False, d=False), dict(c=False, d=False, masked=True),
            dict(d=False, masked=True), dict(d=False, masked=True)]

    def drain(nxt):
        return [dict(masked=True, restage=False, qt_of=nxt), dict(a=False, masked=True),
                dict(a=False, b=False), dict(a=False, b=False, c=False), dict(a=False, b=False, c=False)]

    cur = SubTile(0)
    cur.stage_qt()
    for i in range(4):
        run([(cur, i - 2, i % 2, fill[i])])
    for sub in range(n_sub):
        def steady(u, state, st=cur):
            st.state = state
            steady_trip(st, 2 + 2 * u)
            return st.state

        cur.state = lax.fori_loop(0, cur.n_steady, steady, cur.state)
        nxt = SubTile(sub + 1) if sub + 1 < n_sub else None
        dr = drain(nxt)
        run([(cur, cur.j0, 0, dr[0])])
        for i in range(4):
            steps = [(cur, cur.j0 + 1 + i, (i + 1) % 2, dr[i + 1])]
            if nxt is not None:
                steps.append((nxt, i - 2, i % 2, fill[i]))
            run(steps)
        cur.finish()
        cur = nxt


def _attn_prompt(q, k, vt):
    bsz, seq, _ = q.shape
    qb = min(Q_BLOCK_ROWS, seq)
    n_sub = qb // ATT_TILE
    n_kt = seq // ATT_TILE
    kern = functools.partial(_attn_prompt_kernel, n_sub=n_sub, n_kt=n_kt)
    idx = jnp.arange(ATT_TILE) // CHUNK
    diag = jnp.where(idx[:, None] <= idx[None, :], 0.0, -jnp.inf).astype(F32)
    bias = jnp.stack([jnp.zeros_like(diag), diag, jnp.full_like(diag, -jnp.inf)])
    return pl.pallas_call(
        kern,
        grid=(bsz, N_HEADS // 2, seq // qb),
        in_specs=[
            pl.BlockSpec((1, qb, 2 * HEAD_LANES), lambda b, hp, i: (b, i, hp)),
            pl.BlockSpec((1, seq, 2 * HEAD_LANES), lambda b, hp, i: (b, 0, hp)),
            pl.BlockSpec((1, 2, n_kt, V_ROWS, ATT_TILE), lambda b, hp, i: (b, hp, 0, 0, 0)),
            pl.BlockSpec((3, ATT_TILE, ATT_TILE), lambda b, hp, i: (0, 0, 0)),
        ],
        out_specs=pl.BlockSpec((1, qb, 2 * V_DIM), lambda b, hp, i: (b, i, hp)),
        out_shape=jax.ShapeDtypeStruct((bsz, seq, W_C), BF16),
        scratch_shapes=[pltpu.VMEM((2, 2, 2 * HEAD_LANES, ATT_TILE), BF16),
                        pltpu.VMEM((2, 2, 2, ATT_TILE, ATT_TILE), F32),
                        pltpu.VMEM((2, 2, V_ROWS, ATT_TILE), F32)],
        compiler_params=pltpu.CompilerParams(dimension_semantics=("arbitrary", "arbitrary", "arbitrary"),
                                             vmem_limit_bytes=VMEM_LIMIT),
        name="attn_prompt",
    )(q, k, vt, bias)


def _attn_sample_kernel(q_ref, latc_ref, krc_ref, latn_ref, krn_ref, wabs_ref, fold_ref, wuv_ref, o_ref, *, t_new):
    n_rows = N_HEADS * t_new
    q = q_ref[0]
    qrep = jnp.concatenate([q] * N_HEADS, axis=0)
    r1 = lax.broadcasted_iota(jnp.int32, qrep.shape, 0)
    c1 = lax.broadcasted_iota(jnp.int32, qrep.shape, 1)
    qrep = jnp.where((r1 // t_new) == (c1 // HEAD_LANES), qrep, jnp.zeros((), BF16))
    qlat = _dot(qrep, wabs_ref[...]).astype(BF16)
    qr = _dot(qrep, fold_ref[...])[:, 0:QK_ROPE].astype(BF16)
    latc = latc_ref[0, 0].astype(BF16)
    krc = krc_ref[0, 0].astype(BF16)
    latn = latn_ref[0].astype(BF16)
    krn = krn_ref[0].astype(BF16)
    s_c = _dot_nt(qlat, latc) + _dot_nt(qr, krc)
    s_n = _dot_nt(qlat, latn) + _dot_nt(qr, krn)
    m = jnp.maximum(jnp.max(s_c, axis=-1, keepdims=True), jnp.max(s_n, axis=-1, keepdims=True))
    p_c = jnp.exp2(s_c - m)
    p_n = jnp.exp2(s_n - m)
    den = jnp.sum(p_c, axis=-1, keepdims=True) + jnp.sum(p_n, axis=-1, keepdims=True)
    olat = (_dot(p_c.astype(BF16), latc) + _dot(p_n.astype(BF16), latn)) * (1.0 / den)
    of = _dot(olat.astype(BF16), wuv_ref[...])
    r2 = lax.broadcasted_iota(jnp.int32, of.shape, 0)
    c2 = lax.broadcasted_iota(jnp.int32, of.shape, 1)
    of = jnp.where((r2 // t_new) == (c2 // V_DIM), of, 0.0)
    o = of[0:t_new]
    for hd in range(1, N_HEADS):
        o = o + of[hd * t_new:(hd + 1) * t_new]
    o_ref[0] = o.astype(BF16)
    del n_rows


def _attn_sample(q, lat_cache, kr_cache, layer, lat_new, kr_new, wts):
    bsz, t_new, _ = q.shape
    past = lat_cache.shape[2]
    kern = functools.partial(_attn_sample_kernel, t_new=t_new)

    def full(a):
        return pl.BlockSpec(a.shape, lambda b, _n=a.ndim: (0,) * _n)

    return pl.pallas_call(
        kern,
        grid=(bsz,),
        in_specs=[
            pl.BlockSpec((1, t_new, N_HEADS * HEAD_LANES), lambda b: (b, 0, 0)),
            pl.BlockSpec((1, 1, past, KV_LORA), lambda b: (layer, b, 0, 0)),
            pl.BlockSpec((1, 1, past, QK_ROPE), lambda b: (layer, b, 0, 0)),
            pl.BlockSpec((1, t_new, KV_LORA), lambda b: (b, 0, 0)),
            pl.BlockSpec((1, t_new, QK_ROPE), lambda b: (b, 0, 0)),
            full(wts["wabs"]), full(wts["fold"]), full(wts["w_uv"]),
        ],
        out_specs=pl.BlockSpec((1, t_new, W_C), lambda b: (b, 0, 0)),
        out_shape=jax.ShapeDtypeStruct((bsz, t_new, W_C), BF16),
        compiler_params=pltpu.CompilerParams(dimension_semantics=("arbitrary",), vmem_limit_bytes=VMEM_LIMIT),
        name="attn_sample",
    )(q, lat_cache, kr_cache, lat_new, kr_new, wts["wabs"], wts["fold"], wts["w_uv"])


def _mixer_out_kernel(x_ref, mod_ref, yab_ref, o_ref, gc_ref, w_out_ref, ln_ref, out_ref, *, alpha):
    gate = mod_ref[0][:, 2 * D_MODEL:3 * D_MODEL]
    yc = (o_ref[0].astype(F32) * gc_ref[0].astype(F32)).astype(BF16)
    y = _dot(yab_ref[0], w_out_ref[0:W_A + W_B]) + _dot(yc, w_out_ref[W_A + W_B:W_A + W_B + W_C])
    r = alpha * x_ref[0] + gate * y
    ln = ln_ref[...]
    out_ref[0] = _norm_rows(r, LN_EPS) * ln[0:1] + ln[1:2]


def _mixer_out(x, mod, yab, o, gc, w_out, ln, *, tile, alpha):
    bsz, seq, _ = x.shape
    tok = lambda width: pl.BlockSpec((1, tile, width), lambda b, t: (b, t, 0))
    return pl.pallas_call(
        functools.partial(_mixer_out_kernel, alpha=alpha),
        grid=(bsz, seq // tile),
        in_specs=[tok(D_MODEL), pl.BlockSpec((1, 1, 3 * D_MODEL), lambda b, t: (b, 0, 0)),
                  tok(W_A + W_B), tok(W_C), tok(W_C),
                  pl.BlockSpec(w_out.shape, lambda b, t: (0, 0)),
                  pl.BlockSpec(ln.shape, lambda b, t: (0, 0))],
        out_specs=tok(D_MODEL),
        out_shape=jax.ShapeDtypeStruct(x.shape, F32),
        compiler_params=pltpu.CompilerParams(dimension_semantics=("arbitrary", "arbitrary"),
                                             vmem_limit_bytes=VMEM_LIMIT),
        name="mixer_out",
    )(x, mod, yab, o, gc, w_out, ln)


def _rope_tables(pos0, n):
    inv = ROPE_THETA ** (-jnp.arange(0, QK_ROPE, 2, dtype=F32) / QK_ROPE)
    ang = (pos0 + jnp.arange(n, dtype=jnp.int32)).astype(F32)[:, None] * inv[None, :]
    cos, sin = lax.optimization_barrier((jnp.cos(ang), jnp.sin(ang)))
    ones = jnp.ones((n, QK_NOPE), F32)
    z64 = jnp.zeros((n, QK_NOPE), F32)
    z32 = jnp.zeros((n, HEAD_LANES - QK_NOPE - QK_ROPE), F32)
    return jnp.concatenate([ones, cos, cos, z32], axis=1), jnp.concatenate([z64, sin, sin, z32], axis=1)


def _gmlp_weights(ws, b_s, chunk):
    idx = jnp.arange(chunk)
    mask = (idx[None, :] // CHUNK) <= (idx[:, None] // CHUNK)
    wsm = jnp.where(mask[None], ws[:, :chunk, :chunk], 0.0)
    ws_all = jnp.transpose(wsm, (1, 0, 2)).reshape(chunk, N_HEADS_A * chunk)
    bs_tab = jnp.repeat(b_s[:, :chunk].T, HEAD_A, axis=1)
    return ws_all.astype(BF16), bs_tab.astype(F32)


def _layer_weights(l, p, chunk_p, chunk_s):
    half = QK_ROPE // 2
    w_in = p["w_in"][l]
    krw = w_in[:, 2176:2208]
    x1, x2 = krw[:, :half], krw[:, half:]
    z64 = jnp.zeros((D_MODEL, QK_NOPE), F32)
    z32 = jnp.zeros((D_MODEL, HEAD_LANES - QK_NOPE - QK_ROPE), F32)
    w_in2 = jnp.concatenate([w_in[:, :2176], z64, x1, x2, z32, z64, -x2, x1, z32, w_in[:, 2208:]], axis=1)

    wq = p["mla_w_uq"][l].reshape(Q_LORA, N_HEADS, QK_NOPE + QK_ROPE)
    qn, q1, q2 = wq[..., :QK_NOPE], wq[..., QK_NOPE:QK_NOPE + half], wq[..., QK_NOPE + half:]
    zq32 = jnp.zeros((Q_LORA, N_HEADS, HEAD_LANES - QK_NOPE - QK_ROPE), F32)
    zq64 = jnp.zeros((Q_LORA, N_HEADS, QK_NOPE), F32)
    w_uq = jnp.concatenate([
        jnp.concatenate([qn, q1, q2, zq32], axis=-1).reshape(Q_LORA, -1),
        jnp.concatenate([zq64, -q2, q1, zq32], axis=-1).reshape(Q_LORA, -1)], axis=1)

    wkv = p["mla_w_ukv"][l].reshape(KV_LORA, N_HEADS, QK_NOPE + V_DIM)
    wk, wv = wkv[..., :QK_NOPE], wkv[..., QK_NOPE:]
    w_uk = jnp.concatenate([wk, jnp.zeros_like(wk)], axis=-1).reshape(KV_LORA, -1)
    wvt = jnp.transpose(wv, (1, 2, 0))
    w_uvt = jnp.concatenate([wvt, jnp.zeros((N_HEADS, V_ROWS - V_DIM, KV_LORA), F32)], axis=1).reshape(-1, KV_LORA)
    wkt = jnp.transpose(wk, (1, 2, 0))
    wabs = jnp.concatenate([wkt, jnp.zeros((N_HEADS, HEAD_LANES - QK_NOPE, KV_LORA), F32)], axis=1).reshape(-1, KV_LORA)
    eye = jnp.eye(QK_ROPE, HEAD_LANES, dtype=F32)
    fold_h = jnp.concatenate([jnp.zeros((QK_NOPE, HEAD_LANES), F32), eye,
                              jnp.zeros((HEAD_LANES - QK_NOPE - QK_ROPE, HEAD_LANES), F32)], axis=0)
    fold = jnp.tile(fold_h, (N_HEADS, 1))

    zrow = jnp.zeros((1, W_B), F32)
    vec = jnp.stack([p["gmlp_ln_g"][l], p["gmlp_ln_b"][l], p["conv_dw_b"][l], p["conv_ln_g"][l],
                     p["conv_ln_b"][l], p["conv_b_pw"][l], p["mla_kv_norm"][l], zrow[0]], axis=0)
    common = dict(
        w_in=w_in2.astype(BF16), vec=vec, q_norm=p["mla_q_norm"][l].reshape(1, Q_LORA),
        dwk=jnp.concatenate([p["conv_dw_k"][l], zrow], axis=0), w_pw=p["conv_w_pw"][l].astype(BF16),
        w_uq=w_uq.astype(BF16), w_uk=w_uk.astype(BF16), w_uvt=w_uvt.astype(BF16),
        wabs=wabs.astype(BF16), fold=fold.astype(BF16), w_uv=wv.reshape(KV_LORA, W_C).astype(BF16),
        w_out=p["w_out"][l].astype(BF16), ln=jnp.stack([p["post_ln_g"][l], p["post_ln_b"][l]], axis=0))
    ws_p, bs_p = _gmlp_weights(p["gmlp_ws"][l], p["gmlp_bs"][l], chunk_p)
    ws_s, bs_s = _gmlp_weights(p["gmlp_ws"][l], p["gmlp_bs"][l], chunk_s)
    return dict(common, ws=ws_p, bs=bs_p), dict(common, ws=ws_s, bs=bs_s)


def kernel(x_prompt, x_sample, cache_latent, cache_krope, state_conv, c_prompt, c_sample,
           w_ada, b_ada, w_in, gmlp_ln_g, gmlp_ln_b, gmlp_ws, gmlp_bs,
           conv_dw_k, conv_dw_b, conv_ln_g, conv_ln_b, conv_w_pw, conv_b_pw,
           mla_q_norm, mla_w_uq, mla_kv_norm, mla_w_ukv, w_out, post_ln_g, post_ln_b):
    p = dict(w_in=w_in, gmlp_ln_g=gmlp_ln_g, gmlp_ln_b=gmlp_ln_b, gmlp_ws=gmlp_ws, gmlp_bs=gmlp_bs,
             conv_dw_k=conv_dw_k, conv_dw_b=conv_dw_b, conv_ln_g=conv_ln_g, conv_ln_b=conv_ln_b,
             conv_w_pw=conv_w_pw, conv_b_pw=conv_b_pw, mla_q_norm=mla_q_norm, mla_w_uq=mla_w_uq,
             mla_kv_norm=mla_kv_norm, mla_w_ukv=mla_w_ukv, w_out=w_out, post_ln_g=post_ln_g, post_ln_b=post_ln_b)
    depth = w_ada.shape[0]
    bp, seq, _ = x_prompt.shape
    bs, t_new, _ = x_sample.shape
    past_len = cache_latent.shape[2]
    alpha = (2 * depth) ** 0.25
    tile_p = min(IN_TILE, seq)
    chunk_p = min(seq, MLP_CHUNK)
    chunk_s = min(t_new, MLP_CHUNK)

    n_c = bp + bs
    c_all = jnp.concatenate([c_prompt, c_sample, jnp.zeros((-n_c % 8, D_MODEL), F32)], axis=0)
    mod_all = _ada(c_all, w_ada, b_ada)

    ct_p, st_p = _rope_tables(0, seq)
    ct_s, st_s = _rope_tables(past_len, t_new)
    zero_past = jnp.zeros((bp, HIST, W_B), F32)

    xp, xs = x_prompt, x_sample
    p_conv, p_lat, p_kr, s_conv, s_lat, s_kr, s_v = [], [], [], [], [], [], []
    for l in range(depth):
        wp, wsm = _layer_weights(l, p, chunk_p, chunk_s)
        mod_p = mod_all[l, 0:bp].reshape(bp, 1, -1)
        mod_s = mod_all[l, bp:n_c].reshape(bs, 1, -1)

        q, lat, kr, yab, gc, conv, k, vt = _mixer_in(xp, mod_p, ct_p, st_p, zero_past, wp,
                                                      tile=tile_p, chunk=chunk_p, prompt=True)
        o = _attn_prompt(q, k, vt)
        xp = _mixer_out(xp, mod_p, yab, o, gc, wp["w_out"], wp["ln"], tile=tile_p, alpha=alpha)
        p_conv.append(conv[:, HIST - (CONV_W - 1):])
        p_lat.append(lat)
        p_kr.append(kr)

        past = jnp.concatenate([jnp.zeros((bs, HIST - (CONV_W - 1), W_B), F32), state_conv[l]], axis=1)
        q, lat, kr, yab, gc, conv, vst = _mixer_in(xs, mod_s, ct_s, st_s, past, wsm,
                                                   tile=t_new, chunk=chunk_s, prompt=False)
        o = _attn_sample(q, cache_latent, cache_krope, l, lat, kr, wsm)
        xs = _mixer_out(xs, mod_s, yab, o, gc, wsm["w_out"], wsm["ln"], tile=t_new, alpha=alpha)
        s_conv.append(conv[:, HIST - (CONV_W - 1):])
        s_lat.append(lat)
        s_kr.append(kr)
        s_v.append(vst)

    return (xp, xs, jnp.stack(p_conv), jnp.stack(p_lat), jnp.stack(p_kr),
            jnp.stack(s_conv), jnp.stack(s_lat), jnp.stack(s_kr), jnp.stack(s_v))
```

```python
import functools
import math

import jax
import jax.numpy as jnp
from jax import lax
from jax.experimental import pallas as pl
from jax.experimental.pallas import tpu as pltpu

F32 = jnp.float32
BF16 = jnp.bfloat16

D_MODEL = 1024
N_HEADS_A = 4
HEAD_A = 64
W_A = 256
W_B = 256
CONV_W = 31
N_HEADS = 8
QK_NOPE = 64
QK_ROPE = 32
V_DIM = 64
W_C = N_HEADS * V_DIM
Q_LORA = 384
KV_LORA = 256
CHUNK = 64
MLP_CHUNK = 128
ROPE_THETA = 10000.0
ATTN_SCALE = (QK_NOPE + QK_ROPE) ** -0.5
LN_EPS = 1e-5
RMS_EPS = 1e-6

HEAD_LANES = 128
V_ROWS = 80
HIST = 32
ATT_TILE = 256
Q_BLOCK_ROWS = 1024
IN_TILE = 1024
VMEM_LIMIT = 56 * 1024 * 1024

_C_A, _C_B, _C_Q, _C_KV, _C_KR, _C_GC, _C_END = 0, 768, 1536, 1920, 2176, 2432, 2944
_Q_SCALE = ATTN_SCALE * math.log2(math.e)


def _dot(a, b):
    return jnp.dot(a, b, preferred_element_type=F32)


def _dot_nt(a, b):
    return lax.dot_general(a, b, (((1,), (1,)), ((), ())), preferred_element_type=F32)


def _norm_rows(x, eps):
    mu = jnp.mean(x, axis=-1, keepdims=True)
    xc = x - mu
    var = jnp.mean(xc * xc, axis=-1, keepdims=True)
    return xc * lax.rsqrt(var + eps)


def _gelu_tanh(x):
    return x * (0.5 * (1.0 + jnp.tanh(math.sqrt(2.0 / math.pi) * (x + 0.044715 * (x * x * x)))))


def _silu(x):
    return x * jax.nn.sigmoid(x)


def _ada_kernel(c_ref, w_ref, b_ref, o_ref):
    c = c_ref[...]
    a = _silu(c)
    w = w_ref[0]
    a_hi = a.astype(BF16)
    a_lo = (a - a_hi.astype(F32)).astype(BF16)
    w_hi = w.astype(BF16)
    w_lo = (w - w_hi.astype(F32)).astype(BF16)
    o_ref[0] = _dot(a_hi, w_hi) + (_dot(a_hi, w_lo) + _dot(a_lo, w_hi)) + b_ref[0]


def _ada(c_all, w_ada, b_ada):
    depth = w_ada.shape[0]
    rows = c_all.shape[0]
    n_col = w_ada.shape[2] // D_MODEL
    return pl.pallas_call(
        _ada_kernel,
        grid=(depth, n_col),
        in_specs=[
            pl.BlockSpec((rows, D_MODEL), lambda l, j: (0, 0)),
            pl.BlockSpec((1, D_MODEL, D_MODEL), lambda l, j: (l, 0, j)),
            pl.BlockSpec((1, 1, D_MODEL), lambda l, j: (l, 0, j)),
        ],
        out_specs=pl.BlockSpec((1, rows, D_MODEL), lambda l, j: (l, 0, j)),
        out_shape=jax.ShapeDtypeStruct((depth, rows, w_ada.shape[2]), F32),
        compiler_params=pltpu.CompilerParams(vmem_limit_bytes=VMEM_LIMIT),
        name="ada_mod",
    )(c_all, w_ada, b_ada.reshape(depth, 1, -1))


def _mixer_in_kernel(x_ref, mod_ref, ct_ref, st_ref, past_ref, w_in_ref, vec_ref, qn_ref, ws_ref, bs_ref,
                     dwk_ref, w_pw_ref, w_uq_ref, *rest, tile, chunk, prompt):
    if prompt:
        (w_uk_ref, w_uvt_ref, q_ref, lat_ref, kr_ref, yab_ref, gc_ref, conv_ref, k_ref, vt_ref, hist) = rest
    else:
        (q_ref, lat_ref, kr_ref, yab_ref, gc_ref, conv_ref, vst_ref, hist) = rest
    t = pl.program_id(1)

    mod = mod_ref[0]
    shift = mod[:, 0:D_MODEL]
    scale = mod[:, D_MODEL:2 * D_MODEL]
    h = _norm_rows(x_ref[0], LN_EPS) * (1.0 + scale) + shift
    hb = h.astype(BF16)

    vec = vec_ref[...]
    gmlp_g, gmlp_b = vec[0:1], vec[1:2]
    dw_b, conv_g, conv_b, b_pw, kv_norm = vec[2:3], vec[3:4], vec[4:5], vec[5:6], vec[6:7]

    za = _dot(hb, w_in_ref[:, _C_A:_C_B])
    u = _gelu_tanh(za[:, 0:W_A])
    v = _norm_rows(_gelu_tanh(za[:, W_A:2 * W_A]), LN_EPS) * gmlp_g + gmlp_b
    if not prompt:
        vst_ref[0] = v
    vb = v.astype(BF16)
    rows = lax.broadcasted_iota(jnp.int32, (N_HEADS_A * chunk, W_A), 0)
    cols = lax.broadcasted_iota(jnp.int32, (N_HEADS_A * chunk, W_A), 1)
    own_head = (rows // chunk) == (cols // HEAD_A)
    ws = ws_ref[...]
    bs = bs_ref[...]
    mixed = []
    for c in range(tile // chunk):
        vc = vb[c * chunk:(c + 1) * chunk]
        vbd = jnp.where(own_head, jnp.concatenate([vc] * N_HEADS_A, axis=0), jnp.zeros((), BF16))
        mixed.append(_dot(ws, vbd) + bs)
    s = mixed[0] if len(mixed) == 1 else jnp.concatenate(mixed, axis=0)
    y_a = u * s * _silu(za[:, 2 * W_A:3 * W_A])
    yab_ref[0, :, 0:W_A] = y_a.astype(BF16)

    zb = _dot(hb, w_in_ref[:, _C_B:_C_Q])
    g = zb[:, 0:W_B] * jax.nn.sigmoid(zb[:, W_B:2 * W_B])

    @pl.when(t == 0)
    def _():
        hist[0:HIST] = past_ref[0]

    hist[HIST:HIST + tile] = g
    dwk = dwk_ref[...]
    y = jnp.broadcast_to(dw_b, (tile, W_B))
    first = HIST - (CONV_W - 1)
    grouped = tile >= 8 * 8
    for r in range(8 if grouped else 0):
        rows = tile if r == 0 else tile + 8
        part = None
        for a in range(HIST // 8 + 1):
            k = 8 * a + r - first
            if 0 <= k < CONV_W:
                term = dwk[k:k + 1] * hist[8 * a:8 * a + rows]
                part = term if part is None else part + term
        y = y + (part if r == 0 else part[r:r + tile])
    for k in range(0 if grouped else CONV_W):
        y = y + dwk[k:k + 1] * hist[first + k:first + k + tile]
    new_hist = hist[tile:tile + HIST]
    hist[0:HIST] = new_hist
    conv_ref[0] = new_hist
    y = _silu(_norm_rows(y, LN_EPS) * conv_g + conv_b)
    y = _dot(y.astype(BF16), w_pw_ref[...]) + b_pw
    y_b = y * _silu(zb[:, 2 * W_B:3 * W_B])
    yab_ref[0, :, W_A:W_A + W_B] = y_b.astype(BF16)

    ct = ct_ref[...]
    st = st_ref[...]
    zq = _dot(hb, w_in_ref[:, _C_Q:_C_KV])
    cq = zq * lax.rsqrt(jnp.mean(zq * zq, axis=-1, keepdims=True) + RMS_EPS) * qn_ref[...]
    q2 = _dot(cq.astype(BF16), w_uq_ref[...])
    ctq = ct * _Q_SCALE
    stq = st * _Q_SCALE
    for hd in range(N_HEADS):
        lo = hd * HEAD_LANES
        qh = q2[:, lo:lo + HEAD_LANES] * ctq + q2[:, N_HEADS * HEAD_LANES + lo:N_HEADS * HEAD_LANES + lo + HEAD_LANES] * stq
        q_ref[0, :, lo:lo + HEAD_LANES] = qh.astype(BF16)

    zkv = _dot(hb, w_in_ref[:, _C_KV:_C_KR])
    lat = zkv * lax.rsqrt(jnp.mean(zkv * zkv, axis=-1, keepdims=True) + RMS_EPS) * kv_norm
    lat_ref[0] = lat
    zkr = _dot(hb, w_in_ref[:, _C_KR:_C_GC])
    krot = zkr[:, 0:HEAD_LANES] * ct + zkr[:, HEAD_LANES:2 * HEAD_LANES] * st
    kr_ref[0] = krot[:, QK_NOPE:QK_NOPE + QK_ROPE]

    if prompt:
        latb = lat.astype(BF16)
        kn = _dot(latb, w_uk_ref[...])
        for hd in range(N_HEADS):
            lo = hd * HEAD_LANES
            k_ref[0, :, lo:lo + HEAD_LANES] = (kn[:, lo:lo + HEAD_LANES] + krot).astype(BF16)
        vt = _dot_nt(w_uvt_ref[...], latb)
        vrow = lax.broadcasted_iota(jnp.int32, vt.shape, 0)
        vt = jnp.where((vrow % V_ROWS) == V_DIM, 1.0, vt)
        for c in range(tile // ATT_TILE):
            blk = vt[:, c * ATT_TILE:(c + 1) * ATT_TILE].reshape(N_HEADS, V_ROWS, ATT_TILE)
            vt_ref[0, :, c] = blk.astype(BF16)

    zg = _dot(hb, w_in_ref[:, _C_GC:_C_END])
    gc_ref[0] = _silu(zg).astype(BF16)


def _mixer_in(x, mod, ctab, stab, past, wts, *, tile, chunk, prompt):
    bsz, seq, _ = x.shape
    n_t = seq // tile
    kern = functools.partial(_mixer_in_kernel, tile=tile, chunk=chunk, prompt=prompt)

    def full(a):
        return pl.BlockSpec(a.shape, lambda b, t, _n=a.ndim: (0,) * _n)

    weights = [wts["w_in"], wts["vec"], wts["q_norm"], wts["ws"], wts["bs"], wts["dwk"], wts["w_pw"], wts["w_uq"]]
    if prompt:
        weights += [wts["w_uk"], wts["w_uvt"]]
    in_specs = [
        pl.BlockSpec((1, tile, D_MODEL), lambda b, t: (b, t, 0)),
        pl.BlockSpec((1, 1, 3 * D_MODEL), lambda b, t: (b, 0, 0)),
        pl.BlockSpec((tile, HEAD_LANES), lambda b, t: (t, 0)),
        pl.BlockSpec((tile, HEAD_LANES), lambda b, t: (t, 0)),
        pl.BlockSpec((1, HIST, W_B), lambda b, t: (b, 0, 0)),
    ] + [full(w) for w in weights]
    tok = lambda width: pl.BlockSpec((1, tile, width), lambda b, t: (b, t, 0))
    out_specs = [tok(N_HEADS * HEAD_LANES), tok(KV_LORA), tok(QK_ROPE), tok(W_A + W_B), tok(W_C),
                 pl.BlockSpec((1, HIST, W_B), lambda b, t: (b, 0, 0))]
    out_shape = [
        jax.ShapeDtypeStruct((bsz, seq, N_HEADS * HEAD_LANES), BF16),
        jax.ShapeDtypeStruct((bsz, seq, KV_LORA), F32),
        jax.ShapeDtypeStruct((bsz, seq, QK_ROPE), F32),
        jax.ShapeDtypeStruct((bsz, seq, W_A + W_B), BF16),
        jax.ShapeDtypeStruct((bsz, seq, W_C), BF16),
        jax.ShapeDtypeStruct((bsz, HIST, W_B), F32),
    ]
    if prompt:
        n_kt = tile // ATT_TILE
        out_specs += [tok(N_HEADS * HEAD_LANES),
                      pl.BlockSpec((1, N_HEADS, n_kt, V_ROWS, ATT_TILE), lambda b, t: (b, 0, t, 0, 0))]
        out_shape += [jax.ShapeDtypeStruct((bsz, seq, N_HEADS * HEAD_LANES), BF16),
                      jax.ShapeDtypeStruct((bsz, N_HEADS, seq // ATT_TILE, V_ROWS, ATT_TILE), BF16)]
    else:
        out_specs += [tok(W_A)]
        out_shape += [jax.ShapeDtypeStruct((bsz, seq, W_A), F32)]
    return pl.pallas_call(
        kern,
        grid=(bsz, n_t),
        in_specs=in_specs,
        out_specs=out_specs,
        out_shape=out_shape,
        scratch_shapes=[pltpu.VMEM((HIST + tile, W_B), F32)],
        compiler_params=pltpu.CompilerParams(dimension_semantics=("arbitrary", "arbitrary"),
                                             vmem_limit_bytes=VMEM_LIMIT),
        name="mixer_in_prompt" if prompt else "mixer_in_sample",
    )(x, mod, ctab, stab, past, *weights)


_S_ADDR = (0, 64)
_O_ADDR = ((128, 148), (168, 188))
_FIXED_TILES = 5


def _attn_prompt_kernel(q_ref, k_ref, vt_ref, bias_ref, o_ref, qt_sc, s_sc, acc_sc, *, n_sub, n_kt):
    qi = pl.program_id(2)

    def tile_index(t):
        return jnp.minimum(t, n_kt - 1)

    class SubTile:
        def __init__(self, sub):
            self.sp = sub % 2
            self.row0 = sub * ATT_TILE
            self.g = qi * n_sub + sub
            self.n_steady = jnp.maximum((self.g - 3) // 2, 0)
            self.j0 = _FIXED_TILES + 2 * self.n_steady - 3
            q_t = q_ref[0, self.row0:self.row0 + ATT_TILE, :].astype(F32).T
            zero = jnp.zeros((HEAD_LANES, ATT_TILE), BF16)
            for hd in range(2):
                own = slice(hd * HEAD_LANES, (hd + 1) * HEAD_LANES)
                other = slice((1 - hd) * HEAD_LANES, (2 - hd) * HEAD_LANES)
                qt_sc[self.sp, hd, own] = q_t[own].astype(BF16)
                qt_sc[self.sp, hd, other] = zero
            acc_sc[self.sp] = jnp.zeros(acc_sc.shape[1:], F32)
            neg = jnp.full((1, ATT_TILE), -jnp.inf, F32)
            one = jnp.ones((1, ATT_TILE), F32)
            self.state = ([neg, neg], [one, one], [one, one], [one, one])

        def stage_qt(self):
            for hd in range(2):
                pltpu.matmul_push_rhs(qt_sc[self.sp, hd], staging_register=0, mxu_index=hd)

        def stage_a(self, t, par, restage):
            k_tile = k_ref[0, pl.ds(pl.multiple_of(tile_index(t) * ATT_TILE, ATT_TILE), ATT_TILE), :]
            for hd in range(2):
                pltpu.matmul_acc_lhs(_S_ADDR[par], k_tile, mxu_index=hd, load_staged_rhs=0)
                if restage:
                    pltpu.matmul_push_rhs(qt_sc[self.sp, hd], staging_register=0, mxu_index=hd)

        def stage_b(self, t, par, masked, m_old=None):
            m_old = self.state[0] if m_old is None else m_old
            m_new, alpha = [], []
            if masked:
                bias = bias_ref[jnp.where(t < self.g, 0, jnp.where(t == self.g, 1, 2))]
            for hd in range(2):
                s = pltpu.matmul_pop(_S_ADDR[par], (ATT_TILE, ATT_TILE), F32, mxu_index=hd)
                if masked:
                    s = s + bias
                m_hd = jnp.maximum(m_old[hd], jnp.max(s, axis=0, keepdims=True))
                alpha.append(jnp.exp2(m_old[hd] - m_hd))
                m_new.append(m_hd)
                s_sc[self.sp, hd, par] = s
            return m_new, alpha

        def stage_c(self, t, par, m_t=None):
            m_t = self.state[0] if m_t is None else m_t
            for hd in range(2):
                p = jnp.exp2(s_sc[self.sp, hd, par] - m_t[hd]).astype(BF16)
                pltpu.matmul_push_rhs(p, staging_register=1, mxu_index=hd)
                pltpu.matmul_acc_lhs(_O_ADDR[self.sp][par], vt_ref[0, hd, tile_index(t)], mxu_index=hd,
                                     load_staged_rhs=1)

        def stage_d(self, par, alpha_t=None):
            alpha_t = self.state[3] if alpha_t is None else alpha_t
            for hd in range(2):
                o = pltpu.matmul_pop(_O_ADDR[self.sp][par], (V_ROWS, ATT_TILE), F32, mxu_index=hd)
                acc_sc[self.sp, hd] = alpha_t[hd] * acc_sc[self.sp, hd] + o

        def finish(self):
            outs = []
            for hd in range(2):
                acc = acc_sc[self.sp, hd]
                outs.append(acc[0:V_DIM] * (1.0 / acc[V_DIM:V_DIM + 1]))
            o_t = jnp.concatenate(outs, axis=0)
            o_ref[0, self.row0:self.row0 + ATT_TILE, :] = o_t.T.astype(BF16)

    def run(steps):
        for st, j, par, f in steps:
            if f.get("d", True):
                st.stage_d(par)
        new = []
        for st, j, par, f in steps:
            m, al1, al2, al3 = st.state
            if f.get("b", True):
                m_next, al0 = st.stage_b(j + 1, 1 - par, f.get("masked", False))
            else:
                m_next, al0 = m, al1
            new.append((m_next, al0, al1, al2))
        for st, j, par, f in steps:
            if f.get("a", True):
                st.stage_a(j + 2, par, f.get("restage", True))
            if f.get("qt_of") is not None:
                f["qt_of"].stage_qt()
        for st, j, par, f in steps:
            if f.get("c", True):
                st.stage_c(j, par)
        for (st, j, par, f), state in zip(steps, new):
            st.state = state

    fill = [dict(b=False, c=False, d=False), dict(c=False, d=False, masked=True),
            dict(d=False, masked=True), dict(d=False, masked=True)]

    def drain(nxt):
        return [dict(masked=True, restage=False, qt_of=nxt), dict(a=False, masked=True),
                dict(a=False, b=False), dict(a=False, b=False, c=False), dict(a=False, b=False, c=False)]

    cur = SubTile(0)
    cur.stage_qt()
    for i in range(4):
        run([(cur, i - 2, i % 2, fill[i])])
    for sub in range(n_sub):
        def steady(u, state, st=cur):
            st.state = state
            j = 2 + 2 * u
            run([(st, j, 0, {})])
            run([(st, j + 1, 1, {})])
            return st.state

        cur.state = lax.fori_loop(0, cur.n_steady, steady, cur.state)
        nxt = SubTile(sub + 1) if sub + 1 < n_sub else None
        dr = drain(nxt)
        run([(cur, cur.j0, 0, dr[0])])
        for i in range(4):
            steps = [(cur, cur.j0 + 1 + i, (i + 1) % 2, dr[i + 1])]
            if nxt is not None:
                steps.append((nxt, i - 2, i % 2, fill[i]))
            run(steps)
        cur.finish()
        cur = nxt


def _attn_prompt(q, k, vt):
    bsz, seq, _ = q.shape
    qb = min(Q_BLOCK_ROWS, seq)
    n_sub = qb // ATT_TILE
    n_kt = seq // ATT_TILE
    kern = functools.partial(_attn_prompt_kernel, n_sub=n_sub, n_kt=n_kt)
    idx = jnp.arange(ATT_TILE) // CHUNK
    diag = jnp.where(idx[:, None] <= idx[None, :], 0.0, -jnp.inf).astype(F32)
    bias = jnp.stack([jnp.zeros_like(diag), diag, jnp.full_like(diag, -jnp.inf)])
    return pl.pallas_call(
        kern,
        grid=(bsz, N_HEADS // 2, seq // qb),
        in_specs=[
            pl.BlockSpec((1, qb, 2 * HEAD_LANES), lambda b, hp, i: (b, i, hp)),
            pl.BlockSpec((1, seq, 2 * HEAD_LANES), lambda b, hp, i: (b, 0, hp)),
            pl.BlockSpec((1, 2, n_kt, V_ROWS, ATT_TILE), lambda b, hp, i: (b, hp, 0, 0, 0)),
            pl.BlockSpec((3, ATT_TILE, ATT_TILE), lambda b, hp, i: (0, 0, 0)),
        ],
        out_specs=pl.BlockSpec((1, qb, 2 * V_DIM), lambda b, hp, i: (b, i, hp)),
        out_shape=jax.ShapeDtypeStruct((bsz, seq, W_C), BF16),
        scratch_shapes=[pltpu.VMEM((2, 2, 2 * HEAD_LANES, ATT_TILE), BF16),
                        pltpu.VMEM((2, 2, 2, ATT_TILE, ATT_TILE), F32),
                        pltpu.VMEM((2, 2, V_ROWS, ATT_TILE), F32)],
        compiler_params=pltpu.CompilerParams(dimension_semantics=("arbitrary", "arbitrary", "arbitrary"),
                                             vmem_limit_bytes=VMEM_LIMIT),
        name="attn_prompt",
    )(q, k, vt, bias)


def _attn_sample_kernel(q_ref, latc_ref, krc_ref, latn_ref, krn_ref, wabs_ref, fold_ref, wuv_ref, o_ref, *, t_new):
    n_rows = N_HEADS * t_new
    q = q_ref[0]
    qrep = jnp.concatenate([q] * N_HEADS, axis=0)
    r1 = lax.broadcasted_iota(jnp.int32, qrep.shape, 0)
    c1 = lax.broadcasted_iota(jnp.int32, qrep.shape, 1)
    qrep = jnp.where((r1 // t_new) == (c1 // HEAD_LANES), qrep, jnp.zeros((), BF16))
    qlat = _dot(qrep, wabs_ref[...]).astype(BF16)
    qr = _dot(qrep, fold_ref[...])[:, 0:QK_ROPE].astype(BF16)
    latc = latc_ref[0, 0].astype(BF16)
    krc = krc_ref[0, 0].astype(BF16)
    latn = latn_ref[0].astype(BF16)
    krn = krn_ref[0].astype(BF16)
    s_c = _dot_nt(qlat, latc) + _dot_nt(qr, krc)
    s_n = _dot_nt(qlat, latn) + _dot_nt(qr, krn)
    m = jnp.maximum(jnp.max(s_c, axis=-1, keepdims=True), jnp.max(s_n, axis=-1, keepdims=True))
    p_c = jnp.exp2(s_c - m)
    p_n = jnp.exp2(s_n - m)
    den = jnp.sum(p_c, axis=-1, keepdims=True) + jnp.sum(p_n, axis=-1, keepdims=True)
    olat = (_dot(p_c.astype(BF16), latc) + _dot(p_n.astype(BF16), latn)) * (1.0 / den)
    of = _dot(olat.astype(BF16), wuv_ref[...])
    r2 = lax.broadcasted_iota(jnp.int32, of.shape, 0)
    c2 = lax.broadcasted_iota(jnp.int32, of.shape, 1)
    of = jnp.where((r2 // t_new) == (c2 // V_DIM), of, 0.0)
    o = of[0:t_new]
    for hd in range(1, N_HEADS):
        o = o + of[hd * t_new:(hd + 1) * t_new]
    o_ref[0] = o.astype(BF16)
    del n_rows


def _attn_sample(q, lat_cache, kr_cache, layer, lat_new, kr_new, wts):
    bsz, t_new, _ = q.shape
    past = lat_cache.shape[2]
    kern = functools.partial(_attn_sample_kernel, t_new=t_new)

    def full(a):
        return pl.BlockSpec(a.shape, lambda b, _n=a.ndim: (0,) * _n)

    return pl.pallas_call(
        kern,
        grid=(bsz,),
        in_specs=[
            pl.BlockSpec((1, t_new, N_HEADS * HEAD_LANES), lambda b: (b, 0, 0)),
            pl.BlockSpec((1, 1, past, KV_LORA), lambda b: (layer, b, 0, 0)),
            pl.BlockSpec((1, 1, past, QK_ROPE), lambda b: (layer, b, 0, 0)),
            pl.BlockSpec((1, t_new, KV_LORA), lambda b: (b, 0, 0)),
            pl.BlockSpec((1, t_new, QK_ROPE), lambda b: (b, 0, 0)),
            full(wts["wabs"]), full(wts["fold"]), full(wts["w_uv"]),
        ],
        out_specs=pl.BlockSpec((1, t_new, W_C), lambda b: (b, 0, 0)),
        out_shape=jax.ShapeDtypeStruct((bsz, t_new, W_C), BF16),
        compiler_params=pltpu.CompilerParams(dimension_semantics=("arbitrary",), vmem_limit_bytes=VMEM_LIMIT),
        name="attn_sample",
    )(q, lat_cache, kr_cache, lat_new, kr_new, wts["wabs"], wts["fold"], wts["w_uv"])


def _mixer_out_kernel(x_ref, mod_ref, yab_ref, o_ref, gc_ref, w_out_ref, ln_ref, out_ref, *, alpha):
    gate = mod_ref[0][:, 2 * D_MODEL:3 * D_MODEL]
    yc = (o_ref[0].astype(F32) * gc_ref[0].astype(F32)).astype(BF16)
    y = _dot(yab_ref[0], w_out_ref[0:W_A + W_B]) + _dot(yc, w_out_ref[W_A + W_B:W_A + W_B + W_C])
    r = alpha * x_ref[0] + gate * y
    ln = ln_ref[...]
    out_ref[0] = _norm_rows(r, LN_EPS) * ln[0:1] + ln[1:2]


def _mixer_out(x, mod, yab, o, gc, w_out, ln, *, tile, alpha):
    bsz, seq, _ = x.shape
    tok = lambda width: pl.BlockSpec((1, tile, width), lambda b, t: (b, t, 0))
    return pl.pallas_call(
        functools.partial(_mixer_out_kernel, alpha=alpha),
        grid=(bsz, seq // tile),
        in_specs=[tok(D_MODEL), pl.BlockSpec((1, 1, 3 * D_MODEL), lambda b, t: (b, 0, 0)),
                  tok(W_A + W_B), tok(W_C), tok(W_C),
                  pl.BlockSpec(w_out.shape, lambda b, t: (0, 0)),
                  pl.BlockSpec(ln.shape, lambda b, t: (0, 0))],
        out_specs=tok(D_MODEL),
        out_shape=jax.ShapeDtypeStruct(x.shape, F32),
        compiler_params=pltpu.CompilerParams(dimension_semantics=("arbitrary", "arbitrary"),
                                             vmem_limit_bytes=VMEM_LIMIT),
        name="mixer_out",
    )(x, mod, yab, o, gc, w_out, ln)


def _rope_tables(pos0, n):
    inv = ROPE_THETA ** (-jnp.arange(0, QK_ROPE, 2, dtype=F32) / QK_ROPE)
    ang = (pos0 + jnp.arange(n, dtype=jnp.int32)).astype(F32)[:, None] * inv[None, :]
    cos, sin = lax.optimization_barrier((jnp.cos(ang), jnp.sin(ang)))
    ones = jnp.ones((n, QK_NOPE), F32)
    z64 = jnp.zeros((n, QK_NOPE), F32)
    z32 = jnp.zeros((n, HEAD_LANES - QK_NOPE - QK_ROPE), F32)
    return jnp.concatenate([ones, cos, cos, z32], axis=1), jnp.concatenate([z64, sin, sin, z32], axis=1)


def _gmlp_weights(ws, b_s, chunk):
    idx = jnp.arange(chunk)
    mask = (idx[None, :] // CHUNK) <= (idx[:, None] // CHUNK)
    wsm = jnp.where(mask[None], ws[:, :chunk, :chunk], 0.0)
    ws_all = jnp.transpose(wsm, (1, 0, 2)).reshape(chunk, N_HEADS_A * chunk)
    bs_tab = jnp.repeat(b_s[:, :chunk].T, HEAD_A, axis=1)
    return ws_all.astype(BF16), bs_tab.astype(F32)


def _layer_weights(l, p, chunk_p, chunk_s):
    half = QK_ROPE // 2
    w_in = p["w_in"][l]
    krw = w_in[:, 2176:2208]
    x1, x2 = krw[:, :half], krw[:, half:]
    z64 = jnp.zeros((D_MODEL, QK_NOPE), F32)
    z32 = jnp.zeros((D_MODEL, HEAD_LANES - QK_NOPE - QK_ROPE), F32)
    w_in2 = jnp.concatenate([w_in[:, :2176], z64, x1, x2, z32, z64, -x2, x1, z32, w_in[:, 2208:]], axis=1)

    wq = p["mla_w_uq"][l].reshape(Q_LORA, N_HEADS, QK_NOPE + QK_ROPE)
    qn, q1, q2 = wq[..., :QK_NOPE], wq[..., QK_NOPE:QK_NOPE + half], wq[..., QK_NOPE + half:]
    zq32 = jnp.zeros((Q_LORA, N_HEADS, HEAD_LANES - QK_NOPE - QK_ROPE), F32)
    zq64 = jnp.zeros((Q_LORA, N_HEADS, QK_NOPE), F32)
    w_uq = jnp.concatenate([
        jnp.concatenate([qn, q1, q2, zq32], axis=-1).reshape(Q_LORA, -1),
        jnp.concatenate([zq64, -q2, q1, zq32], axis=-1).reshape(Q_LORA, -1)], axis=1)

    wkv = p["mla_w_ukv"][l].reshape(KV_LORA, N_HEADS, QK_NOPE + V_DIM)
    wk, wv = wkv[..., :QK_NOPE], wkv[..., QK_NOPE:]
    w_uk = jnp.concatenate([wk, jnp.zeros_like(wk)], axis=-1).reshape(KV_LORA, -1)
    wvt = jnp.transpose(wv, (1, 2, 0))
    w_uvt = jnp.concatenate([wvt, jnp.zeros((N_HEADS, V_ROWS - V_DIM, KV_LORA), F32)], axis=1).reshape(-1, KV_LORA)
    wkt = jnp.transpose(wk, (1, 2, 0))
    wabs = jnp.concatenate([wkt, jnp.zeros((N_HEADS, HEAD_LANES - QK_NOPE, KV_LORA), F32)], axis=1).reshape(-1, KV_LORA)
    eye = jnp.eye(QK_ROPE, HEAD_LANES, dtype=F32)
    fold_h = jnp.concatenate([jnp.zeros((QK_NOPE, HEAD_LANES), F32), eye,
                              jnp.zeros((HEAD_LANES - QK_NOPE - QK_ROPE, HEAD_LANES), F32)], axis=0)
    fold = jnp.tile(fold_h, (N_HEADS, 1))

    zrow = jnp.zeros((1, W_B), F32)
    vec = jnp.stack([p["gmlp_ln_g"][l], p["gmlp_ln_b"][l], p["conv_dw_b"][l], p["conv_ln_g"][l],
                     p["conv_ln_b"][l], p["conv_b_pw"][l], p["mla_kv_norm"][l], zrow[0]], axis=0)
    common = dict(
        w_in=w_in2.astype(BF16), vec=vec, q_norm=p["mla_q_norm"][l].reshape(1, Q_LORA),
        dwk=jnp.concatenate([p["conv_dw_k"][l], zrow], axis=0), w_pw=p["conv_w_pw"][l].astype(BF16),
        w_uq=w_uq.astype(BF16), w_uk=w_uk.astype(BF16), w_uvt=w_uvt.astype(BF16),
        wabs=wabs.astype(BF16), fold=fold.astype(BF16), w_uv=wv.reshape(KV_LORA, W_C).astype(BF16),
        w_out=p["w_out"][l].astype(BF16), ln=jnp.stack([p["post_ln_g"][l], p["post_ln_b"][l]], axis=0))
    ws_p, bs_p = _gmlp_weights(p["gmlp_ws"][l], p["gmlp_bs"][l], chunk_p)
    ws_s, bs_s = _gmlp_weights(p["gmlp_ws"][l], p["gmlp_bs"][l], chunk_s)
    return dict(common, ws=ws_p, bs=bs_p), dict(common, ws=ws_s, bs=bs_s)


def kernel(x_prompt, x_sample, cache_latent, cache_krope, state_conv, c_prompt, c_sample,
           w_ada, b_ada, w_in, gmlp_ln_g, gmlp_ln_b, gmlp_ws, gmlp_bs,
           conv_dw_k, conv_dw_b, conv_ln_g, conv_ln_b, conv_w_pw, conv_b_pw,
           mla_q_norm, mla_w_uq, mla_kv_norm, mla_w_ukv, w_out, post_ln_g, post_ln_b):
    p = dict(w_in=w_in, gmlp_ln_g=gmlp_ln_g, gmlp_ln_b=gmlp_ln_b, gmlp_ws=gmlp_ws, gmlp_bs=gmlp_bs,
             conv_dw_k=conv_dw_k, conv_dw_b=conv_dw_b, conv_ln_g=conv_ln_g, conv_ln_b=conv_ln_b,
             conv_w_pw=conv_w_pw, conv_b_pw=conv_b_pw, mla_q_norm=mla_q_norm, mla_w_uq=mla_w_uq,
             mla_kv_norm=mla_kv_norm, mla_w_ukv=mla_w_ukv, w_out=w_out, post_ln_g=post_ln_g, post_ln_b=post_ln_b)
    depth = w_ada.shape[0]
    bp, seq, _ = x_prompt.shape
    bs, t_new, _ = x_sample.shape
    past_len = cache_latent.shape[2]
    alpha = (2 * depth) ** 0.25
    tile_p = min(IN_TILE, seq)
    chunk_p = min(seq, MLP_CHUNK)
    chunk_s = min(t_new, MLP_CHUNK)

    n_c = bp + bs
    c_all = jnp.concatenate([c_prompt, c_sample, jnp.zeros((-n_c % 8, D_MODEL), F32)], axis=0)
    mod_all = _ada(c_all, w_ada, b_ada)

    ct_p, st_p = _rope_tables(0, seq)
    ct_s, st_s = _rope_tables(past_len, t_new)
    zero_past = jnp.zeros((bp, HIST, W_B), F32)

    xp, xs = x_prompt, x_sample
    p_conv, p_lat, p_kr, s_conv, s_lat, s_kr, s_v = [], [], [], [], [], [], []
    for l in range(depth):
        wp, wsm = _layer_weights(l, p, chunk_p, chunk_s)
        mod_p = mod_all[l, 0:bp].reshape(bp, 1, -1)
        mod_s = mod_all[l, bp:n_c].reshape(bs, 1, -1)

        q, lat, kr, yab, gc, conv, k, vt = _mixer_in(xp, mod_p, ct_p, st_p, zero_past, wp,
                                                      tile=tile_p, chunk=chunk_p, prompt=True)
        o = _attn_prompt(q, k, vt)
        xp = _mixer_out(xp, mod_p, yab, o, gc, wp["w_out"], wp["ln"], tile=tile_p, alpha=alpha)
        p_conv.append(conv[:, HIST - (CONV_W - 1):])
        p_lat.append(lat)
        p_kr.append(kr)

        past = jnp.concatenate([jnp.zeros((bs, HIST - (CONV_W - 1), W_B), F32), state_conv[l]], axis=1)
        q, lat, kr, yab, gc, conv, vst = _mixer_in(xs, mod_s, ct_s, st_s, past, wsm,
                                                   tile=t_new, chunk=chunk_s, prompt=False)
        o = _attn_sample(q, cache_latent, cache_krope, l, lat, kr, wsm)
        xs = _mixer_out(xs, mod_s, yab, o, gc, wsm["w_out"], wsm["ln"], tile=t_new, alpha=alpha)
        s_conv.append(conv[:, HIST - (CONV_W - 1):])
        s_lat.append(lat)
        s_kr.append(kr)
        s_v.append(vst)

    return (xp, xs, jnp.stack(p_conv), jnp.stack(p_lat), jnp.stack(p_kr),
            jnp.stack(s_conv), jnp.stack(s_lat), jnp.stack(s_kr), jnp.stack(s_v))
```

```python
import functools
import math

import jax
import jax.numpy as jnp
from jax import lax
from jax.experimental import pallas as pl
from jax.experimental.pallas import tpu as pltpu

F32 = jnp.float32
BF16 = jnp.bfloat16

D_MODEL = 1024
N_HEADS_A = 4
HEAD_A = 64
W_A = 256
W_B = 256
CONV_W = 31
N_HEADS = 8
QK_NOPE = 64
QK_ROPE = 32
V_DIM = 64
W_C = N_HEADS * V_DIM
Q_LORA = 384
KV_LORA = 256
CHUNK = 64
MLP_CHUNK = 128
ROPE_THETA = 10000.0
ATTN_SCALE = (QK_NOPE + QK_ROPE) ** -0.5
LN_EPS = 1e-5
RMS_EPS = 1e-6

HEAD_LANES = 128
V_ROWS = 80
HIST = 32
ATT_TILE = 256
Q_BLOCK_ROWS = 1024
IN_TILE = 1024
VMEM_LIMIT = 56 * 1024 * 1024

_C_A, _C_B, _C_Q, _C_KV, _C_KR, _C_GC, _C_END = 0, 768, 1536, 1920, 2176, 2432, 2944
_Q_SCALE = ATTN_SCALE * math.log2(math.e)


def _dot(a, b):
    return jnp.dot(a, b, preferred_element_type=F32)


def _dot_nt(a, b):
    return lax.dot_general(a, b, (((1,), (1,)), ((), ())), preferred_element_type=F32)


def _norm_rows(x, eps):
    mu = jnp.mean(x, axis=-1, keepdims=True)
    xc = x - mu
    var = jnp.mean(xc * xc, axis=-1, keepdims=True)
    return xc * lax.rsqrt(var + eps)


def _gelu_tanh(x):
    return x * (0.5 * (1.0 + jnp.tanh(math.sqrt(2.0 / math.pi) * (x + 0.044715 * (x * x * x)))))


def _silu(x):
    return x * jax.nn.sigmoid(x)


def _ada_kernel(c_ref, w_ref, b_ref, o_ref):
    c = c_ref[...]
    a = _silu(c)
    w = w_ref[0]
    a_hi = a.astype(BF16)
    a_lo = (a - a_hi.astype(F32)).astype(BF16)
    w_hi = w.astype(BF16)
    w_lo = (w - w_hi.astype(F32)).astype(BF16)
    o_ref[0] = _dot(a_hi, w_hi) + (_dot(a_hi, w_lo) + _dot(a_lo, w_hi)) + b_ref[0]


def _ada(c_all, w_ada, b_ada):
    depth = w_ada.shape[0]
    rows = c_all.shape[0]
    n_col = w_ada.shape[2] // D_MODEL
    return pl.pallas_call(
        _ada_kernel,
        grid=(depth, n_col),
        in_specs=[
            pl.BlockSpec((rows, D_MODEL), lambda l, j: (0, 0)),
            pl.BlockSpec((1, D_MODEL, D_MODEL), lambda l, j: (l, 0, j)),
            pl.BlockSpec((1, 1, D_MODEL), lambda l, j: (l, 0, j)),
        ],
        out_specs=pl.BlockSpec((1, rows, D_MODEL), lambda l, j: (l, 0, j)),
        out_shape=jax.ShapeDtypeStruct((depth, rows, w_ada.shape[2]), F32),
        compiler_params=pltpu.CompilerParams(vmem_limit_bytes=VMEM_LIMIT),
        name="ada_mod",
    )(c_all, w_ada, b_ada.reshape(depth, 1, -1))


def _mixer_in_kernel(x_ref, mod_ref, ct_ref, st_ref, past_ref, w_in_ref, vec_ref, qn_ref, ws_ref, bs_ref,
                     dwk_ref, w_pw_ref, w_uq_ref, *rest, tile, chunk, prompt):
    if prompt:
        (w_uk_ref, w_uvt_ref, q_ref, lat_ref, kr_ref, yab_ref, gc_ref, conv_ref, k_ref, vt_ref, hist) = rest
    else:
        (q_ref, lat_ref, kr_ref, yab_ref, gc_ref, conv_ref, vst_ref, hist) = rest
    t = pl.program_id(1)

    mod = mod_ref[0]
    shift = mod[:, 0:D_MODEL]
    scale = mod[:, D_MODEL:2 * D_MODEL]
    h = _norm_rows(x_ref[0], LN_EPS) * (1.0 + scale) + shift
    hb = h.astype(BF16)

    vec = vec_ref[...]
    gmlp_g, gmlp_b = vec[0:1], vec[1:2]
    dw_b, conv_g, conv_b, b_pw, kv_norm = vec[2:3], vec[3:4], vec[4:5], vec[5:6], vec[6:7]

    za = _dot(hb, w_in_ref[:, _C_A:_C_B])
    u = _gelu_tanh(za[:, 0:W_A])
    v = _norm_rows(_gelu_tanh(za[:, W_A:2 * W_A]), LN_EPS) * gmlp_g + gmlp_b
    if not prompt:
        vst_ref[0] = v
    vb = v.astype(BF16)
    rows = lax.broadcasted_iota(jnp.int32, (N_HEADS_A * chunk, W_A), 0)
    cols = lax.broadcasted_iota(jnp.int32, (N_HEADS_A * chunk, W_A), 1)
    own_head = (rows // chunk) == (cols // HEAD_A)
    ws = ws_ref[...]
    bs = bs_ref[...]
    mixed = []
    for c in range(tile // chunk):
        vc = vb[c * chunk:(c + 1) * chunk]
        vbd = jnp.where(own_head, jnp.concatenate([vc] * N_HEADS_A, axis=0), jnp.zeros((), BF16))
        mixed.append(_dot(ws, vbd) + bs)
    s = mixed[0] if len(mixed) == 1 else jnp.concatenate(mixed, axis=0)
    y_a = u * s * _silu(za[:, 2 * W_A:3 * W_A])
    yab_ref[0, :, 0:W_A] = y_a.astype(BF16)

    zb = _dot(hb, w_in_ref[:, _C_B:_C_Q])
    g = zb[:, 0:W_B] * jax.nn.sigmoid(zb[:, W_B:2 * W_B])

    @pl.when(t == 0)
    def _():
        hist[0:HIST] = past_ref[0]

    hist[HIST:HIST + tile] = g
    dwk = dwk_ref[...]
    y = jnp.broadcast_to(dw_b, (tile, W_B))
    first = HIST - (CONV_W - 1)
    grouped = tile >= 8 * 8
    for r in range(8 if grouped else 0):
        rows = tile if r == 0 else tile + 8
        part = None
        for a in range(HIST // 8 + 1):
            k = 8 * a + r - first
            if 0 <= k < CONV_W:
                term = dwk[k:k + 1] * hist[8 * a:8 * a + rows]
                part = term if part is None else part + term
        y = y + (part if r == 0 else part[r:r + tile])
    for k in range(0 if grouped else CONV_W):
        y = y + dwk[k:k + 1] * hist[first + k:first + k + tile]
    new_hist = hist[tile:tile + HIST]
    hist[0:HIST] = new_hist
    conv_ref[0] = new_hist
    y = _silu(_norm_rows(y, LN_EPS) * conv_g + conv_b)
    y = _dot(y.astype(BF16), w_pw_ref[...]) + b_pw
    y_b = y * _silu(zb[:, 2 * W_B:3 * W_B])
    yab_ref[0, :, W_A:W_A + W_B] = y_b.astype(BF16)

    ct = ct_ref[...]
    st = st_ref[...]
    zq = _dot(hb, w_in_ref[:, _C_Q:_C_KV])
    cq = zq * lax.rsqrt(jnp.mean(zq * zq, axis=-1, keepdims=True) + RMS_EPS) * qn_ref[...]
    q2 = _dot(cq.astype(BF16), w_uq_ref[...])
    ctq = ct * _Q_SCALE
    stq = st * _Q_SCALE
    for hd in range(N_HEADS):
        lo = hd * HEAD_LANES
        qh = q2[:, lo:lo + HEAD_LANES] * ctq + q2[:, N_HEADS * HEAD_LANES + lo:N_HEADS * HEAD_LANES + lo + HEAD_LANES] * stq
        q_ref[0, :, lo:lo + HEAD_LANES] = qh.astype(BF16)

    zkv = _dot(hb, w_in_ref[:, _C_KV:_C_KR])
    lat = zkv * lax.rsqrt(jnp.mean(zkv * zkv, axis=-1, keepdims=True) + RMS_EPS) * kv_norm
    lat_ref[0] = lat
    zkr = _dot(hb, w_in_ref[:, _C_KR:_C_GC])
    krot = zkr[:, 0:HEAD_LANES] * ct + zkr[:, HEAD_LANES:2 * HEAD_LANES] * st
    kr_ref[0] = krot[:, QK_NOPE:QK_NOPE + QK_ROPE]

    if prompt:
        latb = lat.astype(BF16)
        kn = _dot(latb, w_uk_ref[...])
        for hd in range(N_HEADS):
            lo = hd * HEAD_LANES
            k_ref[0, :, lo:lo + HEAD_LANES] = (kn[:, lo:lo + HEAD_LANES] + krot).astype(BF16)
        vt = _dot_nt(w_uvt_ref[...], latb)
        vrow = lax.broadcasted_iota(jnp.int32, vt.shape, 0)
        vt = jnp.where((vrow % V_ROWS) == V_DIM, 1.0, vt)
        for c in range(tile // ATT_TILE):
            blk = vt[:, c * ATT_TILE:(c + 1) * ATT_TILE].reshape(N_HEADS, V_ROWS, ATT_TILE)
            vt_ref[0, :, c] = blk.astype(BF16)

    zg = _dot(hb, w_in_ref[:, _C_GC:_C_END])
    gc_ref[0] = _silu(zg).astype(BF16)


def _mixer_in(x, mod, ctab, stab, past, wts, *, tile, chunk, prompt):
    bsz, seq, _ = x.shape
    n_t = seq // tile
    kern = functools.partial(_mixer_in_kernel, tile=tile, chunk=chunk, prompt=prompt)

    def full(a):
        return pl.BlockSpec(a.shape, lambda b, t, _n=a.ndim: (0,) * _n)

    weights = [wts["w_in"], wts["vec"], wts["q_norm"], wts["ws"], wts["bs"], wts["dwk"], wts["w_pw"], wts["w_uq"]]
    if prompt:
        weights += [wts["w_uk"], wts["w_uvt"]]
    in_specs = [
        pl.BlockSpec((1, tile, D_MODEL), lambda b, t: (b, t, 0)),
        pl.BlockSpec((1, 1, 3 * D_MODEL), lambda b, t: (b, 0, 0)),
        pl.BlockSpec((tile, HEAD_LANES), lambda b, t: (t, 0)),
        pl.BlockSpec((tile, HEAD_LANES), lambda b, t: (t, 0)),
        pl.BlockSpec((1, HIST, W_B), lambda b, t: (b, 0, 0)),
    ] + [full(w) for w in weights]
    tok = lambda width: pl.BlockSpec((1, tile, width), lambda b, t: (b, t, 0))
    out_specs = [tok(N_HEADS * HEAD_LANES), tok(KV_LORA), tok(QK_ROPE), tok(W_A + W_B), tok(W_C),
                 pl.BlockSpec((1, HIST, W_B), lambda b, t: (b, 0, 0))]
    out_shape = [
        jax.ShapeDtypeStruct((bsz, seq, N_HEADS * HEAD_LANES), BF16),
        jax.ShapeDtypeStruct((bsz, seq, KV_LORA), F32),
        jax.ShapeDtypeStruct((bsz, seq, QK_ROPE), F32),
        jax.ShapeDtypeStruct((bsz, seq, W_A + W_B), BF16),
        jax.ShapeDtypeStruct((bsz, seq, W_C), BF16),
        jax.ShapeDtypeStruct((bsz, HIST, W_B), F32),
    ]
    if prompt:
        n_kt = tile // ATT_TILE
        out_specs += [tok(N_HEADS * HEAD_LANES),
                      pl.BlockSpec((1, N_HEADS, n_kt, V_ROWS, ATT_TILE), lambda b, t: (b, 0, t, 0, 0))]
        out_shape += [jax.ShapeDtypeStruct((bsz, seq, N_HEADS * HEAD_LANES), BF16),
                      jax.ShapeDtypeStruct((bsz, N_HEADS, seq // ATT_TILE, V_ROWS, ATT_TILE), BF16)]
    else:
        out_specs += [tok(W_A)]
        out_shape += [jax.ShapeDtypeStruct((bsz, seq, W_A), F32)]
    return pl.pallas_call(
        kern,
        grid=(bsz, n_t),
        in_specs=in_specs,
        out_specs=out_specs,
        out_shape=out_shape,
        scratch_shapes=[pltpu.VMEM((HIST + tile, W_B), F32)],
        compiler_params=pltpu.CompilerParams(dimension_semantics=("arbitrary", "arbitrary"),
                                             vmem_limit_bytes=VMEM_LIMIT),
        name="mixer_in_prompt" if prompt else "mixer_in_sample",
    )(x, mod, ctab, stab, past, *weights)


_S_ADDR = (0, 64)
_O_ADDR = ((128, 148), (168, 188))
_FIXED_TILES = 5


def _attn_prompt_kernel(q_ref, k_ref, vt_ref, bias_ref, o_ref, qt_sc, s_sc, acc_sc, *, n_sub, n_kt):
    qi = pl.program_id(2)

    def tile_index(t):
        return jnp.minimum(t, n_kt - 1)

    class SubTile:
        def __init__(self, sub):
            self.sp = sub % 2
            self.row0 = sub * ATT_TILE
            self.g = qi * n_sub + sub
            self.n_steady = jnp.maximum((self.g - 3) // 2, 0)
            self.j0 = _FIXED_TILES + 2 * self.n_steady - 3
            q_t = q_ref[0, self.row0:self.row0 + ATT_TILE, :].astype(F32).T
            zero = jnp.zeros((HEAD_LANES, ATT_TILE), BF16)
            for hd in range(2):
                own = slice(hd * HEAD_LANES, (hd + 1) * HEAD_LANES)
                other = slice((1 - hd) * HEAD_LANES, (2 - hd) * HEAD_LANES)
                qt_sc[self.sp, hd, own] = q_t[own].astype(BF16)
                qt_sc[self.sp, hd, other] = zero
            acc_sc[self.sp] = jnp.zeros(acc_sc.shape[1:], F32)
            neg = jnp.full((1, ATT_TILE), -jnp.inf, F32)
            one = jnp.ones((1, ATT_TILE), F32)
            self.state = ([neg, neg], [one, one], [one, one], [one, one])

        def stage_qt(self):
            for hd in range(2):
                pltpu.matmul_push_rhs(qt_sc[self.sp, hd], staging_register=0, mxu_index=hd)

        def stage_a(self, t, par, restage):
            k_tile = k_ref[0, pl.ds(pl.multiple_of(tile_index(t) * ATT_TILE, ATT_TILE), ATT_TILE), :]
            for hd in range(2):
                pltpu.matmul_acc_lhs(_S_ADDR[par], k_tile, mxu_index=hd, load_staged_rhs=0)
                if restage:
                    pltpu.matmul_push_rhs(qt_sc[self.sp, hd], staging_register=0, mxu_index=hd)

        def stage_b(self, t, par, masked, m_old=None):
            m_old = self.state[0] if m_old is None else m_old
            m_new, alpha = [], []
            if masked:
                bias = bias_ref[jnp.where(t < self.g, 0, jnp.where(t == self.g, 1, 2))]
            for hd in range(2):
                s = pltpu.matmul_pop(_S_ADDR[par], (ATT_TILE, ATT_TILE), F32, mxu_index=hd)
                if masked:
                    s = s + bias
                m_hd = jnp.maximum(m_old[hd], jnp.max(s, axis=0, keepdims=True))
                alpha.append(jnp.exp2(m_old[hd] - m_hd))
                m_new.append(m_hd)
                s_sc[self.sp, hd, par] = s
            return m_new, alpha

        def stage_c(self, t, par, m_t=None):
            m_t = self.state[0] if m_t is None else m_t
            for hd in range(2):
                p = jnp.exp2(s_sc[self.sp, hd, par] - m_t[hd]).astype(BF16)
                pltpu.matmul_push_rhs(p, staging_register=1, mxu_index=hd)
                pltpu.matmul_acc_lhs(_O_ADDR[self.sp][par], vt_ref[0, hd, tile_index(t)], mxu_index=hd,
                                     load_staged_rhs=1)

        def stage_d(self, par, alpha_t=None):
            alpha_t = self.state[3] if alpha_t is None else alpha_t
            for hd in range(2):
                o = pltpu.matmul_pop(_O_ADDR[self.sp][par], (V_ROWS, ATT_TILE), F32, mxu_index=hd)
                acc_sc[self.sp, hd] = alpha_t[hd] * acc_sc[self.sp, hd] + o

        def finish(self):
            outs = []
            for hd in range(2):
                acc = acc_sc[self.sp, hd]
                outs.append(acc[0:V_DIM] * (1.0 / acc[V_DIM:V_DIM + 1]))
            o_t = jnp.concatenate(outs, axis=0)
            o_ref[0, self.row0:self.row0 + ATT_TILE, :] = o_t.T.astype(BF16)

    def run(steps):
        for st, j, par, f in steps:
            if f.get("d", True):
                st.stage_d(par)
        new = []
        for st, j, par, f in steps:
            m, al1, al2, al3 = st.state
            if f.get("b", True):
                m_next, al0 = st.stage_b(j + 1, 1 - par, f.get("masked", False))
            else:
                m_next, al0 = m, al1
            new.append((m_next, al0, al1, al2))
        for st, j, par, f in steps:
            if f.get("a", True):
                st.stage_a(j + 2, par, f.get("restage", True))
            if f.get("qt_of") is not None:
                f["qt_of"].stage_qt()
        for st, j, par, f in steps:
            if f.get("c", True):
                st.stage_c(j, par)
        for (st, j, par, f), state in zip(steps, new):
            st.state = state

    fill = [dict(b=False, c=False, d=False), dict(c=False, d=False, masked=True),
            dict(d=False, masked=True), dict(d=False, masked=True)]

    def drain(nxt):
        return [dict(masked=True, restage=False, qt_of=nxt), dict(a=False, masked=True),
                dict(a=False, b=False), dict(a=False, b=False, c=False), dict(a=False, b=False, c=False)]

    cur = SubTile(0)
    cur.stage_qt()
    for i in range(4):
        run([(cur, i - 2, i % 2, fill[i])])
    for sub in range(n_sub):
        def steady(state, j, n_iter, st=cur):
            st.state = state
            for i in range(n_iter):
                run([(st, j + i, i % 2, {})])
            return st.state

        quads = cur.n_steady // 2
        cur.state = lax.fori_loop(0, quads, lambda u, s: steady(s, 2 + 4 * u, 4), cur.state)
        cur.state = lax.cond(cur.n_steady % 2 == 1, lambda s: steady(s, 2 + 4 * quads, 2), lambda s: s,
                             cur.state)
        nxt = SubTile(sub + 1) if sub + 1 < n_sub else None
        dr = drain(nxt)
        run([(cur, cur.j0, 0, dr[0])])
        for i in range(4):
            steps = [(cur, cur.j0 + 1 + i, (i + 1) % 2, dr[i + 1])]
            if nxt is not None:
                steps.append((nxt, i - 2, i % 2, fill[i]))
            run(steps)
        cur.finish()
        cur = nxt


def _attn_prompt(q, k, vt):
    bsz, seq, _ = q.shape
    qb = min(Q_BLOCK_ROWS, seq)
    n_sub = qb // ATT_TILE
    n_kt = seq // ATT_TILE
    kern = functools.partial(_attn_prompt_kernel, n_sub=n_sub, n_kt=n_kt)
    idx = jnp.arange(ATT_TILE) // CHUNK
    diag = jnp.where(idx[:, None] <= idx[None, :], 0.0, -jnp.inf).astype(F32)
    bias = jnp.stack([jnp.zeros_like(diag), diag, jnp.full_like(diag, -jnp.inf)])
    return pl.pallas_call(
        kern,
        grid=(bsz, N_HEADS // 2, seq // qb),
        in_specs=[
            pl.BlockSpec((1, qb, 2 * HEAD_LANES), lambda b, hp, i: (b, i, hp)),
            pl.BlockSpec((1, seq, 2 * HEAD_LANES), lambda b, hp, i: (b, 0, hp)),
            pl.BlockSpec((1, 2, n_kt, V_ROWS, ATT_TILE), lambda b, hp, i: (b, hp, 0, 0, 0)),
            pl.BlockSpec((3, ATT_TILE, ATT_TILE), lambda b, hp, i: (0, 0, 0)),
        ],
        out_specs=pl.BlockSpec((1, qb, 2 * V_DIM), lambda b, hp, i: (b, i, hp)),
        out_shape=jax.ShapeDtypeStruct((bsz, seq, W_C), BF16),
        scratch_shapes=[pltpu.VMEM((2, 2, 2 * HEAD_LANES, ATT_TILE), BF16),
                        pltpu.VMEM((2, 2, 2, ATT_TILE, ATT_TILE), F32),
                        pltpu.VMEM((2, 2, V_ROWS, ATT_TILE), F32)],
        compiler_params=pltpu.CompilerParams(dimension_semantics=("arbitrary", "arbitrary", "arbitrary"),
                                             vmem_limit_bytes=VMEM_LIMIT),
        name="attn_prompt",
    )(q, k, vt, bias)


def _attn_sample_kernel(q_ref, latc_ref, krc_ref, latn_ref, krn_ref, wabs_ref, fold_ref, wuv_ref, o_ref, *, t_new):
    n_rows = N_HEADS * t_new
    q = q_ref[0]
    qrep = jnp.concatenate([q] * N_HEADS, axis=0)
    r1 = lax.broadcasted_iota(jnp.int32, qrep.shape, 0)
    c1 = lax.broadcasted_iota(jnp.int32, qrep.shape, 1)
    qrep = jnp.where((r1 // t_new) == (c1 // HEAD_LANES), qrep, jnp.zeros((), BF16))
    qlat = _dot(qrep, wabs_ref[...]).astype(BF16)
    qr = _dot(qrep, fold_ref[...])[:, 0:QK_ROPE].astype(BF16)
    latc = latc_ref[0, 0].astype(BF16)
    krc = krc_ref[0, 0].astype(BF16)
    latn = latn_ref[0].astype(BF16)
    krn = krn_ref[0].astype(BF16)
    s_c = _dot_nt(qlat, latc) + _dot_nt(qr, krc)
    s_n = _dot_nt(qlat, latn) + _dot_nt(qr, krn)
    m = jnp.maximum(jnp.max(s_c, axis=-1, keepdims=True), jnp.max(s_n, axis=-1, keepdims=True))
    p_c = jnp.exp2(s_c - m)
    p_n = jnp.exp2(s_n - m)
    den = jnp.sum(p_c, axis=-1, keepdims=True) + jnp.sum(p_n, axis=-1, keepdims=True)
    olat = (_dot(p_c.astype(BF16), latc) + _dot(p_n.astype(BF16), latn)) * (1.0 / den)
    of = _dot(olat.astype(BF16), wuv_ref[...])
    r2 = lax.broadcasted_iota(jnp.int32, of.shape, 0)
    c2 = lax.broadcasted_iota(jnp.int32, of.shape, 1)
    of = jnp.where((r2 // t_new) == (c2 // V_DIM), of, 0.0)
    o = of[0:t_new]
    for hd in range(1, N_HEADS):
        o = o + of[hd * t_new:(hd + 1) * t_new]
    o_ref[0] = o.astype(BF16)
    del n_rows


def _attn_sample(q, lat_cache, kr_cache, layer, lat_new, kr_new, wts):
    bsz, t_new, _ = q.shape
    past = lat_cache.shape[2]
    kern = functools.partial(_attn_sample_kernel, t_new=t_new)

    def full(a):
        return pl.BlockSpec(a.shape, lambda b, _n=a.ndim: (0,) * _n)

    return pl.pallas_call(
        kern,
        grid=(bsz,),
        in_specs=[
            pl.BlockSpec((1, t_new, N_HEADS * HEAD_LANES), lambda b: (b, 0, 0)),
            pl.BlockSpec((1, 1, past, KV_LORA), lambda b: (layer, b, 0, 0)),
            pl.BlockSpec((1, 1, past, QK_ROPE), lambda b: (layer, b, 0, 0)),
            pl.BlockSpec((1, t_new, KV_LORA), lambda b: (b, 0, 0)),
            pl.BlockSpec((1, t_new, QK_ROPE), lambda b: (b, 0, 0)),
            full(wts["wabs"]), full(wts["fold"]), full(wts["w_uv"]),
        ],
        out_specs=pl.BlockSpec((1, t_new, W_C), lambda b: (b, 0, 0)),
        out_shape=jax.ShapeDtypeStruct((bsz, t_new, W_C), BF16),
        compiler_params=pltpu.CompilerParams(dimension_semantics=("arbitrary",), vmem_limit_bytes=VMEM_LIMIT),
        name="attn_sample",
    )(q, lat_cache, kr_cache, lat_new, kr_new, wts["wabs"], wts["fold"], wts["w_uv"])


def _mixer_out_kernel(x_ref, mod_ref, yab_ref, o_ref, gc_ref, w_out_ref, ln_ref, out_ref, *, alpha):
    gate = mod_ref[0][:, 2 * D_MODEL:3 * D_MODEL]
    yc = (o_ref[0].astype(F32) * gc_ref[0].astype(F32)).astype(BF16)
    y = _dot(yab_ref[0], w_out_ref[0:W_A + W_B]) + _dot(yc, w_out_ref[W_A + W_B:W_A + W_B + W_C])
    r = alpha * x_ref[0] + gate * y
    ln = ln_ref[...]
    out_ref[0] = _norm_rows(r, LN_EPS) * ln[0:1] + ln[1:2]


def _mixer_out(x, mod, yab, o, gc, w_out, ln, *, tile, alpha):
    bsz, seq, _ = x.shape
    tok = lambda width: pl.BlockSpec((1, tile, width), lambda b, t: (b, t, 0))
    return pl.pallas_call(
        functools.partial(_mixer_out_kernel, alpha=alpha),
        grid=(bsz, seq // tile),
        in_specs=[tok(D_MODEL), pl.BlockSpec((1, 1, 3 * D_MODEL), lambda b, t: (b, 0, 0)),
                  tok(W_A + W_B), tok(W_C), tok(W_C),
                  pl.BlockSpec(w_out.shape, lambda b, t: (0, 0)),
                  pl.BlockSpec(ln.shape, lambda b, t: (0, 0))],
        out_specs=tok(D_MODEL),
        out_shape=jax.ShapeDtypeStruct(x.shape, F32),
        compiler_params=pltpu.CompilerParams(dimension_semantics=("arbitrary", "arbitrary"),
                                             vmem_limit_bytes=VMEM_LIMIT),
        name="mixer_out",
    )(x, mod, yab, o, gc, w_out, ln)


def _rope_tables(pos0, n):
    inv = ROPE_THETA ** (-jnp.arange(0, QK_ROPE, 2, dtype=F32) / QK_ROPE)
    ang = (pos0 + jnp.arange(n, dtype=jnp.int32)).astype(F32)[:, None] * inv[None, :]
    cos, sin = lax.optimization_barrier((jnp.cos(ang), jnp.sin(ang)))
    ones = jnp.ones((n, QK_NOPE), F32)
    z64 = jnp.zeros((n, QK_NOPE), F32)
    z32 = jnp.zeros((n, HEAD_LANES - QK_NOPE - QK_ROPE), F32)
    return jnp.concatenate([ones, cos, cos, z32], axis=1), jnp.concatenate([z64, sin, sin, z32], axis=1)


def _gmlp_weights(ws, b_s, chunk):
    idx = jnp.arange(chunk)
    mask = (idx[None, :] // CHUNK) <= (idx[:, None] // CHUNK)
    wsm = jnp.where(mask[None], ws[:, :chunk, :chunk], 0.0)
    ws_all = jnp.transpose(wsm, (1, 0, 2)).reshape(chunk, N_HEADS_A * chunk)
    bs_tab = jnp.repeat(b_s[:, :chunk].T, HEAD_A, axis=1)
    return ws_all.astype(BF16), bs_tab.astype(F32)


def _layer_weights(l, p, chunk_p, chunk_s):
    half = QK_ROPE // 2
    w_in = p["w_in"][l]
    krw = w_in[:, 2176:2208]
    x1, x2 = krw[:, :half], krw[:, half:]
    z64 = jnp.zeros((D_MODEL, QK_NOPE), F32)
    z32 = jnp.zeros((D_MODEL, HEAD_LANES - QK_NOPE - QK_ROPE), F32)
    w_in2 = jnp.concatenate([w_in[:, :2176], z64, x1, x2, z32, z64, -x2, x1, z32, w_in[:, 2208:]], axis=1)

    wq = p["mla_w_uq"][l].reshape(Q_LORA, N_HEADS, QK_NOPE + QK_ROPE)
    qn, q1, q2 = wq[..., :QK_NOPE], wq[..., QK_NOPE:QK_NOPE + half], wq[..., QK_NOPE + half:]
    zq32 = jnp.zeros((Q_LORA, N_HEADS, HEAD_LANES - QK_NOPE - QK_ROPE), F32)
    zq64 = jnp.zeros((Q_LORA, N_HEADS, QK_NOPE), F32)
    w_uq = jnp.concatenate([
        jnp.concatenate([qn, q1, q2, zq32], axis=-1).reshape(Q_LORA, -1),
        jnp.concatenate([zq64, -q2, q1, zq32], axis=-1).reshape(Q_LORA, -1)], axis=1)

    wkv = p["mla_w_ukv"][l].reshape(KV_LORA, N_HEADS, QK_NOPE + V_DIM)
    wk, wv = wkv[..., :QK_NOPE], wkv[..., QK_NOPE:]
    w_uk = jnp.concatenate([wk, jnp.zeros_like(wk)], axis=-1).reshape(KV_LORA, -1)
    wvt = jnp.transpose(wv, (1, 2, 0))
    w_uvt = jnp.concatenate([wvt, jnp.zeros((N_HEADS, V_ROWS - V_DIM, KV_LORA), F32)], axis=1).reshape(-1, KV_LORA)
    wkt = jnp.transpose(wk, (1, 2, 0))
    wabs = jnp.concatenate([wkt, jnp.zeros((N_HEADS, HEAD_LANES - QK_NOPE, KV_LORA), F32)], axis=1).reshape(-1, KV_LORA)
    eye = jnp.eye(QK_ROPE, HEAD_LANES, dtype=F32)
    fold_h = jnp.concatenate([jnp.zeros((QK_NOPE, HEAD_LANES), F32), eye,
                              jnp.zeros((HEAD_LANES - QK_NOPE - QK_ROPE, HEAD_LANES), F32)], axis=0)
    fold = jnp.tile(fold_h, (N_HEADS, 1))

    zrow = jnp.zeros((1, W_B), F32)
    vec = jnp.stack([p["gmlp_ln_g"][l], p["gmlp_ln_b"][l], p["conv_dw_b"][l], p["conv_ln_g"][l],
                     p["conv_ln_b"][l], p["conv_b_pw"][l], p["mla_kv_norm"][l], zrow[0]], axis=0)
    common = dict(
        w_in=w_in2.astype(BF16), vec=vec, q_norm=p["mla_q_norm"][l].reshape(1, Q_LORA),
        dwk=jnp.concatenate([p["conv_dw_k"][l], zrow], axis=0), w_pw=p["conv_w_pw"][l].astype(BF16),
        w_uq=w_uq.astype(BF16), w_uk=w_uk.astype(BF16), w_uvt=w_uvt.astype(BF16),
        wabs=wabs.astype(BF16), fold=fold.astype(BF16), w_uv=wv.reshape(KV_LORA, W_C).astype(BF16),
        w_out=p["w_out"][l].astype(BF16), ln=jnp.stack([p["post_ln_g"][l], p["post_ln_b"][l]], axis=0))
    ws_p, bs_p = _gmlp_weights(p["gmlp_ws"][l], p["gmlp_bs"][l], chunk_p)
    ws_s, bs_s = _gmlp_weights(p["gmlp_ws"][l], p["gmlp_bs"][l], chunk_s)
    return dict(common, ws=ws_p, bs=bs_p), dict(common, ws=ws_s, bs=bs_s)


def kernel(x_prompt, x_sample, cache_latent, cache_krope, state_conv, c_prompt, c_sample,
           w_ada, b_ada, w_in, gmlp_ln_g, gmlp_ln_b, gmlp_ws, gmlp_bs,
           conv_dw_k, conv_dw_b, conv_ln_g, conv_ln_b, conv_w_pw, conv_b_pw,
           mla_q_norm, mla_w_uq, mla_kv_norm, mla_w_ukv, w_out, post_ln_g, post_ln_b):
    p = dict(w_in=w_in, gmlp_ln_g=gmlp_ln_g, gmlp_ln_b=gmlp_ln_b, gmlp_ws=gmlp_ws, gmlp_bs=gmlp_bs,
             conv_dw_k=conv_dw_k, conv_dw_b=conv_dw_b, conv_ln_g=conv_ln_g, conv_ln_b=conv_ln_b,
             conv_w_pw=conv_w_pw, conv_b_pw=conv_b_pw, mla_q_norm=mla_q_norm, mla_w_uq=mla_w_uq,
             mla_kv_norm=mla_kv_norm, mla_w_ukv=mla_w_ukv, w_out=w_out, post_ln_g=post_ln_g, post_ln_b=post_ln_b)
    depth = w_ada.shape[0]
    bp, seq, _ = x_prompt.shape
    bs, t_new, _ = x_sample.shape
    past_len = cache_latent.shape[2]
    alpha = (2 * depth) ** 0.25
    tile_p = min(IN_TILE, seq)
    chunk_p = min(seq, MLP_CHUNK)
    chunk_s = min(t_new, MLP_CHUNK)

    n_c = bp + bs
    c_all = jnp.concatenate([c_prompt, c_sample, jnp.zeros((-n_c % 8, D_MODEL), F32)], axis=0)
    mod_all = _ada(c_all, w_ada, b_ada)

    ct_p, st_p = _rope_tables(0, seq)
    ct_s, st_s = _rope_tables(past_len, t_new)
    zero_past = jnp.zeros((bp, HIST, W_B), F32)

    xp, xs = x_prompt, x_sample
    p_conv, p_lat, p_kr, s_conv, s_lat, s_kr, s_v = [], [], [], [], [], [], []
    for l in range(depth):
        wp, wsm = _layer_weights(l, p, chunk_p, chunk_s)
        mod_p = mod_all[l, 0:bp].reshape(bp, 1, -1)
        mod_s = mod_all[l, bp:n_c].reshape(bs, 1, -1)

        q, lat, kr, yab, gc, conv, k, vt = _mixer_in(xp, mod_p, ct_p, st_p, zero_past, wp,
                                                      tile=tile_p, chunk=chunk_p, prompt=True)
        o = _attn_prompt(q, k, vt)
        xp = _mixer_out(xp, mod_p, yab, o, gc, wp["w_out"], wp["ln"], tile=tile_p, alpha=alpha)
        p_conv.append(conv[:, HIST - (CONV_W - 1):])
        p_lat.append(lat)
        p_kr.append(kr)

        past = jnp.concatenate([jnp.zeros((bs, HIST - (CONV_W - 1), W_B), F32), state_conv[l]], axis=1)
        q, lat, kr, yab, gc, conv, vst = _mixer_in(xs, mod_s, ct_s, st_s, past, wsm,
                                                   tile=t_new, chunk=chunk_s, prompt=False)
        o = _attn_sample(q, cache_latent, cache_krope, l, lat, kr, wsm)
        xs = _mixer_out(xs, mod_s, yab, o, gc, wsm["w_out"], wsm["ln"], tile=t_new, alpha=alpha)
        s_conv.append(conv[:, HIST - (CONV_W - 1):])
        s_lat.append(lat)
        s_kr.append(kr)
        s_v.append(vst)

    return (xp, xs, jnp.stack(p_conv), jnp.stack(p_lat), jnp.stack(p_kr),
            jnp.stack(s_conv), jnp.stack(s_lat), jnp.stack(s_kr), jnp.stack(s_v))
```

```python
import functools
import math

import jax
import jax.numpy as jnp
from jax import lax
from jax.experimental import pallas as pl
from jax.experimental.pallas import tpu as pltpu

F32 = jnp.float32
BF16 = jnp.bfloat16

D_MODEL = 1024
N_HEADS_A = 4
HEAD_A = 64
W_A = 256
W_B = 256
CONV_W = 31
N_HEADS = 8
QK_NOPE = 64
QK_ROPE = 32
V_DIM = 64
W_C = N_HEADS * V_DIM
Q_LORA = 384
KV_LORA = 256
CHUNK = 64
MLP_CHUNK = 128
ROPE_THETA = 10000.0
ATTN_SCALE = (QK_NOPE + QK_ROPE) ** -0.5
LN_EPS = 1e-5
RMS_EPS = 1e-6

HEAD_LANES = 128
V_ROWS = 80
HIST = 32
ATT_TILE = 256
Q_BLOCK_ROWS = 2048
IN_TILE = 1024
VMEM_LIMIT = 56 * 1024 * 1024

_C_A, _C_B, _C_Q, _C_KV, _C_KR, _C_GC, _C_END = 0, 768, 1536, 1920, 2176, 2432, 2944
_Q_SCALE = ATTN_SCALE * math.log2(math.e)


def _dot(a, b):
    return jnp.dot(a, b, preferred_element_type=F32)


def _dot_nt(a, b):
    return lax.dot_general(a, b, (((1,), (1,)), ((), ())), preferred_element_type=F32)


def _norm_rows(x, eps):
    mu = jnp.mean(x, axis=-1, keepdims=True)
    xc = x - mu
    var = jnp.mean(xc * xc, axis=-1, keepdims=True)
    return xc * lax.rsqrt(var + eps)


def _gelu_tanh(x):
    return x * (0.5 * (1.0 + jnp.tanh(math.sqrt(2.0 / math.pi) * (x + 0.044715 * (x * x * x)))))


def _silu(x):
    return x * jax.nn.sigmoid(x)


def _ada_kernel(c_ref, w_ref, b_ref, o_ref):
    c = c_ref[...]
    a = _silu(c)
    w = w_ref[0]
    a_hi = a.astype(BF16)
    a_lo = (a - a_hi.astype(F32)).astype(BF16)
    w_hi = w.astype(BF16)
    w_lo = (w - w_hi.astype(F32)).astype(BF16)
    o_ref[0] = _dot(a_hi, w_hi) + (_dot(a_hi, w_lo) + _dot(a_lo, w_hi)) + b_ref[0]


def _ada(c_all, w_ada, b_ada):
    depth = w_ada.shape[0]
    rows = c_all.shape[0]
    n_col = w_ada.shape[2] // D_MODEL
    return pl.pallas_call(
        _ada_kernel,
        grid=(depth, n_col),
        in_specs=[
            pl.BlockSpec((rows, D_MODEL), lambda l, j: (0, 0)),
            pl.BlockSpec((1, D_MODEL, D_MODEL), lambda l, j: (l, 0, j)),
            pl.BlockSpec((1, 1, D_MODEL), lambda l, j: (l, 0, j)),
        ],
        out_specs=pl.BlockSpec((1, rows, D_MODEL), lambda l, j: (l, 0, j)),
        out_shape=jax.ShapeDtypeStruct((depth, rows, w_ada.shape[2]), F32),
        compiler_params=pltpu.CompilerParams(vmem_limit_bytes=VMEM_LIMIT),
        name="ada_mod",
    )(c_all, w_ada, b_ada.reshape(depth, 1, -1))


def _mixer_in_kernel(x_ref, mod_ref, ct_ref, st_ref, past_ref, w_in_ref, vec_ref, qn_ref, ws_ref, bs_ref,
                     dwk_ref, w_pw_ref, w_uq_ref, *rest, tile, chunk, prompt, n_stacked):
    if prompt:
        w_uk_ref, w_uvt_ref = rest[:2]
        rest = rest[2:]
    rest = rest[n_stacked:]
    if prompt:
        (q_ref, lat_ref, kr_ref, yab_ref, gc_ref, conv_ref, k_ref, vt_ref, hist) = rest
    else:
        (q_ref, lat_ref, kr_ref, yab_ref, gc_ref, conv_ref, vst_ref, hist) = rest
    t = pl.program_id(1)

    mod = mod_ref[0]
    shift = mod[:, 0:D_MODEL]
    scale = mod[:, D_MODEL:2 * D_MODEL]
    h = _norm_rows(x_ref[0], LN_EPS) * (1.0 + scale) + shift
    hb = h.astype(BF16)

    vec = vec_ref[...]
    gmlp_g, gmlp_b = vec[0:1], vec[1:2]
    dw_b, conv_g, conv_b, b_pw, kv_norm = vec[2:3], vec[3:4], vec[4:5], vec[5:6], vec[6:7]

    za = _dot(hb, w_in_ref[:, _C_A:_C_B])
    u = _gelu_tanh(za[:, 0:W_A])
    v = _norm_rows(_gelu_tanh(za[:, W_A:2 * W_A]), LN_EPS) * gmlp_g + gmlp_b
    if not prompt:
        vst_ref[0] = v
    vb = v.astype(BF16)
    rows = lax.broadcasted_iota(jnp.int32, (N_HEADS_A * chunk, W_A), 0)
    cols = lax.broadcasted_iota(jnp.int32, (N_HEADS_A * chunk, W_A), 1)
    own_head = (rows // chunk) == (cols // HEAD_A)
    ws = ws_ref[...]
    bs = bs_ref[...]
    mixed = []
    for c in range(tile // chunk):
        vc = vb[c * chunk:(c + 1) * chunk]
        vbd = jnp.where(own_head, jnp.concatenate([vc] * N_HEADS_A, axis=0), jnp.zeros((), BF16))
        mixed.append(_dot(ws, vbd) + bs)
    s = mixed[0] if len(mixed) == 1 else jnp.concatenate(mixed, axis=0)
    y_a = u * s * _silu(za[:, 2 * W_A:3 * W_A])
    yab_ref[0, :, 0:W_A] = y_a.astype(BF16)

    zb = _dot(hb, w_in_ref[:, _C_B:_C_Q])
    g = zb[:, 0:W_B] * jax.nn.sigmoid(zb[:, W_B:2 * W_B])

    @pl.when(t == 0)
    def _():
        hist[0:HIST] = past_ref[0]

    hist[HIST:HIST + tile] = g
    dwk = dwk_ref[...]
    y = jnp.broadcast_to(dw_b, (tile, W_B))
    first = HIST - (CONV_W - 1)
    grouped = tile >= 8 * 8
    for r in range(8 if grouped else 0):
        rows = tile if r == 0 else tile + 8
        part = None
        for a in range(HIST // 8 + 1):
            k = 8 * a + r - first
            if 0 <= k < CONV_W:
                term = dwk[k:k + 1] * hist[8 * a:8 * a + rows]
                part = term if part is None else part + term
        y = y + (part if r == 0 else part[r:r + tile])
    for k in range(0 if grouped else CONV_W):
        y = y + dwk[k:k + 1] * hist[first + k:first + k + tile]
    new_hist = hist[tile:tile + HIST]
    hist[0:HIST] = new_hist
    conv_ref[0] = new_hist
    y = _silu(_norm_rows(y, LN_EPS) * conv_g + conv_b)
    y = _dot(y.astype(BF16), w_pw_ref[...]) + b_pw
    y_b = y * _silu(zb[:, 2 * W_B:3 * W_B])
    yab_ref[0, :, W_A:W_A + W_B] = y_b.astype(BF16)

    ct = ct_ref[...]
    st = st_ref[...]
    zq = _dot(hb, w_in_ref[:, _C_Q:_C_KV])
    cq = zq * lax.rsqrt(jnp.mean(zq * zq, axis=-1, keepdims=True) + RMS_EPS) * qn_ref[...]
    q2 = _dot(cq.astype(BF16), w_uq_ref[...])
    ctq = ct * _Q_SCALE
    stq = st * _Q_SCALE
    for hd in range(N_HEADS):
        lo = hd * HEAD_LANES
        qh = q2[:, lo:lo + HEAD_LANES] * ctq + q2[:, N_HEADS * HEAD_LANES + lo:N_HEADS * HEAD_LANES + lo + HEAD_LANES] * stq
        q_ref[0, :, lo:lo + HEAD_LANES] = qh.astype(BF16)

    zkv = _dot(hb, w_in_ref[:, _C_KV:_C_KR])
    lat = zkv * lax.rsqrt(jnp.mean(zkv * zkv, axis=-1, keepdims=True) + RMS_EPS) * kv_norm
    lat_ref[0, 0] = lat
    zkr = _dot(hb, w_in_ref[:, _C_KR:_C_GC])
    krot = zkr[:, 0:HEAD_LANES] * ct + zkr[:, HEAD_LANES:2 * HEAD_LANES] * st
    kr_ref[0, 0] = krot[:, QK_NOPE:QK_NOPE + QK_ROPE]

    if prompt:
        latb = lat.astype(BF16)
        kn = _dot(latb, w_uk_ref[...])
        for hd in range(N_HEADS):
            lo = hd * HEAD_LANES
            k_ref[0, :, lo:lo + HEAD_LANES] = (kn[:, lo:lo + HEAD_LANES] + krot).astype(BF16)
        vt = _dot_nt(w_uvt_ref[...], latb)
        vrow = lax.broadcasted_iota(jnp.int32, vt.shape, 0)
        vt = jnp.where((vrow % V_ROWS) == V_DIM, 1.0, vt)
        for c in range(tile // ATT_TILE):
            blk = vt[:, c * ATT_TILE:(c + 1) * ATT_TILE].reshape(N_HEADS, V_ROWS, ATT_TILE)
            vt_ref[0, :, c] = blk.astype(BF16)

    zg = _dot(hb, w_in_ref[:, _C_GC:_C_END])
    gc_ref[0] = _silu(zg).astype(BF16)


def _mixer_in(x, mod, ctab, stab, past, wts, *, tile, chunk, prompt, layer, depth, stacked=None):
    bsz, seq, _ = x.shape
    n_t = seq // tile
    kern = functools.partial(_mixer_in_kernel, tile=tile, chunk=chunk, prompt=prompt,
                             n_stacked=0 if stacked is None else len(stacked))

    def full(a):
        return pl.BlockSpec(a.shape, lambda b, t, _n=a.ndim: (0,) * _n)

    weights = [wts["w_in"], wts["vec"], wts["q_norm"], wts["ws"], wts["bs"], wts["dwk"], wts["w_pw"], wts["w_uq"]]
    if prompt:
        weights += [wts["w_uk"], wts["w_uvt"]]
    in_specs = [
        pl.BlockSpec((1, tile, D_MODEL), lambda b, t: (b, t, 0)),
        pl.BlockSpec((1, 1, 3 * D_MODEL), lambda b, t: (b, 0, 0)),
        pl.BlockSpec((tile, HEAD_LANES), lambda b, t: (t, 0)),
        pl.BlockSpec((tile, HEAD_LANES), lambda b, t: (t, 0)),
        pl.BlockSpec((1, HIST, W_B), lambda b, t: (b, 0, 0)),
    ] + [full(w) for w in weights]
    operands = [x, mod, ctab, stab, past, *weights]
    aliases = {}
    if stacked is not None:
        for i, a in enumerate(stacked):
            aliases[len(operands)] = 1 + i
            in_specs.append(pl.BlockSpec(memory_space=pl.ANY))
            operands.append(a)
    tok = lambda width: pl.BlockSpec((1, tile, width), lambda b, t: (b, t, 0))
    lay = lambda width: pl.BlockSpec((1, 1, tile, width), lambda b, t: (layer, b, t, 0))
    out_specs = [tok(N_HEADS * HEAD_LANES), lay(KV_LORA), lay(QK_ROPE), tok(W_A + W_B), tok(W_C),
                 pl.BlockSpec((1, HIST, W_B), lambda b, t: (b, 0, 0))]
    out_shape = [
        jax.ShapeDtypeStruct((bsz, seq, N_HEADS * HEAD_LANES), BF16),
        jax.ShapeDtypeStruct((depth, bsz, seq, KV_LORA), F32),
        jax.ShapeDtypeStruct((depth, bsz, seq, QK_ROPE), F32),
        jax.ShapeDtypeStruct((bsz, seq, W_A + W_B), BF16),
        jax.ShapeDtypeStruct((bsz, seq, W_C), BF16),
        jax.ShapeDtypeStruct((bsz, HIST, W_B), F32),
    ]
    if prompt:
        n_kt = tile // ATT_TILE
        out_specs += [tok(N_HEADS * HEAD_LANES),
                      pl.BlockSpec((1, N_HEADS, n_kt, V_ROWS, ATT_TILE), lambda b, t: (b, 0, t, 0, 0))]
        out_shape += [jax.ShapeDtypeStruct((bsz, seq, N_HEADS * HEAD_LANES), BF16),
                      jax.ShapeDtypeStruct((bsz, N_HEADS, seq // ATT_TILE, V_ROWS, ATT_TILE), BF16)]
    else:
        out_specs += [tok(W_A)]
        out_shape += [jax.ShapeDtypeStruct((bsz, seq, W_A), F32)]
    return pl.pallas_call(
        kern,
        grid=(bsz, n_t),
        in_specs=in_specs,
        out_specs=out_specs,
        out_shape=out_shape,
        scratch_shapes=[pltpu.VMEM((HIST + tile, W_B), F32)],
        input_output_aliases=aliases,
        compiler_params=pltpu.CompilerParams(dimension_semantics=("arbitrary", "arbitrary"),
                                             vmem_limit_bytes=VMEM_LIMIT),
        name="mixer_in_prompt" if prompt else "mixer_in_sample",
    )(*operands)


_S_ADDR = (0, 64)
_O_ADDR = ((128, 148), (168, 188))
_FIXED_TILES = 5


def _attn_prompt_kernel(q_ref, k_ref, vt_ref, bias_ref, o_ref, qt_sc, s_sc, acc_sc, *, n_sub, n_kt):
    qi = pl.program_id(2)

    def tile_index(t):
        return jnp.minimum(t, n_kt - 1)

    class SubTile:
        def __init__(self, sub):
            self.sp = sub % 2
            self.row0 = sub * ATT_TILE
            self.g = qi * n_sub + sub
            self.n_steady = jnp.maximum((self.g - 3) // 2, 0)
            self.j0 = _FIXED_TILES + 2 * self.n_steady - 3
            q_t = q_ref[0, self.row0:self.row0 + ATT_TILE, :].astype(F32).T
            zero = jnp.zeros((HEAD_LANES, ATT_TILE), BF16)
            for hd in range(2):
                own = slice(hd * HEAD_LANES, (hd + 1) * HEAD_LANES)
                other = slice((1 - hd) * HEAD_LANES, (2 - hd) * HEAD_LANES)
                qt_sc[self.sp, hd, own] = q_t[own].astype(BF16)
                qt_sc[self.sp, hd, other] = zero
            acc_sc[self.sp] = jnp.zeros(acc_sc.shape[1:], F32)
            neg = jnp.full((1, ATT_TILE), -jnp.inf, F32)
            one = jnp.ones((1, ATT_TILE), F32)
            self.state = ([neg, neg], [one, one], [one, one], [one, one])

        def stage_qt(self):
            for hd in range(2):
                pltpu.matmul_push_rhs(qt_sc[self.sp, hd], staging_register=0, mxu_index=hd)

        def stage_a(self, t, par, restage):
            k_tile = k_ref[0, pl.ds(pl.multiple_of(tile_index(t) * ATT_TILE, ATT_TILE), ATT_TILE), :]
            for hd in range(2):
                pltpu.matmul_acc_lhs(_S_ADDR[par], k_tile, mxu_index=hd, load_staged_rhs=0)
                if restage:
                    pltpu.matmul_push_rhs(qt_sc[self.sp, hd], staging_register=0, mxu_index=hd)

        def stage_b(self, t, par, masked, m_old=None):
            m_old = self.state[0] if m_old is None else m_old
            m_new, alpha = [], []
            if masked:
                bias = bias_ref[jnp.where(t < self.g, 0, jnp.where(t == self.g, 1, 2))]
            for hd in range(2):
                s = pltpu.matmul_pop(_S_ADDR[par], (ATT_TILE, ATT_TILE), F32, mxu_index=hd)
                if masked:
                    s = s + bias
                m_hd = jnp.maximum(m_old[hd], jnp.max(s, axis=0, keepdims=True))
                alpha.append(jnp.exp2(m_old[hd] - m_hd))
                m_new.append(m_hd)
                s_sc[self.sp, hd, par] = s
            return m_new, alpha

        def stage_c(self, t, par, m_t=None):
            m_t = self.state[0] if m_t is None else m_t
            for hd in range(2):
                p = jnp.exp2(s_sc[self.sp, hd, par] - m_t[hd]).astype(BF16)
                pltpu.matmul_push_rhs(p, staging_register=1, mxu_index=hd)
                pltpu.matmul_acc_lhs(_O_ADDR[self.sp][par], vt_ref[0, hd, tile_index(t)], mxu_index=hd,
                                     load_staged_rhs=1)

        def stage_d(self, par, alpha_t=None):
            alpha_t = self.state[3] if alpha_t is None else alpha_t
            for hd in range(2):
                o = pltpu.matmul_pop(_O_ADDR[self.sp][par], (V_ROWS, ATT_TILE), F32, mxu_index=hd)
                acc_sc[self.sp, hd] = alpha_t[hd] * acc_sc[self.sp, hd] + o

        def finish(self):
            outs = []
            for hd in range(2):
                acc = acc_sc[self.sp, hd]
                outs.append(acc[0:V_DIM] * (1.0 / acc[V_DIM:V_DIM + 1]))
            o_t = jnp.concatenate(outs, axis=0)
            o_ref[0, self.row0:self.row0 + ATT_TILE, :] = o_t.T.astype(BF16)

    def run(steps):
        for st, j, par, f in steps:
            if f.get("d", True):
                st.stage_d(par)
        new = []
        for st, j, par, f in steps:
            m, al1, al2, al3 = st.state
            if f.get("b", True):
                m_next, al0 = st.stage_b(j + 1, 1 - par, f.get("masked", False))
            else:
                m_next, al0 = m, al1
            new.append((m_next, al0, al1, al2))
        for st, j, par, f in steps:
            if f.get("a", True):
                st.stage_a(j + 2, par, f.get("restage", True))
            if f.get("qt_of") is not None:
                f["qt_of"].stage_qt()
        for st, j, par, f in steps:
            if f.get("c", True):
                st.stage_c(j, par)
        for (st, j, par, f), state in zip(steps, new):
            st.state = state

    fill = [dict(b=False, c=False, d=False), dict(c=False, d=False, masked=True),
            dict(d=False, masked=True), dict(d=False, masked=True)]

    def drain(nxt):
        return [dict(masked=True, restage=False, qt_of=nxt), dict(a=False, masked=True),
                dict(a=False, b=False), dict(a=False, b=False, c=False), dict(a=False, b=False, c=False)]

    cur = SubTile(0)
    cur.stage_qt()
    for i in range(4):
        run([(cur, i - 2, i % 2, fill[i])])
    for sub in range(n_sub):
        def steady(state, j, n_iter, st=cur):
            st.state = state
            for i in range(n_iter):
                run([(st, j + i, i % 2, {})])
            return st.state

        quads = cur.n_steady // 2
        cur.state = lax.fori_loop(0, quads, lambda u, s: steady(s, 2 + 4 * u, 4), cur.state)
        cur.state = lax.cond(cur.n_steady % 2 == 1, lambda s: steady(s, 2 + 4 * quads, 2), lambda s: s,
                             cur.state)
        nxt = SubTile(sub + 1) if sub + 1 < n_sub else None
        dr = drain(nxt)
        run([(cur, cur.j0, 0, dr[0])])
        for i in range(4):
            steps = [(cur, cur.j0 + 1 + i, (i + 1) % 2, dr[i + 1])]
            if nxt is not None:
                steps.append((nxt, i - 2, i % 2, fill[i]))
            run(steps)
        cur.finish()
        cur = nxt


def _attn_prompt(q, k, vt):
    bsz, seq, _ = q.shape
    qb = min(Q_BLOCK_ROWS, seq)
    n_sub = qb // ATT_TILE
    n_kt = seq // ATT_TILE
    kern = functools.partial(_attn_prompt_kernel, n_sub=n_sub, n_kt=n_kt)
    idx = jnp.arange(ATT_TILE) // CHUNK
    diag = jnp.where(idx[:, None] <= idx[None, :], 0.0, -jnp.inf).astype(F32)
    bias = jnp.stack([jnp.zeros_like(diag), diag, jnp.full_like(diag, -jnp.inf)])
    return pl.pallas_call(
        kern,
        grid=(bsz, N_HEADS // 2, seq // qb),
        in_specs=[
            pl.BlockSpec((1, qb, 2 * HEAD_LANES), lambda b, hp, i: (b, i, hp)),
            pl.BlockSpec((1, seq, 2 * HEAD_LANES), lambda b, hp, i: (b, 0, hp)),
            pl.BlockSpec((1, 2, n_kt, V_ROWS, ATT_TILE), lambda b, hp, i: (b, hp, 0, 0, 0)),
            pl.BlockSpec((3, ATT_TILE, ATT_TILE), lambda b, hp, i: (0, 0, 0)),
        ],
        out_specs=pl.BlockSpec((1, qb, 2 * V_DIM), lambda b, hp, i: (b, i, hp)),
        out_shape=jax.ShapeDtypeStruct((bsz, seq, W_C), BF16),
        scratch_shapes=[pltpu.VMEM((2, 2, 2 * HEAD_LANES, ATT_TILE), BF16),
                        pltpu.VMEM((2, 2, 2, ATT_TILE, ATT_TILE), F32),
                        pltpu.VMEM((2, 2, V_ROWS, ATT_TILE), F32)],
        compiler_params=pltpu.CompilerParams(dimension_semantics=("arbitrary", "arbitrary", "arbitrary"),
                                             vmem_limit_bytes=VMEM_LIMIT),
        name="attn_prompt",
    )(q, k, vt, bias)


def _attn_sample_kernel(q_ref, latc_ref, krc_ref, latn_ref, krn_ref, wabs_ref, fold_ref, wuv_ref, o_ref, *, t_new):
    q = q_ref[0]
    qrep = jnp.concatenate([q] * N_HEADS, axis=0)
    r1 = lax.broadcasted_iota(jnp.int32, qrep.shape, 0)
    c1 = lax.broadcasted_iota(jnp.int32, qrep.shape, 1)
    qrep = jnp.where((r1 // t_new) == (c1 // HEAD_LANES), qrep, jnp.zeros((), BF16))
    qlat = _dot(qrep, wabs_ref[...]).astype(BF16)
    qr = _dot(qrep, fold_ref[...])[:, 0:QK_ROPE].astype(BF16)
    latc = latc_ref[0, 0].astype(BF16)
    krc = krc_ref[0, 0].astype(BF16)
    latn = latn_ref[0, 0].astype(BF16)
    krn = krn_ref[0, 0].astype(BF16)
    s_c = _dot_nt(qlat, latc) + _dot_nt(qr, krc)
    s_n = _dot_nt(qlat, latn) + _dot_nt(qr, krn)
    m = jnp.maximum(jnp.max(s_c, axis=-1, keepdims=True), jnp.max(s_n, axis=-1, keepdims=True))
    p_c = jnp.exp2(s_c - m)
    p_n = jnp.exp2(s_n - m)
    den = jnp.sum(p_c, axis=-1, keepdims=True) + jnp.sum(p_n, axis=-1, keepdims=True)
    olat = (_dot(p_c.astype(BF16), latc) + _dot(p_n.astype(BF16), latn)) * (1.0 / den)
    of = _dot(olat.astype(BF16), wuv_ref[...])
    r2 = lax.broadcasted_iota(jnp.int32, of.shape, 0)
    c2 = lax.broadcasted_iota(jnp.int32, of.shape, 1)
    of = jnp.where((r2 // t_new) == (c2 // V_DIM), of, 0.0)
    o = of[0:t_new]
    for hd in range(1, N_HEADS):
        o = o + of[hd * t_new:(hd + 1) * t_new]
    o_ref[0] = o.astype(BF16)


def _attn_sample(q, lat_cache, kr_cache, layer, lat_new, kr_new, wts):
    bsz, t_new, _ = q.shape
    past = lat_cache.shape[2]
    kern = functools.partial(_attn_sample_kernel, t_new=t_new)

    def full(a):
        return pl.BlockSpec(a.shape, lambda b, _n=a.ndim: (0,) * _n)

    return pl.pallas_call(
        kern,
        grid=(bsz,),
        in_specs=[
            pl.BlockSpec((1, t_new, N_HEADS * HEAD_LANES), lambda b: (b, 0, 0)),
            pl.BlockSpec((1, 1, past, KV_LORA), lambda b: (layer, b, 0, 0)),
            pl.BlockSpec((1, 1, past, QK_ROPE), lambda b: (layer, b, 0, 0)),
            pl.BlockSpec((1, 1, t_new, KV_LORA), lambda b: (layer, b, 0, 0)),
            pl.BlockSpec((1, 1, t_new, QK_ROPE), lambda b: (layer, b, 0, 0)),
            full(wts["wabs"]), full(wts["fold"]), full(wts["w_uv"]),
        ],
        out_specs=pl.BlockSpec((1, t_new, W_C), lambda b: (b, 0, 0)),
        out_shape=jax.ShapeDtypeStruct((bsz, t_new, W_C), BF16),
        compiler_params=pltpu.CompilerParams(dimension_semantics=("arbitrary",), vmem_limit_bytes=VMEM_LIMIT),
        name="attn_sample",
    )(q, lat_cache, kr_cache, lat_new, kr_new, wts["wabs"], wts["fold"], wts["w_uv"])


def _mixer_out_kernel(x_ref, mod_ref, yab_ref, o_ref, gc_ref, w_out_ref, ln_ref, out_ref, *, alpha):
    gate = mod_ref[0][:, 2 * D_MODEL:3 * D_MODEL]
    yc = (o_ref[0].astype(F32) * gc_ref[0].astype(F32)).astype(BF16)
    y = _dot(yab_ref[0], w_out_ref[0:W_A + W_B]) + _dot(yc, w_out_ref[W_A + W_B:W_A + W_B + W_C])
    r = alpha * x_ref[0] + gate * y
    ln = ln_ref[...]
    out_ref[0] = _norm_rows(r, LN_EPS) * ln[0:1] + ln[1:2]


def _mixer_out(x, mod, yab, o, gc, w_out, ln, *, tile, alpha):
    bsz, seq, _ = x.shape
    tok = lambda width: pl.BlockSpec((1, tile, width), lambda b, t: (b, t, 0))
    return pl.pallas_call(
        functools.partial(_mixer_out_kernel, alpha=alpha),
        grid=(bsz, seq // tile),
        in_specs=[tok(D_MODEL), pl.BlockSpec((1, 1, 3 * D_MODEL), lambda b, t: (b, 0, 0)),
                  tok(W_A + W_B), tok(W_C), tok(W_C),
                  pl.BlockSpec(w_out.shape, lambda b, t: (0, 0)),
                  pl.BlockSpec(ln.shape, lambda b, t: (0, 0))],
        out_specs=tok(D_MODEL),
        out_shape=jax.ShapeDtypeStruct(x.shape, F32),
        compiler_params=pltpu.CompilerParams(dimension_semantics=("arbitrary", "arbitrary"),
                                             vmem_limit_bytes=VMEM_LIMIT),
        name="mixer_out",
    )(x, mod, yab, o, gc, w_out, ln)


def _rope_tables(pos0, n):
    inv = ROPE_THETA ** (-jnp.arange(0, QK_ROPE, 2, dtype=F32) / QK_ROPE)
    ang = (pos0 + jnp.arange(n, dtype=jnp.int32)).astype(F32)[:, None] * inv[None, :]
    cos, sin = lax.optimization_barrier((jnp.cos(ang), jnp.sin(ang)))
    ones = jnp.ones((n, QK_NOPE), F32)
    z64 = jnp.zeros((n, QK_NOPE), F32)
    z32 = jnp.zeros((n, HEAD_LANES - QK_NOPE - QK_ROPE), F32)
    return jnp.concatenate([ones, cos, cos, z32], axis=1), jnp.concatenate([z64, sin, sin, z32], axis=1)


def _gmlp_weights(ws, b_s, chunk):
    idx = jnp.arange(chunk)
    mask = (idx[None, :] // CHUNK) <= (idx[:, None] // CHUNK)
    wsm = jnp.where(mask[None], ws[:, :chunk, :chunk], 0.0)
    ws_all = jnp.transpose(wsm, (1, 0, 2)).reshape(chunk, N_HEADS_A * chunk)
    bs_tab = jnp.repeat(b_s[:, :chunk].T, HEAD_A, axis=1)
    return ws_all.astype(BF16), bs_tab.astype(F32)


def _layer_weights(l, p, chunk_p, chunk_s):
    half = QK_ROPE // 2
    w_in = p["w_in"][l]
    krw = w_in[:, 2176:2208]
    x1, x2 = krw[:, :half], krw[:, half:]
    z64 = jnp.zeros((D_MODEL, QK_NOPE), F32)
    z32 = jnp.zeros((D_MODEL, HEAD_LANES - QK_NOPE - QK_ROPE), F32)
    w_in2 = jnp.concatenate([w_in[:, :2176], z64, x1, x2, z32, z64, -x2, x1, z32, w_in[:, 2208:]], axis=1)

    wq = p["mla_w_uq"][l].reshape(Q_LORA, N_HEADS, QK_NOPE + QK_ROPE)
    qn, q1, q2 = wq[..., :QK_NOPE], wq[..., QK_NOPE:QK_NOPE + half], wq[..., QK_NOPE + half:]
    zq32 = jnp.zeros((Q_LORA, N_HEADS, HEAD_LANES - QK_NOPE - QK_ROPE), F32)
    zq64 = jnp.zeros((Q_LORA, N_HEADS, QK_NOPE), F32)
    w_uq = jnp.concatenate([
        jnp.concatenate([qn, q1, q2, zq32], axis=-1).reshape(Q_LORA, -1),
        jnp.concatenate([zq64, -q2, q1, zq32], axis=-1).reshape(Q_LORA, -1)], axis=1)

    wkv = p["mla_w_ukv"][l].reshape(KV_LORA, N_HEADS, QK_NOPE + V_DIM)
    wk, wv = wkv[..., :QK_NOPE], wkv[..., QK_NOPE:]
    w_uk = jnp.concatenate([wk, jnp.zeros_like(wk)], axis=-1).reshape(KV_LORA, -1)
    wvt = jnp.transpose(wv, (1, 2, 0))
    w_uvt = jnp.concatenate([wvt, jnp.zeros((N_HEADS, V_ROWS - V_DIM, KV_LORA), F32)], axis=1).reshape(-1, KV_LORA)
    wkt = jnp.transpose(wk, (1, 2, 0))
    wabs = jnp.concatenate([wkt, jnp.zeros((N_HEADS, HEAD_LANES - QK_NOPE, KV_LORA), F32)], axis=1).reshape(-1, KV_LORA)
    eye = jnp.eye(QK_ROPE, HEAD_LANES, dtype=F32)
    fold_h = jnp.concatenate([jnp.zeros((QK_NOPE, HEAD_LANES), F32), eye,
                              jnp.zeros((HEAD_LANES - QK_NOPE - QK_ROPE, HEAD_LANES), F32)], axis=0)
    fold = jnp.tile(fold_h, (N_HEADS, 1))

    zrow = jnp.zeros((1, W_B), F32)
    vec = jnp.stack([p["gmlp_ln_g"][l], p["gmlp_ln_b"][l], p["conv_dw_b"][l], p["conv_ln_g"][l],
                     p["conv_ln_b"][l], p["conv_b_pw"][l], p["mla_kv_norm"][l], zrow[0]], axis=0)
    common = dict(
        w_in=w_in2.astype(BF16), vec=vec, q_norm=p["mla_q_norm"][l].reshape(1, Q_LORA),
        dwk=jnp.concatenate([p["conv_dw_k"][l], zrow], axis=0), w_pw=p["conv_w_pw"][l].astype(BF16),
        w_uq=w_uq.astype(BF16), w_uk=w_uk.astype(BF16), w_uvt=w_uvt.astype(BF16),
        wabs=wabs.astype(BF16), fold=fold.astype(BF16), w_uv=wv.reshape(KV_LORA, W_C).astype(BF16),
        w_out=p["w_out"][l].astype(BF16), ln=jnp.stack([p["post_ln_g"][l], p["post_ln_b"][l]], axis=0))
    ws_p, bs_p = _gmlp_weights(p["gmlp_ws"][l], p["gmlp_bs"][l], chunk_p)
    ws_s, bs_s = _gmlp_weights(p["gmlp_ws"][l], p["gmlp_bs"][l], chunk_s)
    return dict(common, ws=ws_p, bs=bs_p), dict(common, ws=ws_s, bs=bs_s)


def kernel(x_prompt, x_sample, cache_latent, cache_krope, state_conv, c_prompt, c_sample,
           w_ada, b_ada, w_in, gmlp_ln_g, gmlp_ln_b, gmlp_ws, gmlp_bs,
           conv_dw_k, conv_dw_b, conv_ln_g, conv_ln_b, conv_w_pw, conv_b_pw,
           mla_q_norm, mla_w_uq, mla_kv_norm, mla_w_ukv, w_out, post_ln_g, post_ln_b):
    p = dict(w_in=w_in, gmlp_ln_g=gmlp_ln_g, gmlp_ln_b=gmlp_ln_b, gmlp_ws=gmlp_ws, gmlp_bs=gmlp_bs,
             conv_dw_k=conv_dw_k, conv_dw_b=conv_dw_b, conv_ln_g=conv_ln_g, conv_ln_b=conv_ln_b,
             conv_w_pw=conv_w_pw, conv_b_pw=conv_b_pw, mla_q_norm=mla_q_norm, mla_w_uq=mla_w_uq,
             mla_kv_norm=mla_kv_norm, mla_w_ukv=mla_w_ukv, w_out=w_out, post_ln_g=post_ln_g, post_ln_b=post_ln_b)
    depth = w_ada.shape[0]
    bp, seq, _ = x_prompt.shape
    bs, t_new, _ = x_sample.shape
    past_len = cache_latent.shape[2]
    alpha = (2 * depth) ** 0.25
    tile_p = min(IN_TILE, seq)
    chunk_p = min(seq, MLP_CHUNK)
    chunk_s = min(t_new, MLP_CHUNK)

    n_c = bp + bs
    c_all = jnp.concatenate([c_prompt, c_sample, jnp.zeros((-n_c % 8, D_MODEL), F32)], axis=0)
    mod_all = _ada(c_all, w_ada, b_ada)

    ct_p, st_p = _rope_tables(0, seq)
    ct_s, st_s = _rope_tables(past_len, t_new)
    zero_past = jnp.zeros((bp, HIST, W_B), F32)

    xp, xs = x_prompt, x_sample
    p_conv, s_conv, s_v = [], [], []
    p_stack = s_stack = None
    for l in range(depth):
        wp, wsm = _layer_weights(l, p, chunk_p, chunk_s)
        mod_p = mod_all[l, 0:bp].reshape(bp, 1, -1)
        mod_s = mod_all[l, bp:n_c].reshape(bs, 1, -1)

        q, lat, kr, yab, gc, conv, k, vt = _mixer_in(xp, mod_p, ct_p, st_p, zero_past, wp, tile=tile_p,
                                                      chunk=chunk_p, prompt=True, layer=l, depth=depth,
                                                      stacked=p_stack)
        p_stack = (lat, kr)
        o = _attn_prompt(q, k, vt)
        xp = _mixer_out(xp, mod_p, yab, o, gc, wp["w_out"], wp["ln"], tile=tile_p, alpha=alpha)
        p_conv.append(conv[:, HIST - (CONV_W - 1):])

        past = jnp.concatenate([jnp.zeros((bs, HIST - (CONV_W - 1), W_B), F32), state_conv[l]], axis=1)
        q, lat, kr, yab, gc, conv, vst = _mixer_in(xs, mod_s, ct_s, st_s, past, wsm, tile=t_new,
                                                   chunk=chunk_s, prompt=False, layer=l, depth=depth,
                                                   stacked=s_stack)
        s_stack = (lat, kr)
        o = _attn_sample(q, cache_latent, cache_krope, l, lat, kr, wsm)
        xs = _mixer_out(xs, mod_s, yab, o, gc, wsm["w_out"], wsm["ln"], tile=t_new, alpha=alpha)
        s_conv.append(conv[:, HIST - (CONV_W - 1):])
        s_v.append(vst)

    return (xp, xs, jnp.stack(p_conv), p_stack[0], p_stack[1],
            jnp.stack(s_conv), s_stack[0], s_stack[1], jnp.stack(s_v))
```

```python
import functools
import math

import jax
import jax.numpy as jnp
from jax import lax
from jax.experimental import pallas as pl
from jax.experimental.pallas import tpu as pltpu

F32 = jnp.float32
BF16 = jnp.bfloat16

D_MODEL = 1024
N_HEADS_A = 4
HEAD_A = 64
W_A = 256
W_B = 256
CONV_W = 31
N_HEADS = 8
QK_NOPE = 64
QK_ROPE = 32
V_DIM = 64
W_C = N_HEADS * V_DIM
Q_LORA = 384
KV_LORA = 256
CHUNK = 64
MLP_CHUNK = 128
ROPE_THETA = 10000.0
ATTN_SCALE = (QK_NOPE + QK_ROPE) ** -0.5
LN_EPS = 1e-5
RMS_EPS = 1e-6

HEAD_LANES = 128
V_ROWS = 80
HIST = 32
ATT_TILE = 256
Q_BLOCK_ROWS = 4096
IN_TILE = 1024
VMEM_LIMIT = 56 * 1024 * 1024

_C_A, _C_B, _C_Q, _C_KV, _C_KR, _C_GC, _C_END = 0, 768, 1536, 1920, 2176, 2432, 2944
_Q_SCALE = ATTN_SCALE * math.log2(math.e)


def _dot(a, b):
    return jnp.dot(a, b, preferred_element_type=F32)


def _dot_nt(a, b):
    return lax.dot_general(a, b, (((1,), (1,)), ((), ())), preferred_element_type=F32)


def _norm_rows(x, eps):
    mu = jnp.mean(x, axis=-1, keepdims=True)
    xc = x - mu
    var = jnp.mean(xc * xc, axis=-1, keepdims=True)
    return xc * lax.rsqrt(var + eps)


def _gelu_tanh(x):
    return x * (0.5 * (1.0 + jnp.tanh(math.sqrt(2.0 / math.pi) * (x + 0.044715 * (x * x * x)))))


def _silu(x):
    return x * jax.nn.sigmoid(x)


def _ada_kernel(c_ref, w_ref, b_ref, o_ref):
    c = c_ref[...]
    a = _silu(c)
    w = w_ref[0]
    a_hi = a.astype(BF16)
    a_lo = (a - a_hi.astype(F32)).astype(BF16)
    w_hi = w.astype(BF16)
    w_lo = (w - w_hi.astype(F32)).astype(BF16)
    o_ref[0] = _dot(a_hi, w_hi) + (_dot(a_hi, w_lo) + _dot(a_lo, w_hi)) + b_ref[0]


def _ada(c_all, w_ada, b_ada):
    depth = w_ada.shape[0]
    rows = c_all.shape[0]
    n_col = w_ada.shape[2] // D_MODEL
    return pl.pallas_call(
        _ada_kernel,
        grid=(depth, n_col),
        in_specs=[
            pl.BlockSpec((rows, D_MODEL), lambda l, j: (0, 0)),
            pl.BlockSpec((1, D_MODEL, D_MODEL), lambda l, j: (l, 0, j)),
            pl.BlockSpec((1, 1, D_MODEL), lambda l, j: (l, 0, j)),
        ],
        out_specs=pl.BlockSpec((1, rows, D_MODEL), lambda l, j: (l, 0, j)),
        out_shape=jax.ShapeDtypeStruct((depth, rows, w_ada.shape[2]), F32),
        compiler_params=pltpu.CompilerParams(vmem_limit_bytes=VMEM_LIMIT),
        name="ada_mod",
    )(c_all, w_ada, b_ada.reshape(depth, 1, -1))


def _mixer_in_kernel(x_ref, mod_ref, ct_ref, st_ref, past_ref, w_in_ref, vec_ref, qn_ref, ws_ref, bs_ref,
                     dwk_ref, w_pw_ref, w_uq_ref, *rest, tile, chunk, prompt, n_stacked):
    if prompt:
        w_uk_ref, w_uvt_ref = rest[:2]
        rest = rest[2:]
    rest = rest[n_stacked:]
    if prompt:
        (q_ref, lat_ref, kr_ref, yab_ref, gc_ref, conv_ref, k_ref, vt_ref, hist) = rest
    else:
        (q_ref, lat_ref, kr_ref, yab_ref, gc_ref, conv_ref, vst_ref, hist) = rest
    t = pl.program_id(1)

    mod = mod_ref[0]
    shift = mod[:, 0:D_MODEL]
    scale = mod[:, D_MODEL:2 * D_MODEL]
    h = _norm_rows(x_ref[0], LN_EPS) * (1.0 + scale) + shift
    hb = h.astype(BF16)

    vec = vec_ref[...]
    gmlp_g, gmlp_b = vec[0:1], vec[1:2]
    dw_b, conv_g, conv_b, b_pw, kv_norm = vec[2:3], vec[3:4], vec[4:5], vec[5:6], vec[6:7]

    za = _dot(hb, w_in_ref[:, _C_A:_C_B])
    u = _gelu_tanh(za[:, 0:W_A])
    v = _norm_rows(_gelu_tanh(za[:, W_A:2 * W_A]), LN_EPS) * gmlp_g + gmlp_b
    if not prompt:
        vst_ref[0] = v
    vb = v.astype(BF16)
    rows = lax.broadcasted_iota(jnp.int32, (N_HEADS_A * chunk, W_A), 0)
    cols = lax.broadcasted_iota(jnp.int32, (N_HEADS_A * chunk, W_A), 1)
    own_head = (rows // chunk) == (cols // HEAD_A)
    ws = ws_ref[...]
    bs = bs_ref[...]
    mixed = []
    for c in range(tile // chunk):
        vc = vb[c * chunk:(c + 1) * chunk]
        vbd = jnp.where(own_head, jnp.concatenate([vc] * N_HEADS_A, axis=0), jnp.zeros((), BF16))
        mixed.append(_dot(ws, vbd) + bs)
    s = mixed[0] if len(mixed) == 1 else jnp.concatenate(mixed, axis=0)
    y_a = u * s * _silu(za[:, 2 * W_A:3 * W_A])
    yab_ref[0, :, 0:W_A] = y_a.astype(BF16)

    zb = _dot(hb, w_in_ref[:, _C_B:_C_Q])
    g = zb[:, 0:W_B] * jax.nn.sigmoid(zb[:, W_B:2 * W_B])

    @pl.when(t == 0)
    def _():
        hist[0:HIST] = past_ref[0]

    hist[HIST:HIST + tile] = g
    dwk = dwk_ref[...]
    y = jnp.broadcast_to(dw_b, (tile, W_B))
    first = HIST - (CONV_W - 1)
    grouped = tile >= 8 * 8
    for r in range(8 if grouped else 0):
        rows = tile if r == 0 else tile + 8
        part = None
        for a in range(HIST // 8 + 1):
            k = 8 * a + r - first
            if 0 <= k < CONV_W:
                term = dwk[k:k + 1] * hist[8 * a:8 * a + rows]
                part = term if part is None else part + term
        y = y + (part if r == 0 else part[r:r + tile])
    for k in range(0 if grouped else CONV_W):
        y = y + dwk[k:k + 1] * hist[first + k:first + k + tile]
    new_hist = hist[tile:tile + HIST]
    hist[0:HIST] = new_hist
    conv_ref[0] = new_hist
    y = _silu(_norm_rows(y, LN_EPS) * conv_g + conv_b)
    y = _dot(y.astype(BF16), w_pw_ref[...]) + b_pw
    y_b = y * _silu(zb[:, 2 * W_B:3 * W_B])
    yab_ref[0, :, W_A:W_A + W_B] = y_b.astype(BF16)

    ct = ct_ref[...]
    st = st_ref[...]
    zq = _dot(hb, w_in_ref[:, _C_Q:_C_KV])
    cq = zq * lax.rsqrt(jnp.mean(zq * zq, axis=-1, keepdims=True) + RMS_EPS) * qn_ref[...]
    q2 = _dot(cq.astype(BF16), w_uq_ref[...])
    ctq = ct * _Q_SCALE
    stq = st * _Q_SCALE
    for hd in range(N_HEADS):
        lo = hd * HEAD_LANES
        qh = q2[:, lo:lo + HEAD_LANES] * ctq + q2[:, N_HEADS * HEAD_LANES + lo:N_HEADS * HEAD_LANES + lo + HEAD_LANES] * stq
        q_ref[0, :, lo:lo + HEAD_LANES] = qh.astype(BF16)

    zkv = _dot(hb, w_in_ref[:, _C_KV:_C_KR])
    lat = zkv * lax.rsqrt(jnp.mean(zkv * zkv, axis=-1, keepdims=True) + RMS_EPS) * kv_norm
    lat_ref[0, 0] = lat
    zkr = _dot(hb, w_in_ref[:, _C_KR:_C_GC])
    krot = zkr[:, 0:HEAD_LANES] * ct + zkr[:, HEAD_LANES:2 * HEAD_LANES] * st
    kr_ref[0, 0] = krot[:, QK_NOPE:QK_NOPE + QK_ROPE]

    if prompt:
        latb = lat.astype(BF16)
        kn = _dot(latb, w_uk_ref[...])
        for hd in range(N_HEADS):
            lo = hd * HEAD_LANES
            k_ref[0, :, lo:lo + HEAD_LANES] = (kn[:, lo:lo + HEAD_LANES] + krot).astype(BF16)
        vt = _dot_nt(w_uvt_ref[...], latb)
        vrow = lax.broadcasted_iota(jnp.int32, vt.shape, 0)
        vt = jnp.where((vrow % V_ROWS) == V_DIM, 1.0, vt)
        for c in range(tile // ATT_TILE):
            blk = vt[:, c * ATT_TILE:(c + 1) * ATT_TILE].reshape(N_HEADS, V_ROWS, ATT_TILE)
            vt_ref[0, :, c] = blk.astype(BF16)

    zg = _dot(hb, w_in_ref[:, _C_GC:_C_END])
    gc_ref[0] = _silu(zg).astype(BF16)


def _mixer_in(x, mod, ctab, stab, past, wts, *, tile, chunk, prompt, layer, depth, stacked=None):
    bsz, seq, _ = x.shape
    n_t = seq // tile
    kern = functools.partial(_mixer_in_kernel, tile=tile, chunk=chunk, prompt=prompt,
                             n_stacked=0 if stacked is None else len(stacked))

    def full(a):
        return pl.BlockSpec(a.shape, lambda b, t, _n=a.ndim: (0,) * _n)

    weights = [wts["w_in"], wts["vec"], wts["q_norm"], wts["ws"], wts["bs"], wts["dwk"], wts["w_pw"], wts["w_uq"]]
    if prompt:
        weights += [wts["w_uk"], wts["w_uvt"]]
    in_specs = [
        pl.BlockSpec((1, tile, D_MODEL), lambda b, t: (b, t, 0)),
        pl.BlockSpec((1, 1, 3 * D_MODEL), lambda b, t: (b, 0, 0)),
        pl.BlockSpec((tile, HEAD_LANES), lambda b, t: (t, 0)),
        pl.BlockSpec((tile, HEAD_LANES), lambda b, t: (t, 0)),
        pl.BlockSpec((1, HIST, W_B), lambda b, t: (b, 0, 0)),
    ] + [full(w) for w in weights]
    operands = [x, mod, ctab, stab, past, *weights]
    aliases = {}
    if stacked is not None:
        for i, a in enumerate(stacked):
            aliases[len(operands)] = 1 + i
            in_specs.append(pl.BlockSpec(memory_space=pl.ANY))
            operands.append(a)
    tok = lambda width: pl.BlockSpec((1, tile, width), lambda b, t: (b, t, 0))
    lay = lambda width: pl.BlockSpec((1, 1, tile, width), lambda b, t: (layer, b, t, 0))
    out_specs = [tok(N_HEADS * HEAD_LANES), lay(KV_LORA), lay(QK_ROPE), tok(W_A + W_B), tok(W_C),
                 pl.BlockSpec((1, HIST, W_B), lambda b, t: (b, 0, 0))]
    out_shape = [
        jax.ShapeDtypeStruct((bsz, seq, N_HEADS * HEAD_LANES), BF16),
        jax.ShapeDtypeStruct((depth, bsz, seq, KV_LORA), F32),
        jax.ShapeDtypeStruct((depth, bsz, seq, QK_ROPE), F32),
        jax.ShapeDtypeStruct((bsz, seq, W_A + W_B), BF16),
        jax.ShapeDtypeStruct((bsz, seq, W_C), BF16),
        jax.ShapeDtypeStruct((bsz, HIST, W_B), F32),
    ]
    if prompt:
        n_kt = tile // ATT_TILE
        out_specs += [tok(N_HEADS * HEAD_LANES),
                      pl.BlockSpec((1, N_HEADS, n_kt, V_ROWS, ATT_TILE), lambda b, t: (b, 0, t, 0, 0))]
        out_shape += [jax.ShapeDtypeStruct((bsz, seq, N_HEADS * HEAD_LANES), BF16),
                      jax.ShapeDtypeStruct((bsz, N_HEADS, seq // ATT_TILE, V_ROWS, ATT_TILE), BF16)]
    else:
        out_specs += [tok(W_A)]
        out_shape += [jax.ShapeDtypeStruct((bsz, seq, W_A), F32)]
    return pl.pallas_call(
        kern,
        grid=(bsz, n_t),
        in_specs=in_specs,
        out_specs=out_specs,
        out_shape=out_shape,
        scratch_shapes=[pltpu.VMEM((HIST + tile, W_B), F32)],
        input_output_aliases=aliases,
        compiler_params=pltpu.CompilerParams(dimension_semantics=("arbitrary", "arbitrary"),
                                             vmem_limit_bytes=VMEM_LIMIT),
        name="mixer_in_prompt" if prompt else "mixer_in_sample",
    )(*operands)


_S_ADDR = (0, 64)
_O_ADDR = ((128, 148), (168, 188))
_FIXED_TILES = 5


def _attn_prompt_kernel(q_ref, k_ref, vt_ref, bias_ref, o_ref, qt_sc, s_sc, acc_sc, *, n_sub, n_kt):
    qi = pl.program_id(2)

    def tile_index(t):
        return jnp.minimum(t, n_kt - 1)

    class SubTile:
        def __init__(self, sub):
            self.sp = sub % 2
            self.row0 = sub * ATT_TILE
            self.g = qi * n_sub + sub
            self.n_steady = jnp.maximum((self.g - 3) // 2, 0)
            self.j0 = _FIXED_TILES + 2 * self.n_steady - 3
            q_t = q_ref[0, self.row0:self.row0 + ATT_TILE, :].astype(F32).T
            zero = jnp.zeros((HEAD_LANES, ATT_TILE), BF16)
            for hd in range(2):
                own = slice(hd * HEAD_LANES, (hd + 1) * HEAD_LANES)
                other = slice((1 - hd) * HEAD_LANES, (2 - hd) * HEAD_LANES)
                qt_sc[self.sp, hd, own] = q_t[own].astype(BF16)
                qt_sc[self.sp, hd, other] = zero
            acc_sc[self.sp] = jnp.zeros(acc_sc.shape[1:], F32)
            neg = jnp.full((1, ATT_TILE), -jnp.inf, F32)
            one = jnp.ones((1, ATT_TILE), F32)
            self.state = ([neg, neg], [one, one], [one, one], [one, one])

        def stage_qt(self):
            for hd in range(2):
                pltpu.matmul_push_rhs(qt_sc[self.sp, hd], staging_register=0, mxu_index=hd)

        def stage_a(self, t, par, restage):
            k_tile = k_ref[0, pl.ds(pl.multiple_of(tile_index(t) * ATT_TILE, ATT_TILE), ATT_TILE), :]
            for hd in range(2):
                pltpu.matmul_acc_lhs(_S_ADDR[par], k_tile, mxu_index=hd, load_staged_rhs=0)
                if restage:
                    pltpu.matmul_push_rhs(qt_sc[self.sp, hd], staging_register=0, mxu_index=hd)

        def stage_b(self, t, par, masked, m_old=None):
            m_old = self.state[0] if m_old is None else m_old
            m_new, alpha = [], []
            if masked:
                bias = bias_ref[jnp.where(t < self.g, 0, jnp.where(t == self.g, 1, 2))]
            for hd in range(2):
                s = pltpu.matmul_pop(_S_ADDR[par], (ATT_TILE, ATT_TILE), F32, mxu_index=hd)
                if masked:
                    s = s + bias
                m_hd = jnp.maximum(m_old[hd], jnp.max(s, axis=0, keepdims=True))
                alpha.append(jnp.exp2(m_old[hd] - m_hd))
                m_new.append(m_hd)
                s_sc[self.sp, hd, par] = s
            return m_new, alpha

        def stage_c(self, t, par, m_t=None):
            m_t = self.state[0] if m_t is None else m_t
            for hd in range(2):
                p = jnp.exp2(s_sc[self.sp, hd, par] - m_t[hd]).astype(BF16)
                pltpu.matmul_push_rhs(p, staging_register=1, mxu_index=hd)
                pltpu.matmul_acc_lhs(_O_ADDR[self.sp][par], vt_ref[0, hd, tile_index(t)], mxu_index=hd,
                                     load_staged_rhs=1)

        def stage_d(self, par, alpha_t=None):
            alpha_t = self.state[3] if alpha_t is None else alpha_t
            for hd in range(2):
                o = pltpu.matmul_pop(_O_ADDR[self.sp][par], (V_ROWS, ATT_TILE), F32, mxu_index=hd)
                acc_sc[self.sp, hd] = alpha_t[hd] * acc_sc[self.sp, hd] + o

        def finish(self):
            outs = []
            for hd in range(2):
                acc = acc_sc[self.sp, hd]
                outs.append(acc[0:V_DIM] * (1.0 / acc[V_DIM:V_DIM + 1]))
            o_t = jnp.concatenate(outs, axis=0)
            o_ref[0, self.row0:self.row0 + ATT_TILE, :] = o_t.T.astype(BF16)

    def run(steps):
        for st, j, par, f in steps:
            if f.get("d", True):
                st.stage_d(par)
        new = []
        for st, j, par, f in steps:
            m, al1, al2, al3 = st.state
            if f.get("b", True):
                m_next, al0 = st.stage_b(j + 1, 1 - par, f.get("masked", False))
            else:
                m_next, al0 = m, al1
            new.append((m_next, al0, al1, al2))
        for st, j, par, f in steps:
            if f.get("a", True):
                st.stage_a(j + 2, par, f.get("restage", True))
            if f.get("qt_of") is not None:
                f["qt_of"].stage_qt()
        for st, j, par, f in steps:
            if f.get("c", True):
                st.stage_c(j, par)
        for (st, j, par, f), state in zip(steps, new):
            st.state = state

    fill = [dict(b=False, c=False, d=False), dict(c=False, d=False, masked=True),
            dict(d=False, masked=True), dict(d=False, masked=True)]

    def drain(nxt):
        return [dict(masked=True, restage=False, qt_of=nxt), dict(a=False, masked=True),
                dict(a=False, b=False), dict(a=False, b=False, c=False), dict(a=False, b=False, c=False)]

    cur = SubTile(0)
    cur.stage_qt()
    for i in range(4):
        run([(cur, i - 2, i % 2, fill[i])])
    for sub in range(n_sub):
        def steady(state, j, n_iter, st=cur):
            st.state = state
            for i in range(n_iter):
                run([(st, j + i, i % 2, {})])
            return st.state

        quads = cur.n_steady // 2
        cur.state = lax.fori_loop(0, quads, lambda u, s: steady(s, 2 + 4 * u, 4), cur.state)
        cur.state = lax.cond(cur.n_steady % 2 == 1, lambda s: steady(s, 2 + 4 * quads, 2), lambda s: s,
                             cur.state)
        nxt = SubTile(sub + 1) if sub + 1 < n_sub else None
        dr = drain(nxt)
        run([(cur, cur.j0, 0, dr[0])])
        for i in range(4):
            steps = [(cur, cur.j0 + 1 + i, (i + 1) % 2, dr[i + 1])]
            if nxt is not None:
                steps.append((nxt, i - 2, i % 2, fill[i]))
            run(steps)
        cur.finish()
        cur = nxt


def _attn_prompt(q, k, vt):
    bsz, seq, _ = q.shape
    qb = min(Q_BLOCK_ROWS, seq)
    n_sub = qb // ATT_TILE
    n_kt = seq // ATT_TILE
    kern = functools.partial(_attn_prompt_kernel, n_sub=n_sub, n_kt=n_kt)
    idx = jnp.arange(ATT_TILE) // CHUNK
    diag = jnp.where(idx[:, None] <= idx[None, :], 0.0, -jnp.inf).astype(F32)
    bias = jnp.stack([jnp.zeros_like(diag), diag, jnp.full_like(diag, -jnp.inf)])
    return pl.pallas_call(
        kern,
        grid=(bsz, N_HEADS // 2, seq // qb),
        in_specs=[
            pl.BlockSpec((1, qb, 2 * HEAD_LANES), lambda b, hp, i: (b, i, hp)),
            pl.BlockSpec((1, seq, 2 * HEAD_LANES), lambda b, hp, i: (b, 0, hp)),
            pl.BlockSpec((1, 2, n_kt, V_ROWS, ATT_TILE), lambda b, hp, i: (b, hp, 0, 0, 0)),
            pl.BlockSpec((3, ATT_TILE, ATT_TILE), lambda b, hp, i: (0, 0, 0)),
        ],
        out_specs=pl.BlockSpec((1, qb, 2 * V_DIM), lambda b, hp, i: (b, i, hp)),
        out_shape=jax.ShapeDtypeStruct((bsz, seq, W_C), BF16),
        scratch_shapes=[pltpu.VMEM((2, 2, 2 * HEAD_LANES, ATT_TILE), BF16),
                        pltpu.VMEM((2, 2, 2, ATT_TILE, ATT_TILE), F32),
                        pltpu.VMEM((2, 2, V_ROWS, ATT_TILE), F32)],
        compiler_params=pltpu.CompilerParams(dimension_semantics=("arbitrary", "arbitrary", "arbitrary"),
                                             vmem_limit_bytes=VMEM_LIMIT),
        name="attn_prompt",
    )(q, k, vt, bias)


def _attn_sample_kernel(q_ref, latc_ref, krc_ref, latn_ref, krn_ref, wabs_ref, fold_ref, wuv_ref, o_ref, *, t_new):
    q = q_ref[0]
    qrep = jnp.concatenate([q] * N_HEADS, axis=0)
    r1 = lax.broadcasted_iota(jnp.int32, qrep.shape, 0)
    c1 = lax.broadcasted_iota(jnp.int32, qrep.shape, 1)
    qrep = jnp.where((r1 // t_new) == (c1 // HEAD_LANES), qrep, jnp.zeros((), BF16))
    qlat = _dot(qrep, wabs_ref[...]).astype(BF16)
    qr = _dot(qrep, fold_ref[...])[:, 0:QK_ROPE].astype(BF16)
    latc = latc_ref[0, 0].astype(BF16)
    krc = krc_ref[0, 0].astype(BF16)
    latn = latn_ref[0, 0].astype(BF16)
    krn = krn_ref[0, 0].astype(BF16)
    s_c = _dot_nt(qlat, latc) + _dot_nt(qr, krc)
    s_n = _dot_nt(qlat, latn) + _dot_nt(qr, krn)
    m = jnp.maximum(jnp.max(s_c, axis=-1, keepdims=True), jnp.max(s_n, axis=-1, keepdims=True))
    p_c = jnp.exp2(s_c - m)
    p_n = jnp.exp2(s_n - m)
    den = jnp.sum(p_c, axis=-1, keepdims=True) + jnp.sum(p_n, axis=-1, keepdims=True)
    olat = (_dot(p_c.astype(BF16), latc) + _dot(p_n.astype(BF16), latn)) * (1.0 / den)
    of = _dot(olat.astype(BF16), wuv_ref[...])
    r2 = lax.broadcasted_iota(jnp.int32, of.shape, 0)
    c2 = lax.broadcasted_iota(jnp.int32, of.shape, 1)
    of = jnp.where((r2 // t_new) == (c2 // V_DIM), of, 0.0)
    o = of[0:t_new]
    for hd in range(1, N_HEADS):
        o = o + of[hd * t_new:(hd + 1) * t_new]
    o_ref[0] = o.astype(BF16)


def _attn_sample(q, lat_cache, kr_cache, layer, lat_new, kr_new, wts):
    bsz, t_new, _ = q.shape
    past = lat_cache.shape[2]
    kern = functools.partial(_attn_sample_kernel, t_new=t_new)

    def full(a):
        return pl.BlockSpec(a.shape, lambda b, _n=a.ndim: (0,) * _n)

    return pl.pallas_call(
        kern,
        grid=(bsz,),
        in_specs=[
            pl.BlockSpec((1, t_new, N_HEADS * HEAD_LANES), lambda b: (b, 0, 0)),
            pl.BlockSpec((1, 1, past, KV_LORA), lambda b: (layer, b, 0, 0)),
            pl.BlockSpec((1, 1, past, QK_ROPE), lambda b: (layer, b, 0, 0)),
            pl.BlockSpec((1, 1, t_new, KV_LORA), lambda b: (layer, b, 0, 0)),
            pl.BlockSpec((1, 1, t_new, QK_ROPE), lambda b: (layer, b, 0, 0)),
            full(wts["wabs"]), full(wts["fold"]), full(wts["w_uv"]),
        ],
        out_specs=pl.BlockSpec((1, t_new, W_C), lambda b: (b, 0, 0)),
        out_shape=jax.ShapeDtypeStruct((bsz, t_new, W_C), BF16),
        compiler_params=pltpu.CompilerParams(dimension_semantics=("arbitrary",), vmem_limit_bytes=VMEM_LIMIT),
        name="attn_sample",
    )(q, lat_cache, kr_cache, lat_new, kr_new, wts["wabs"], wts["fold"], wts["w_uv"])


def _mixer_out_kernel(x_ref, mod_ref, yab_ref, o_ref, gc_ref, w_out_ref, ln_ref, out_ref, *, alpha):
    gate = mod_ref[0][:, 2 * D_MODEL:3 * D_MODEL]
    yc = (o_ref[0].astype(F32) * gc_ref[0].astype(F32)).astype(BF16)
    y = _dot(yab_ref[0], w_out_ref[0:W_A + W_B]) + _dot(yc, w_out_ref[W_A + W_B:W_A + W_B + W_C])
    r = alpha * x_ref[0] + gate * y
    ln = ln_ref[...]
    out_ref[0] = _norm_rows(r, LN_EPS) * ln[0:1] + ln[1:2]


def _mixer_out(x, mod, yab, o, gc, w_out, ln, *, tile, alpha):
    bsz, seq, _ = x.shape
    tok = lambda width: pl.BlockSpec((1, tile, width), lambda b, t: (b, t, 0))
    return pl.pallas_call(
        functools.partial(_mixer_out_kernel, alpha=alpha),
        grid=(bsz, seq // tile),
        in_specs=[tok(D_MODEL), pl.BlockSpec((1, 1, 3 * D_MODEL), lambda b, t: (b, 0, 0)),
                  tok(W_A + W_B), tok(W_C), tok(W_C),
                  pl.BlockSpec(w_out.shape, lambda b, t: (0, 0)),
                  pl.BlockSpec(ln.shape, lambda b, t: (0, 0))],
        out_specs=tok(D_MODEL),
        out_shape=jax.ShapeDtypeStruct(x.shape, F32),
        compiler_params=pltpu.CompilerParams(dimension_semantics=("arbitrary", "arbitrary"),
                                             vmem_limit_bytes=VMEM_LIMIT),
        name="mixer_out",
    )(x, mod, yab, o, gc, w_out, ln)


def _rope_tables(pos0, n):
    half = QK_ROPE // 2
    inv = ROPE_THETA ** (-jnp.arange(0, QK_ROPE, 2, dtype=F32) / QK_ROPE)
    per_row = HEAD_LANES // half
    lane = jnp.arange(HEAD_LANES, dtype=jnp.int32)
    pos = pos0 + jnp.arange(n // per_row, dtype=jnp.int32)[:, None] * per_row + (lane // half)[None, :]
    ang = pos.astype(F32) * inv[lane % half][None, :]
    cos, sin = lax.optimization_barrier((jnp.cos(ang), jnp.sin(ang)))
    cos, sin = cos.reshape(n, half), sin.reshape(n, half)
    ones = jnp.ones((n, QK_NOPE), F32)
    z64 = jnp.zeros((n, QK_NOPE), F32)
    z32 = jnp.zeros((n, HEAD_LANES - QK_NOPE - QK_ROPE), F32)
    return jnp.concatenate([ones, cos, cos, z32], axis=1), jnp.concatenate([z64, sin, sin, z32], axis=1)


def _gmlp_weights(ws, b_s, chunk):
    idx = jnp.arange(chunk)
    mask = (idx[None, :] // CHUNK) <= (idx[:, None] // CHUNK)
    wsm = jnp.where(mask[None], ws[:, :chunk, :chunk], 0.0)
    ws_all = jnp.transpose(wsm, (1, 0, 2)).reshape(chunk, N_HEADS_A * chunk)
    bs_tab = jnp.repeat(b_s[:, :chunk].T, HEAD_A, axis=1)
    return ws_all.astype(BF16), bs_tab.astype(F32)


def _layer_weights(l, p, chunk_p, chunk_s):
    half = QK_ROPE // 2
    w_in = p["w_in"][l]
    krw = w_in[:, 2176:2208]
    x1, x2 = krw[:, :half], krw[:, half:]
    z64 = jnp.zeros((D_MODEL, QK_NOPE), F32)
    z32 = jnp.zeros((D_MODEL, HEAD_LANES - QK_NOPE - QK_ROPE), F32)
    w_in2 = jnp.concatenate([w_in[:, :2176], z64, x1, x2, z32, z64, -x2, x1, z32, w_in[:, 2208:]], axis=1)

    wq = p["mla_w_uq"][l].reshape(Q_LORA, N_HEADS, QK_NOPE + QK_ROPE)
    qn, q1, q2 = wq[..., :QK_NOPE], wq[..., QK_NOPE:QK_NOPE + half], wq[..., QK_NOPE + half:]
    zq32 = jnp.zeros((Q_LORA, N_HEADS, HEAD_LANES - QK_NOPE - QK_ROPE), F32)
    zq64 = jnp.zeros((Q_LORA, N_HEADS, QK_NOPE), F32)
    w_uq = jnp.concatenate([
        jnp.concatenate([qn, q1, q2, zq32], axis=-1).reshape(Q_LORA, -1),
        jnp.concatenate([zq64, -q2, q1, zq32], axis=-1).reshape(Q_LORA, -1)], axis=1)

    wkv = p["mla_w_ukv"][l].reshape(KV_LORA, N_HEADS, QK_NOPE + V_DIM)
    wk, wv = wkv[..., :QK_NOPE], wkv[..., QK_NOPE:]
    w_uk = jnp.concatenate([wk, jnp.zeros_like(wk)], axis=-1).reshape(KV_LORA, -1)
    wvt = jnp.transpose(wv, (1, 2, 0))
    w_uvt = jnp.concatenate([wvt, jnp.zeros((N_HEADS, V_ROWS - V_DIM, KV_LORA), F32)], axis=1).reshape(-1, KV_LORA)
    wkt = jnp.transpose(wk, (1, 2, 0))
    wabs = jnp.concatenate([wkt, jnp.zeros((N_HEADS, HEAD_LANES - QK_NOPE, KV_LORA), F32)], axis=1).reshape(-1, KV_LORA)
    eye = jnp.eye(QK_ROPE, HEAD_LANES, dtype=F32)
    fold_h = jnp.concatenate([jnp.zeros((QK_NOPE, HEAD_LANES), F32), eye,
                              jnp.zeros((HEAD_LANES - QK_NOPE - QK_ROPE, HEAD_LANES), F32)], axis=0)
    fold = jnp.tile(fold_h, (N_HEADS, 1))

    zrow = jnp.zeros((1, W_B), F32)
    vec = jnp.stack([p["gmlp_ln_g"][l], p["gmlp_ln_b"][l], p["conv_dw_b"][l], p["conv_ln_g"][l],
                     p["conv_ln_b"][l], p["conv_b_pw"][l], p["mla_kv_norm"][l], zrow[0]], axis=0)
    common = dict(
        w_in=w_in2.astype(BF16), vec=vec, q_norm=p["mla_q_norm"][l].reshape(1, Q_LORA),
        dwk=jnp.concatenate([p["conv_dw_k"][l], zrow], axis=0), w_pw=p["conv_w_pw"][l].astype(BF16),
        w_uq=w_uq.astype(BF16), w_uk=w_uk.astype(BF16), w_uvt=w_uvt.astype(BF16),
        wabs=wabs.astype(BF16), fold=fold.astype(BF16), w_uv=wv.reshape(KV_LORA, W_C).astype(BF16),
        w_out=p["w_out"][l].astype(BF16), ln=jnp.stack([p["post_ln_g"][l], p["post_ln_b"][l]], axis=0))
    ws_p, bs_p = _gmlp_weights(p["gmlp_ws"][l], p["gmlp_bs"][l], chunk_p)
    ws_s, bs_s = _gmlp_weights(p["gmlp_ws"][l], p["gmlp_bs"][l], chunk_s)
    return dict(common, ws=ws_p, bs=bs_p), dict(common, ws=ws_s, bs=bs_s)


def kernel(x_prompt, x_sample, cache_latent, cache_krope, state_conv, c_prompt, c_sample,
           w_ada, b_ada, w_in, gmlp_ln_g, gmlp_ln_b, gmlp_ws, gmlp_bs,
           conv_dw_k, conv_dw_b, conv_ln_g, conv_ln_b, conv_w_pw, conv_b_pw,
           mla_q_norm, mla_w_uq, mla_kv_norm, mla_w_ukv, w_out, post_ln_g, post_ln_b):
    p = dict(w_in=w_in, gmlp_ln_g=gmlp_ln_g, gmlp_ln_b=gmlp_ln_b, gmlp_ws=gmlp_ws, gmlp_bs=gmlp_bs,
             conv_dw_k=conv_dw_k, conv_dw_b=conv_dw_b, conv_ln_g=conv_ln_g, conv_ln_b=conv_ln_b,
             conv_w_pw=conv_w_pw, conv_b_pw=conv_b_pw, mla_q_norm=mla_q_norm, mla_w_uq=mla_w_uq,
             mla_kv_norm=mla_kv_norm, mla_w_ukv=mla_w_ukv, w_out=w_out, post_ln_g=post_ln_g, post_ln_b=post_ln_b)
    depth = w_ada.shape[0]
    bp, seq, _ = x_prompt.shape
    bs, t_new, _ = x_sample.shape
    past_len = cache_latent.shape[2]
    alpha = (2 * depth) ** 0.25
    tile_p = min(IN_TILE, seq)
    chunk_p = min(seq, MLP_CHUNK)
    chunk_s = min(t_new, MLP_CHUNK)

    n_c = bp + bs
    c_all = jnp.concatenate([c_prompt, c_sample, jnp.zeros((-n_c % 8, D_MODEL), F32)], axis=0)
    mod_all = _ada(c_all, w_ada, b_ada)

    ct_p, st_p = _rope_tables(0, seq)
    ct_s, st_s = _rope_tables(past_len, t_new)
    zero_past = jnp.zeros((bp, HIST, W_B), F32)

    xp, xs = x_prompt, x_sample
    p_conv, s_conv, s_v = [], [], []
    p_stack = s_stack = None
    for l in range(depth):
        wp, wsm = _layer_weights(l, p, chunk_p, chunk_s)
        mod_p = mod_all[l, 0:bp].reshape(bp, 1, -1)
        mod_s = mod_all[l, bp:n_c].reshape(bs, 1, -1)

        q, lat, kr, yab, gc, conv, k, vt = _mixer_in(xp, mod_p, ct_p, st_p, zero_past, wp, tile=tile_p,
                                                      chunk=chunk_p, prompt=True, layer=l, depth=depth,
                                                      stacked=p_stack)
        p_stack = (lat, kr)
        o = _attn_prompt(q, k, vt)
        xp = _mixer_out(xp, mod_p, yab, o, gc, wp["w_out"], wp["ln"], tile=tile_p, alpha=alpha)
        p_conv.append(conv[:, HIST - (CONV_W - 1):])

        past = jnp.concatenate([jnp.zeros((bs, HIST - (CONV_W - 1), W_B), F32), state_conv[l]], axis=1)
        q, lat, kr, yab, gc, conv, vst = _mixer_in(xs, mod_s, ct_s, st_s, past, wsm, tile=t_new,
                                                   chunk=chunk_s, prompt=False, layer=l, depth=depth,
                                                   stacked=s_stack)
        s_stack = (lat, kr)
        o = _attn_sample(q, cache_latent, cache_krope, l, lat, kr, wsm)
        xs = _mixer_out(xs, mod_s, yab, o, gc, wsm["w_out"], wsm["ln"], tile=t_new, alpha=alpha)
        s_conv.append(conv[:, HIST - (CONV_W - 1):])
        s_v.append(vst)

    return (xp, xs, jnp.stack(p_conv), p_stack[0], p_stack[1],
            jnp.stack(s_conv), s_stack[0], s_stack[1], jnp.stack(s_v))
```

```python
import functools
import math

import jax
import jax.numpy as jnp
from jax import lax
from jax.experimental import pallas as pl
from jax.experimental.pallas import tpu as pltpu

F32 = jnp.float32
BF16 = jnp.bfloat16

D_MODEL = 1024
N_HEADS_A = 4
HEAD_A = 64
W_A = 256
W_B = 256
CONV_W = 31
N_HEADS = 8
QK_NOPE = 64
QK_ROPE = 32
V_DIM = 64
W_C = N_HEADS * V_DIM
Q_LORA = 384
KV_LORA = 256
CHUNK = 64
MLP_CHUNK = 128
ROPE_THETA = 10000.0
ATTN_SCALE = (QK_NOPE + QK_ROPE) ** -0.5
LN_EPS = 1e-5
RMS_EPS = 1e-6

HEAD_LANES = 128
V_ROWS = 80
HIST = 32
ATT_TILE = 256
Q_BLOCK_ROWS = 2048
IN_TILE = 1024
VMEM_LIMIT = 56 * 1024 * 1024

_C_A, _C_B, _C_Q, _C_KV, _C_KR, _C_GC, _C_END = 0, 768, 1536, 1920, 2176, 2432, 2944
_Q_SCALE = ATTN_SCALE * math.log2(math.e)


def _dot(a, b):
    return jnp.dot(a, b, preferred_element_type=F32)


def _dot_nt(a, b):
    return lax.dot_general(a, b, (((1,), (1,)), ((), ())), preferred_element_type=F32)


def _norm_rows(x, eps):
    mu = jnp.mean(x, axis=-1, keepdims=True)
    xc = x - mu
    var = jnp.mean(xc * xc, axis=-1, keepdims=True)
    return xc * lax.rsqrt(var + eps)


def _gelu_tanh(x):
    return x * (0.5 * (1.0 + jnp.tanh(math.sqrt(2.0 / math.pi) * (x + 0.044715 * (x * x * x)))))


def _silu(x):
    return x * jax.nn.sigmoid(x)


def _ada_kernel(c_ref, w_ref, b_ref, o_ref):
    c = c_ref[...]
    a = _silu(c)
    w = w_ref[0]
    a_hi = a.astype(BF16)
    a_lo = (a - a_hi.astype(F32)).astype(BF16)
    w_hi = w.astype(BF16)
    w_lo = (w - w_hi.astype(F32)).astype(BF16)
    o_ref[0] = _dot(a_hi, w_hi) + (_dot(a_hi, w_lo) + _dot(a_lo, w_hi)) + b_ref[0]


def _ada(c_all, w_ada, b_ada):
    depth = w_ada.shape[0]
    rows = c_all.shape[0]
    n_col = w_ada.shape[2] // D_MODEL
    return pl.pallas_call(
        _ada_kernel,
        grid=(depth, n_col),
        in_specs=[
            pl.BlockSpec((rows, D_MODEL), lambda l, j: (0, 0)),
            pl.BlockSpec((1, D_MODEL, D_MODEL), lambda l, j: (l, 0, j)),
            pl.BlockSpec((1, 1, D_MODEL), lambda l, j: (l, 0, j)),
        ],
        out_specs=pl.BlockSpec((1, rows, D_MODEL), lambda l, j: (l, 0, j)),
        out_shape=jax.ShapeDtypeStruct((depth, rows, w_ada.shape[2]), F32),
        compiler_params=pltpu.CompilerParams(vmem_limit_bytes=VMEM_LIMIT),
        name="ada_mod",
    )(c_all, w_ada, b_ada.reshape(depth, 1, -1))


def _mixer_in_kernel(x_ref, mod_ref, ct_ref, st_ref, past_ref, w_in_ref, vec_ref, qn_ref, ws_ref, bs_ref,
                     dwk_ref, w_pw_ref, w_uq_ref, *rest, tile, chunk, prompt, n_stacked):
    if prompt:
        w_uk_ref, w_uvt_ref = rest[:2]
        rest = rest[2:]
    rest = rest[n_stacked:]
    if prompt:
        (q_ref, lat_ref, kr_ref, yab_ref, gc_ref, conv_ref, k_ref, vt_ref, hist) = rest
    else:
        (q_ref, lat_ref, kr_ref, yab_ref, gc_ref, conv_ref, vst_ref, hist) = rest
    t = pl.program_id(1)

    mod = mod_ref[0]
    shift = mod[:, 0:D_MODEL]
    scale = mod[:, D_MODEL:2 * D_MODEL]
    h = _norm_rows(x_ref[0], LN_EPS) * (1.0 + scale) + shift
    hb = h.astype(BF16)

    vec = vec_ref[...]
    gmlp_g, gmlp_b = vec[0:1], vec[1:2]
    dw_b, conv_g, conv_b, b_pw, kv_norm = vec[2:3], vec[3:4], vec[4:5], vec[5:6], vec[6:7]

    za = _dot(hb, w_in_ref[:, _C_A:_C_B])
    u = _gelu_tanh(za[:, 0:W_A])
    v = _norm_rows(_gelu_tanh(za[:, W_A:2 * W_A]), LN_EPS) * gmlp_g + gmlp_b
    if not prompt:
        vst_ref[0] = v
    vb = v.astype(BF16)
    rows = lax.broadcasted_iota(jnp.int32, (N_HEADS_A * chunk, W_A), 0)
    cols = lax.broadcasted_iota(jnp.int32, (N_HEADS_A * chunk, W_A), 1)
    own_head = (rows // chunk) == (cols // HEAD_A)
    ws = ws_ref[...]
    bs = bs_ref[...]
    mixed = []
    for c in range(tile // chunk):
        vc = vb[c * chunk:(c + 1) * chunk]
        vbd = jnp.where(own_head, jnp.concatenate([vc] * N_HEADS_A, axis=0), jnp.zeros((), BF16))
        mixed.append(_dot(ws, vbd) + bs)
    s = mixed[0] if len(mixed) == 1 else jnp.concatenate(mixed, axis=0)
    y_a = u * s * _silu(za[:, 2 * W_A:3 * W_A])
    yab_ref[0, :, 0:W_A] = y_a.astype(BF16)

    zb = _dot(hb, w_in_ref[:, _C_B:_C_Q])
    g = zb[:, 0:W_B] * jax.nn.sigmoid(zb[:, W_B:2 * W_B])

    @pl.when(t == 0)
    def _():
        hist[0:HIST] = past_ref[0]

    hist[HIST:HIST + tile] = g
    dwk = dwk_ref[...]
    y = jnp.broadcast_to(dw_b, (tile, W_B))
    first = HIST - (CONV_W - 1)
    grouped = tile >= 8 * 8
    for r in range(8 if grouped else 0):
        rows = tile if r == 0 else tile + 8
        part = None
        for a in range(HIST // 8 + 1):
            k = 8 * a + r - first
            if 0 <= k < CONV_W:
                term = dwk[k:k + 1] * hist[8 * a:8 * a + rows]
                part = term if part is None else part + term
        y = y + (part if r == 0 else part[r:r + tile])
    for k in range(0 if grouped else CONV_W):
        y = y + dwk[k:k + 1] * hist[first + k:first + k + tile]
    new_hist = hist[tile:tile + HIST]
    hist[0:HIST] = new_hist
    conv_ref[0] = new_hist
    y = _silu(_norm_rows(y, LN_EPS) * conv_g + conv_b)
    y = _dot(y.astype(BF16), w_pw_ref[...]) + b_pw
    y_b = y * _silu(zb[:, 2 * W_B:3 * W_B])
    yab_ref[0, :, W_A:W_A + W_B] = y_b.astype(BF16)

    ct = ct_ref[...]
    st = st_ref[...]
    zq = _dot(hb, w_in_ref[:, _C_Q:_C_KV])
    cq = zq * lax.rsqrt(jnp.mean(zq * zq, axis=-1, keepdims=True) + RMS_EPS) * qn_ref[...]
    q2 = _dot(cq.astype(BF16), w_uq_ref[...])
    ctq = ct * _Q_SCALE
    stq = st * _Q_SCALE
    for hd in range(N_HEADS):
        lo = hd * HEAD_LANES
        qh = q2[:, lo:lo + HEAD_LANES] * ctq + q2[:, N_HEADS * HEAD_LANES + lo:N_HEADS * HEAD_LANES + lo + HEAD_LANES] * stq
        q_ref[0, :, lo:lo + HEAD_LANES] = qh.astype(BF16)

    zkv = _dot(hb, w_in_ref[:, _C_KV:_C_KR])
    lat = zkv * lax.rsqrt(jnp.mean(zkv * zkv, axis=-1, keepdims=True) + RMS_EPS) * kv_norm
    lat_ref[0, 0] = lat
    zkr = _dot(hb, w_in_ref[:, _C_KR:_C_GC])
    krot = zkr[:, 0:HEAD_LANES] * ct + zkr[:, HEAD_LANES:2 * HEAD_LANES] * st
    kr_ref[0, 0] = krot[:, QK_NOPE:QK_NOPE + QK_ROPE]

    if prompt:
        latb = lat.astype(BF16)
        kn = _dot(latb, w_uk_ref[...])
        for hd in range(N_HEADS):
            lo = hd * HEAD_LANES
            k_ref[0, :, lo:lo + HEAD_LANES] = (kn[:, lo:lo + HEAD_LANES] + krot).astype(BF16)
        vt = _dot_nt(w_uvt_ref[...], latb)
        vrow = lax.broadcasted_iota(jnp.int32, vt.shape, 0)
        vt = jnp.where((vrow % V_ROWS) == V_DIM, 1.0, vt)
        for c in range(tile // ATT_TILE):
            blk = vt[:, c * ATT_TILE:(c + 1) * ATT_TILE].reshape(N_HEADS, V_ROWS, ATT_TILE)
            vt_ref[0, :, c] = blk.astype(BF16)

    zg = _dot(hb, w_in_ref[:, _C_GC:_C_END])
    gc_ref[0] = _silu(zg).astype(BF16)


def _mixer_in(x, mod, ctab, stab, past, wts, *, tile, chunk, prompt, layer, depth, stacked=None):
    bsz, seq, _ = x.shape
    n_t = seq // tile
    kern = functools.partial(_mixer_in_kernel, tile=tile, chunk=chunk, prompt=prompt,
                             n_stacked=0 if stacked is None else len(stacked))

    def full(a):
        return pl.BlockSpec(a.shape, lambda b, t, _n=a.ndim: (0,) * _n)

    weights = [wts["w_in"], wts["vec"], wts["q_norm"], wts["ws"], wts["bs"], wts["dwk"], wts["w_pw"], wts["w_uq"]]
    if prompt:
        weights += [wts["w_uk"], wts["w_uvt"]]
    in_specs = [
        pl.BlockSpec((1, tile, D_MODEL), lambda b, t: (b, t, 0)),
        pl.BlockSpec((1, 1, 3 * D_MODEL), lambda b, t: (b, 0, 0)),
        pl.BlockSpec((tile, HEAD_LANES), lambda b, t: (t, 0)),
        pl.BlockSpec((tile, HEAD_LANES), lambda b, t: (t, 0)),
        pl.BlockSpec((1, HIST, W_B), lambda b, t: (b, 0, 0)),
    ] + [full(w) for w in weights]
    operands = [x, mod, ctab, stab, past, *weights]
    aliases = {}
    if stacked is not None:
        for i, a in enumerate(stacked):
            aliases[len(operands)] = 1 + i
            in_specs.append(pl.BlockSpec(memory_space=pl.ANY))
            operands.append(a)
    tok = lambda width: pl.BlockSpec((1, tile, width), lambda b, t: (b, t, 0))
    lay = lambda width: pl.BlockSpec((1, 1, tile, width), lambda b, t: (layer, b, t, 0))
    out_specs = [tok(N_HEADS * HEAD_LANES), lay(KV_LORA), lay(QK_ROPE), tok(W_A + W_B), tok(W_C),
                 pl.BlockSpec((1, HIST, W_B), lambda b, t: (b, 0, 0))]
    out_shape = [
        jax.ShapeDtypeStruct((bsz, seq, N_HEADS * HEAD_LANES), BF16),
        jax.ShapeDtypeStruct((depth, bsz, seq, KV_LORA), F32),
        jax.ShapeDtypeStruct((depth, bsz, seq, QK_ROPE), F32),
        jax.ShapeDtypeStruct((bsz, seq, W_A + W_B), BF16),
        jax.ShapeDtypeStruct((bsz, seq, W_C), BF16),
        jax.ShapeDtypeStruct((bsz, HIST, W_B), F32),
    ]
    if prompt:
        n_kt = tile // ATT_TILE
        out_specs += [tok(N_HEADS * HEAD_LANES),
                      pl.BlockSpec((1, N_HEADS, n_kt, V_ROWS, ATT_TILE), lambda b, t: (b, 0, t, 0, 0))]
        out_shape += [jax.ShapeDtypeStruct((bsz, seq, N_HEADS * HEAD_LANES), BF16),
                      jax.ShapeDtypeStruct((bsz, N_HEADS, seq // ATT_TILE, V_ROWS, ATT_TILE), BF16)]
    else:
        out_specs += [tok(W_A)]
        out_shape += [jax.ShapeDtypeStruct((bsz, seq, W_A), F32)]
    return pl.pallas_call(
        kern,
        grid=(bsz, n_t),
        in_specs=in_specs,
        out_specs=out_specs,
        out_shape=out_shape,
        scratch_shapes=[pltpu.VMEM((HIST + tile, W_B), F32)],
        input_output_aliases=aliases,
        compiler_params=pltpu.CompilerParams(dimension_semantics=("arbitrary", "arbitrary"),
                                             vmem_limit_bytes=VMEM_LIMIT),
        name="mixer_in_prompt" if prompt else "mixer_in_sample",
    )(*operands)


_S_ADDR = (0, 64)
_O_ADDR = ((128, 148), (168, 188))
_FIXED_TILES = 5


def _attn_prompt_kernel(q_ref, k_ref, vt_ref, bias_ref, o_ref, qt_sc, s_sc, acc_sc, *, n_sub, n_kt):
    qi = pl.program_id(2)

    def tile_index(t):
        return jnp.minimum(t, n_kt - 1)

    class SubTile:
        def __init__(self, sub):
            self.sp = sub % 2
            self.row0 = sub * ATT_TILE
            self.g = qi * n_sub + sub
            self.n_steady = jnp.maximum((self.g - 3) // 2, 0)
            self.j0 = _FIXED_TILES + 2 * self.n_steady - 3
            q_t = q_ref[0, self.row0:self.row0 + ATT_TILE, :].astype(F32).T
            zero = jnp.zeros((HEAD_LANES, ATT_TILE), BF16)
            for hd in range(2):
                own = slice(hd * HEAD_LANES, (hd + 1) * HEAD_LANES)
                other = slice((1 - hd) * HEAD_LANES, (2 - hd) * HEAD_LANES)
                qt_sc[self.sp, hd, own] = q_t[own].astype(BF16)
                qt_sc[self.sp, hd, other] = zero
            acc_sc[self.sp] = jnp.zeros(acc_sc.shape[1:], F32)
            neg = jnp.full((1, ATT_TILE), -jnp.inf, F32)
            one = jnp.ones((1, ATT_TILE), F32)
            self.state = ([neg, neg], [one, one], [one, one], [one, one])

        def stage_qt(self):
            for hd in range(2):
                pltpu.matmul_push_rhs(qt_sc[self.sp, hd], staging_register=0, mxu_index=hd)

        def stage_a(self, t, par, restage):
            k_tile = k_ref[0, pl.ds(pl.multiple_of(tile_index(t) * ATT_TILE, ATT_TILE), ATT_TILE), :]
            for hd in range(2):
                pltpu.matmul_acc_lhs(_S_ADDR[par], k_tile, mxu_index=hd, load_staged_rhs=0)
                if restage:
                    pltpu.matmul_push_rhs(qt_sc[self.sp, hd], staging_register=0, mxu_index=hd)

        def stage_b(self, t, par, masked, m_old=None):
            m_old = self.state[0] if m_old is None else m_old
            m_new, alpha = [], []
            if masked:
                bias = bias_ref[jnp.where(t < self.g, 0, jnp.where(t == self.g, 1, 2))]
            for hd in range(2):
                s = pltpu.matmul_pop(_S_ADDR[par], (ATT_TILE, ATT_TILE), F32, mxu_index=hd)
                if masked:
                    s = s + bias
                m_hd = jnp.maximum(m_old[hd], jnp.max(s, axis=0, keepdims=True))
                alpha.append(jnp.exp2(m_old[hd] - m_hd))
                m_new.append(m_hd)
                s_sc[self.sp, hd, par] = s
            return m_new, alpha

        def stage_c(self, t, par, m_t=None):
            m_t = self.state[0] if m_t is None else m_t
            for hd in range(2):
                p = jnp.exp2(s_sc[self.sp, hd, par] - m_t[hd]).astype(BF16)
                pltpu.matmul_push_rhs(p, staging_register=1, mxu_index=hd)
                pltpu.matmul_acc_lhs(_O_ADDR[self.sp][par], vt_ref[0, hd, tile_index(t)], mxu_index=hd,
                                     load_staged_rhs=1)

        def stage_d(self, par, alpha_t=None):
            alpha_t = self.state[3] if alpha_t is None else alpha_t
            for hd in range(2):
                o = pltpu.matmul_pop(_O_ADDR[self.sp][par], (V_ROWS, ATT_TILE), F32, mxu_index=hd)
                acc_sc[self.sp, hd] = alpha_t[hd] * acc_sc[self.sp, hd] + o

        def finish(self):
            outs = []
            for hd in range(2):
                acc = acc_sc[self.sp, hd]
                outs.append(acc[0:V_DIM] * (1.0 / acc[V_DIM:V_DIM + 1]))
            o_t = jnp.concatenate(outs, axis=0)
            o_ref[0, self.row0:self.row0 + ATT_TILE, :] = o_t.T.astype(BF16)

    def run(steps):
        for st, j, par, f in steps:
            if f.get("d", True):
                st.stage_d(par)
        new = []
        for st, j, par, f in steps:
            m, al1, al2, al3 = st.state
            if f.get("b", True):
                m_next, al0 = st.stage_b(j + 1, 1 - par, f.get("masked", False))
            else:
                m_next, al0 = m, al1
            new.append((m_next, al0, al1, al2))
        for st, j, par, f in steps:
            if f.get("a", True):
                st.stage_a(j + 2, par, f.get("restage", True))
            if f.get("qt_of") is not None:
                f["qt_of"].stage_qt()
        for st, j, par, f in steps:
            if f.get("c", True):
                st.stage_c(j, par)
        for (st, j, par, f), state in zip(steps, new):
            st.state = state

    fill = [dict(b=False, c=False, d=False), dict(c=False, d=False, masked=True),
            dict(d=False, masked=True), dict(d=False, masked=True)]

    def drain(nxt):
        return [dict(masked=True, restage=False, qt_of=nxt), dict(a=False, masked=True),
                dict(a=False, b=False), dict(a=False, b=False, c=False), dict(a=False, b=False, c=False)]

    cur = SubTile(0)
    cur.stage_qt()
    for i in range(4):
        run([(cur, i - 2, i % 2, fill[i])])
    for sub in range(n_sub):
        def steady(state, j, n_iter, st=cur):
            st.state = state
            for i in range(n_iter):
                run([(st, j + i, i % 2, {})])
            return st.state

        octs = cur.n_steady // 4
        rest = cur.n_steady - 4 * octs
        cur.state = lax.fori_loop(0, octs, lambda u, s: steady(s, 2 + 8 * u, 8), cur.state)
        j_q = 2 + 8 * octs
        cur.state = lax.cond(rest >= 2, lambda s: steady(s, j_q, 4), lambda s: s, cur.state)
        j_p = j_q + jnp.where(rest >= 2, 4, 0)
        cur.state = lax.cond(rest % 2 == 1, lambda s: steady(s, j_p, 2), lambda s: s, cur.state)
        nxt = SubTile(sub + 1) if sub + 1 < n_sub else None
        dr = drain(nxt)
        run([(cur, cur.j0, 0, dr[0])])
        for i in range(4):
            steps = [(cur, cur.j0 + 1 + i, (i + 1) % 2, dr[i + 1])]
            if nxt is not None:
                steps.append((nxt, i - 2, i % 2, fill[i]))
            run(steps)
        cur.finish()
        cur = nxt


def _attn_prompt(q, k, vt):
    bsz, seq, _ = q.shape
    qb = min(Q_BLOCK_ROWS, seq)
    n_sub = qb // ATT_TILE
    n_kt = seq // ATT_TILE
    kern = functools.partial(_attn_prompt_kernel, n_sub=n_sub, n_kt=n_kt)
    idx = jnp.arange(ATT_TILE) // CHUNK
    diag = jnp.where(idx[:, None] <= idx[None, :], 0.0, -jnp.inf).astype(F32)
    bias = jnp.stack([jnp.zeros_like(diag), diag, jnp.full_like(diag, -jnp.inf)])
    return pl.pallas_call(
        kern,
        grid=(bsz, N_HEADS // 2, seq // qb),
        in_specs=[
            pl.BlockSpec((1, qb, 2 * HEAD_LANES), lambda b, hp, i: (b, i, hp)),
            pl.BlockSpec((1, seq, 2 * HEAD_LANES), lambda b, hp, i: (b, 0, hp)),
            pl.BlockSpec((1, 2, n_kt, V_ROWS, ATT_TILE), lambda b, hp, i: (b, hp, 0, 0, 0)),
            pl.BlockSpec((3, ATT_TILE, ATT_TILE), lambda b, hp, i: (0, 0, 0)),
        ],
        out_specs=pl.BlockSpec((1, qb, 2 * V_DIM), lambda b, hp, i: (b, i, hp)),
        out_shape=jax.ShapeDtypeStruct((bsz, seq, W_C), BF16),
        scratch_shapes=[pltpu.VMEM((2, 2, 2 * HEAD_LANES, ATT_TILE), BF16),
                        pltpu.VMEM((2, 2, 2, ATT_TILE, ATT_TILE), F32),
                        pltpu.VMEM((2, 2, V_ROWS, ATT_TILE), F32)],
        compiler_params=pltpu.CompilerParams(dimension_semantics=("arbitrary", "arbitrary", "arbitrary"),
                                             vmem_limit_bytes=VMEM_LIMIT),
        name="attn_prompt",
    )(q, k, vt, bias)


def _attn_sample_kernel(q_ref, latc_ref, krc_ref, latn_ref, krn_ref, wabs_ref, fold_ref, wuv_ref, o_ref, *, t_new):
    q = q_ref[0]
    qrep = jnp.concatenate([q] * N_HEADS, axis=0)
    r1 = lax.broadcasted_iota(jnp.int32, qrep.shape, 0)
    c1 = lax.broadcasted_iota(jnp.int32, qrep.shape, 1)
    qrep = jnp.where((r1 // t_new) == (c1 // HEAD_LANES), qrep, jnp.zeros((), BF16))
    qlat = _dot(qrep, wabs_ref[...]).astype(BF16)
    qr = _dot(qrep, fold_ref[...])[:, 0:QK_ROPE].astype(BF16)
    latc = latc_ref[0, 0].astype(BF16)
    krc = krc_ref[0, 0].astype(BF16)
    latn = latn_ref[0, 0].astype(BF16)
    krn = krn_ref[0, 0].astype(BF16)
    s_c = _dot_nt(qlat, latc) + _dot_nt(qr, krc)
    s_n = _dot_nt(qlat, latn) + _dot_nt(qr, krn)
    m = jnp.maximum(jnp.max(s_c, axis=-1, keepdims=True), jnp.max(s_n, axis=-1, keepdims=True))
    p_c = jnp.exp2(s_c - m)
    p_n = jnp.exp2(s_n - m)
    den = jnp.sum(p_c, axis=-1, keepdims=True) + jnp.sum(p_n, axis=-1, keepdims=True)
    olat = (_dot(p_c.astype(BF16), latc) + _dot(p_n.astype(BF16), latn)) * (1.0 / den)
    of = _dot(olat.astype(BF16), wuv_ref[...])
    r2 = lax.broadcasted_iota(jnp.int32, of.shape, 0)
    c2 = lax.broadcasted_iota(jnp.int32, of.shape, 1)
    of = jnp.where((r2 // t_new) == (c2 // V_DIM), of, 0.0)
    o = of[0:t_new]
    for hd in range(1, N_HEADS):
        o = o + of[hd * t_new:(hd + 1) * t_new]
    o_ref[0] = o.astype(BF16)


def _attn_sample(q, lat_cache, kr_cache, layer, lat_new, kr_new, wts):
    bsz, t_new, _ = q.shape
    past = lat_cache.shape[2]
    kern = functools.partial(_attn_sample_kernel, t_new=t_new)

    def full(a):
        return pl.BlockSpec(a.shape, lambda b, _n=a.ndim: (0,) * _n)

    return pl.pallas_call(
        kern,
        grid=(bsz,),
        in_specs=[
            pl.BlockSpec((1, t_new, N_HEADS * HEAD_LANES), lambda b: (b, 0, 0)),
            pl.BlockSpec((1, 1, past, KV_LORA), lambda b: (layer, b, 0, 0)),
            pl.BlockSpec((1, 1, past, QK_ROPE), lambda b: (layer, b, 0, 0)),
            pl.BlockSpec((1, 1, t_new, KV_LORA), lambda b: (layer, b, 0, 0)),
            pl.BlockSpec((1, 1, t_new, QK_ROPE), lambda b: (layer, b, 0, 0)),
            full(wts["wabs"]), full(wts["fold"]), full(wts["w_uv"]),
        ],
        out_specs=pl.BlockSpec((1, t_new, W_C), lambda b: (b, 0, 0)),
        out_shape=jax.ShapeDtypeStruct((bsz, t_new, W_C), BF16),
        compiler_params=pltpu.CompilerParams(dimension_semantics=("arbitrary",), vmem_limit_bytes=VMEM_LIMIT),
        name="attn_sample",
    )(q, lat_cache, kr_cache, lat_new, kr_new, wts["wabs"], wts["fold"], wts["w_uv"])


def _mixer_out_kernel(x_ref, mod_ref, yab_ref, o_ref, gc_ref, w_out_ref, ln_ref, out_ref, *, alpha):
    gate = mod_ref[0][:, 2 * D_MODEL:3 * D_MODEL]
    yc = (o_ref[0].astype(F32) * gc_ref[0].astype(F32)).astype(BF16)
    y = _dot(yab_ref[0], w_out_ref[0:W_A + W_B]) + _dot(yc, w_out_ref[W_A + W_B:W_A + W_B + W_C])
    r = alpha * x_ref[0] + gate * y
    ln = ln_ref[...]
    out_ref[0] = _norm_rows(r, LN_EPS) * ln[0:1] + ln[1:2]


def _mixer_out(x, mod, yab, o, gc, w_out, ln, *, tile, alpha):
    bsz, seq, _ = x.shape
    tok = lambda width: pl.BlockSpec((1, tile, width), lambda b, t: (b, t, 0))
    return pl.pallas_call(
        functools.partial(_mixer_out_kernel, alpha=alpha),
        grid=(bsz, seq // tile),
        in_specs=[tok(D_MODEL), pl.BlockSpec((1, 1, 3 * D_MODEL), lambda b, t: (b, 0, 0)),
                  tok(W_A + W_B), tok(W_C), tok(W_C),
                  pl.BlockSpec(w_out.shape, lambda b, t: (0, 0)),
                  pl.BlockSpec(ln.shape, lambda b, t: (0, 0))],
        out_specs=tok(D_MODEL),
        out_shape=jax.ShapeDtypeStruct(x.shape, F32),
        compiler_params=pltpu.CompilerParams(dimension_semantics=("arbitrary", "arbitrary"),
                                             vmem_limit_bytes=VMEM_LIMIT),
        name="mixer_out",
    )(x, mod, yab, o, gc, w_out, ln)


def _rope_tables(pos0, n):
    half = QK_ROPE // 2
    inv = ROPE_THETA ** (-jnp.arange(0, QK_ROPE, 2, dtype=F32) / QK_ROPE)
    per_row = HEAD_LANES // half
    lane = jnp.arange(HEAD_LANES, dtype=jnp.int32)
    pos = pos0 + jnp.arange(n // per_row, dtype=jnp.int32)[:, None] * per_row + (lane // half)[None, :]
    ang = pos.astype(F32) * inv[lane % half][None, :]
    cos, sin = lax.optimization_barrier((jnp.cos(ang), jnp.sin(ang)))
    cos, sin = cos.reshape(n, half), sin.reshape(n, half)
    ones = jnp.ones((n, QK_NOPE), F32)
    z64 = jnp.zeros((n, QK_NOPE), F32)
    z32 = jnp.zeros((n, HEAD_LANES - QK_NOPE - QK_ROPE), F32)
    return jnp.concatenate([ones, cos, cos, z32], axis=1), jnp.concatenate([z64, sin, sin, z32], axis=1)


def _gmlp_weights(ws, b_s, chunk):
    idx = jnp.arange(chunk)
    mask = (idx[None, :] // CHUNK) <= (idx[:, None] // CHUNK)
    wsm = jnp.where(mask[None], ws[:, :chunk, :chunk], 0.0)
    ws_all = jnp.transpose(wsm, (1, 0, 2)).reshape(chunk, N_HEADS_A * chunk)
    bs_tab = jnp.repeat(b_s[:, :chunk].T, HEAD_A, axis=1)
    return ws_all.astype(BF16), bs_tab.astype(F32)


def _layer_weights(l, p, chunk_p, chunk_s):
    half = QK_ROPE // 2
    w_in = p["w_in"][l]
    krw = w_in[:, 2176:2208]
    x1, x2 = krw[:, :half], krw[:, half:]
    z64 = jnp.zeros((D_MODEL, QK_NOPE), F32)
    z32 = jnp.zeros((D_MODEL, HEAD_LANES - QK_NOPE - QK_ROPE), F32)
    w_in2 = jnp.concatenate([w_in[:, :2176], z64, x1, x2, z32, z64, -x2, x1, z32, w_in[:, 2208:]], axis=1)

    wq = p["mla_w_uq"][l].reshape(Q_LORA, N_HEADS, QK_NOPE + QK_ROPE)
    qn, q1, q2 = wq[..., :QK_NOPE], wq[..., QK_NOPE:QK_NOPE + half], wq[..., QK_NOPE + half:]
    zq32 = jnp.zeros((Q_LORA, N_HEADS, HEAD_LANES - QK_NOPE - QK_ROPE), F32)
    zq64 = jnp.zeros((Q_LORA, N_HEADS, QK_NOPE), F32)
    w_uq = jnp.concatenate([
        jnp.concatenate([qn, q1, q2, zq32], axis=-1).reshape(Q_LORA, -1),
        jnp.concatenate([zq64, -q2, q1, zq32], axis=-1).reshape(Q_LORA, -1)], axis=1)

    wkv = p["mla_w_ukv"][l].reshape(KV_LORA, N_HEADS, QK_NOPE + V_DIM)
    wk, wv = wkv[..., :QK_NOPE], wkv[..., QK_NOPE:]
    w_uk = jnp.concatenate([wk, jnp.zeros_like(wk)], axis=-1).reshape(KV_LORA, -1)
    wvt = jnp.transpose(wv, (1, 2, 0))
    w_uvt = jnp.concatenate([wvt, jnp.zeros((N_HEADS, V_ROWS - V_DIM, KV_LORA), F32)], axis=1).reshape(-1, KV_LORA)
    wkt = jnp.transpose(wk, (1, 2, 0))
    wabs = jnp.concatenate([wkt, jnp.zeros((N_HEADS, HEAD_LANES - QK_NOPE, KV_LORA), F32)], axis=1).reshape(-1, KV_LORA)
    eye = jnp.eye(QK_ROPE, HEAD_LANES, dtype=F32)
    fold_h = jnp.concatenate([jnp.zeros((QK_NOPE, HEAD_LANES), F32), eye,
                              jnp.zeros((HEAD_LANES - QK_NOPE - QK_ROPE, HEAD_LANES), F32)], axis=0)
    fold = jnp.tile(fold_h, (N_HEADS, 1))

    zrow = jnp.zeros((1, W_B), F32)
    vec = jnp.stack([p["gmlp_ln_g"][l], p["gmlp_ln_b"][l], p["conv_dw_b"][l], p["conv_ln_g"][l],
                     p["conv_ln_b"][l], p["conv_b_pw"][l], p["mla_kv_norm"][l], zrow[0]], axis=0)
    common = dict(
        w_in=w_in2.astype(BF16), vec=vec, q_norm=p["mla_q_norm"][l].reshape(1, Q_LORA),
        dwk=jnp.concatenate([p["conv_dw_k"][l], zrow], axis=0), w_pw=p["conv_w_pw"][l].astype(BF16),
        w_uq=w_uq.astype(BF16), w_uk=w_uk.astype(BF16), w_uvt=w_uvt.astype(BF16),
        wabs=wabs.astype(BF16), fold=fold.astype(BF16), w_uv=wv.reshape(KV_LORA, W_C).astype(BF16),
        w_out=p["w_out"][l].astype(BF16), ln=jnp.stack([p["post_ln_g"][l], p["post_ln_b"][l]], axis=0))
    ws_p, bs_p = _gmlp_weights(p["gmlp_ws"][l], p["gmlp_bs"][l], chunk_p)
    ws_s, bs_s = _gmlp_weights(p["gmlp_ws"][l], p["gmlp_bs"][l], chunk_s)
    return dict(common, ws=ws_p, bs=bs_p), dict(common, ws=ws_s, bs=bs_s)


def kernel(x_prompt, x_sample, cache_latent, cache_krope, state_conv, c_prompt, c_sample,
           w_ada, b_ada, w_in, gmlp_ln_g, gmlp_ln_b, gmlp_ws, gmlp_bs,
           conv_dw_k, conv_dw_b, conv_ln_g, conv_ln_b, conv_w_pw, conv_b_pw,
           mla_q_norm, mla_w_uq, mla_kv_norm, mla_w_ukv, w_out, post_ln_g, post_ln_b):
    p = dict(w_in=w_in, gmlp_ln_g=gmlp_ln_g, gmlp_ln_b=gmlp_ln_b, gmlp_ws=gmlp_ws, gmlp_bs=gmlp_bs,
             conv_dw_k=conv_dw_k, conv_dw_b=conv_dw_b, conv_ln_g=conv_ln_g, conv_ln_b=conv_ln_b,
             conv_w_pw=conv_w_pw, conv_b_pw=conv_b_pw, mla_q_norm=mla_q_norm, mla_w_uq=mla_w_uq,
             mla_kv_norm=mla_kv_norm, mla_w_ukv=mla_w_ukv, w_out=w_out, post_ln_g=post_ln_g, post_ln_b=post_ln_b)
    depth = w_ada.shape[0]
    bp, seq, _ = x_prompt.shape
    bs, t_new, _ = x_sample.shape
    past_len = cache_latent.shape[2]
    alpha = (2 * depth) ** 0.25
    tile_p = min(IN_TILE, seq)
    chunk_p = min(seq, MLP_CHUNK)
    chunk_s = min(t_new, MLP_CHUNK)

    n_c = bp + bs
    c_all = jnp.concatenate([c_prompt, c_sample, jnp.zeros((-n_c % 8, D_MODEL), F32)], axis=0)
    mod_all = _ada(c_all, w_ada, b_ada)

    ct_p, st_p = _rope_tables(0, seq)
    ct_s, st_s = _rope_tables(past_len, t_new)
    zero_past = jnp.zeros((bp, HIST, W_B), F32)

    xp, xs = x_prompt, x_sample
    p_conv, s_conv, s_v = [], [], []
    p_stack = s_stack = None
    for l in range(depth):
        wp, wsm = _layer_weights(l, p, chunk_p, chunk_s)
        mod_p = mod_all[l, 0:bp].reshape(bp, 1, -1)
        mod_s = mod_all[l, bp:n_c].reshape(bs, 1, -1)

        q, lat, kr, yab, gc, conv, k, vt = _mixer_in(xp, mod_p, ct_p, st_p, zero_past, wp, tile=tile_p,
                                                      chunk=chunk_p, prompt=True, layer=l, depth=depth,
                                                      stacked=p_stack)
        p_stack = (lat, kr)
        o = _attn_prompt(q, k, vt)
        xp = _mixer_out(xp, mod_p, yab, o, gc, wp["w_out"], wp["ln"], tile=tile_p, alpha=alpha)
        p_conv.append(conv[:, HIST - (CONV_W - 1):])

        past = jnp.concatenate([jnp.zeros((bs, HIST - (CONV_W - 1), W_B), F32), state_conv[l]], axis=1)
        q, lat, kr, yab, gc, conv, vst = _mixer_in(xs, mod_s, ct_s, st_s, past, wsm, tile=t_new,
                                                   chunk=chunk_s, prompt=False, layer=l, depth=depth,
                                                   stacked=s_stack)
        s_stack = (lat, kr)
        o = _attn_sample(q, cache_latent, cache_krope, l, lat, kr, wsm)
        xs = _mixer_out(xs, mod_s, yab, o, gc, wsm["w_out"], wsm["ln"], tile=t_new, alpha=alpha)
        s_conv.append(conv[:, HIST - (CONV_W - 1):])
        s_v.append(vst)

    return (xp, xs, jnp.stack(p_conv), p_stack[0], p_stack[1],
            jnp.stack(s_conv), s_stack[0], s_stack[1], jnp.stack(s_v))
```

```python
import functools
import math

import jax
import jax.numpy as jnp
from jax import lax
from jax.experimental import pallas as pl
from jax.experimental.pallas import tpu as pltpu

F32 = jnp.float32
BF16 = jnp.bfloat16

D_MODEL = 1024
N_HEADS_A = 4
HEAD_A = 64
W_A = 256
W_B = 256
CONV_W = 31
N_HEADS = 8
QK_NOPE = 64
QK_ROPE = 32
V_DIM = 64
W_C = N_HEADS * V_DIM
Q_LORA = 384
KV_LORA = 256
CHUNK = 64
MLP_CHUNK = 128
ROPE_THETA = 10000.0
ATTN_SCALE = (QK_NOPE + QK_ROPE) ** -0.5
LN_EPS = 1e-5
RMS_EPS = 1e-6

HEAD_LANES = 128
V_ROWS = 80
HIST = 32
ATT_TILE = 256
Q_BLOCK_ROWS = 1024
IN_TILE = 1024
VMEM_LIMIT = 56 * 1024 * 1024

_C_A, _C_B, _C_Q, _C_KV, _C_KR, _C_GC, _C_END = 0, 768, 1536, 1920, 2176, 2432, 2944
_Q_SCALE = ATTN_SCALE * math.log2(math.e)


def _dot(a, b):
    return jnp.dot(a, b, preferred_element_type=F32)


def _dot_nt(a, b):
    return lax.dot_general(a, b, (((1,), (1,)), ((), ())), preferred_element_type=F32)


def _norm_rows(x, eps):
    mu = jnp.mean(x, axis=-1, keepdims=True)
    xc = x - mu
    var = jnp.mean(xc * xc, axis=-1, keepdims=True)
    return xc * lax.rsqrt(var + eps)


def _gelu_tanh(x):
    return x * (0.5 * (1.0 + jnp.tanh(math.sqrt(2.0 / math.pi) * (x + 0.044715 * (x * x * x)))))


def _silu(x):
    return x * jax.nn.sigmoid(x)


def _ada_kernel(c_ref, w_ref, b_ref, o_ref):
    c = c_ref[...]
    a = _silu(c)
    w = w_ref[0]
    a_hi = a.astype(BF16)
    a_lo = (a - a_hi.astype(F32)).astype(BF16)
    w_hi = w.astype(BF16)
    w_lo = (w - w_hi.astype(F32)).astype(BF16)
    o_ref[0] = _dot(a_hi, w_hi) + (_dot(a_hi, w_lo) + _dot(a_lo, w_hi)) + b_ref[0]


def _ada(c_all, w_ada, b_ada):
    depth = w_ada.shape[0]
    rows = c_all.shape[0]
    n_col = w_ada.shape[2] // D_MODEL
    return pl.pallas_call(
        _ada_kernel,
        grid=(depth, n_col),
        in_specs=[
            pl.BlockSpec((rows, D_MODEL), lambda l, j: (0, 0)),
            pl.BlockSpec((1, D_MODEL, D_MODEL), lambda l, j: (l, 0, j)),
            pl.BlockSpec((1, 1, D_MODEL), lambda l, j: (l, 0, j)),
        ],
        out_specs=pl.BlockSpec((1, rows, D_MODEL), lambda l, j: (l, 0, j)),
        out_shape=jax.ShapeDtypeStruct((depth, rows, w_ada.shape[2]), F32),
        compiler_params=pltpu.CompilerParams(vmem_limit_bytes=VMEM_LIMIT),
        name="ada_mod",
    )(c_all, w_ada, b_ada.reshape(depth, 1, -1))


def _mixer_in_kernel(x_ref, mod_ref, ct_ref, st_ref, past_ref, w_in_ref, vec_ref, qn_ref, ws_ref, bs_ref,
                     dwk_ref, w_pw_ref, w_uq_ref, *rest, tile, chunk, prompt, n_stacked):
    if prompt:
        w_uk_ref, w_uvt_ref = rest[:2]
        rest = rest[2:]
    rest = rest[n_stacked:]
    if prompt:
        (q_ref, lat_ref, kr_ref, yab_ref, gc_ref, conv_ref, k_ref, vt_ref, hist) = rest
    else:
        (q_ref, lat_ref, kr_ref, yab_ref, gc_ref, conv_ref, vst_ref, hist) = rest
    t = pl.program_id(1)

    mod = mod_ref[0]
    shift = mod[:, 0:D_MODEL]
    scale = mod[:, D_MODEL:2 * D_MODEL]
    h = _norm_rows(x_ref[0], LN_EPS) * (1.0 + scale) + shift
    hb = h.astype(BF16)

    vec = vec_ref[...]
    gmlp_g, gmlp_b = vec[0:1], vec[1:2]
    dw_b, conv_g, conv_b, b_pw, kv_norm = vec[2:3], vec[3:4], vec[4:5], vec[5:6], vec[6:7]

    za = _dot(hb, w_in_ref[:, _C_A:_C_B])
    u = _gelu_tanh(za[:, 0:W_A])
    v = _norm_rows(_gelu_tanh(za[:, W_A:2 * W_A]), LN_EPS) * gmlp_g + gmlp_b
    if not prompt:
        vst_ref[0] = v
    vb = v.astype(BF16)
    rows = lax.broadcasted_iota(jnp.int32, (N_HEADS_A * chunk, W_A), 0)
    cols = lax.broadcasted_iota(jnp.int32, (N_HEADS_A * chunk, W_A), 1)
    own_head = (rows // chunk) == (cols // HEAD_A)
    ws = ws_ref[...]
    bs = bs_ref[...]
    mixed = []
    for c in range(tile // chunk):
        vc = vb[c * chunk:(c + 1) * chunk]
        vbd = jnp.where(own_head, jnp.concatenate([vc] * N_HEADS_A, axis=0), jnp.zeros((), BF16))
        mixed.append(_dot(ws, vbd) + bs)
    s = mixed[0] if len(mixed) == 1 else jnp.concatenate(mixed, axis=0)
    y_a = u * s * _silu(za[:, 2 * W_A:3 * W_A])
    yab_ref[0, :, 0:W_A] = y_a.astype(BF16)

    zb = _dot(hb, w_in_ref[:, _C_B:_C_Q])
    g = zb[:, 0:W_B] * jax.nn.sigmoid(zb[:, W_B:2 * W_B])

    @pl.when(t == 0)
    def _():
        hist[0:HIST] = past_ref[0]

    hist[HIST:HIST + tile] = g
    dwk = dwk_ref[...]
    y = jnp.broadcast_to(dw_b, (tile, W_B))
    first = HIST - (CONV_W - 1)
    grouped = tile >= 8 * 8
    for r in range(8 if grouped else 0):
        rows = tile if r == 0 else tile + 8
        part = None
        for a in range(HIST // 8 + 1):
            k = 8 * a + r - first
            if 0 <= k < CONV_W:
                term = dwk[k:k + 1] * hist[8 * a:8 * a + rows]
                part = term if part is None else part + term
        y = y + (part if r == 0 else part[r:r + tile])
    for k in range(0 if grouped else CONV_W):
        y = y + dwk[k:k + 1] * hist[first + k:first + k + tile]
    new_hist = hist[tile:tile + HIST]
    hist[0:HIST] = new_hist
    conv_ref[0] = new_hist
    y = _silu(_norm_rows(y, LN_EPS) * conv_g + conv_b)
    y = _dot(y.astype(BF16), w_pw_ref[...]) + b_pw
    y_b = y * _silu(zb[:, 2 * W_B:3 * W_B])
    yab_ref[0, :, W_A:W_A + W_B] = y_b.astype(BF16)

    ct = ct_ref[...]
    st = st_ref[...]
    zq = _dot(hb, w_in_ref[:, _C_Q:_C_KV])
    cq = zq * lax.rsqrt(jnp.mean(zq * zq, axis=-1, keepdims=True) + RMS_EPS) * qn_ref[...]
    q2 = _dot(cq.astype(BF16), w_uq_ref[...])
    ctq = ct * _Q_SCALE
    stq = st * _Q_SCALE
    for hd in range(N_HEADS):
        lo = hd * HEAD_LANES
        qh = q2[:, lo:lo + HEAD_LANES] * ctq + q2[:, N_HEADS * HEAD_LANES + lo:N_HEADS * HEAD_LANES + lo + HEAD_LANES] * stq
        q_ref[0, :, lo:lo + HEAD_LANES] = qh.astype(BF16)

    zkv = _dot(hb, w_in_ref[:, _C_KV:_C_KR])
    lat = zkv * lax.rsqrt(jnp.mean(zkv * zkv, axis=-1, keepdims=True) + RMS_EPS) * kv_norm
    lat_ref[0, 0] = lat
    zkr = _dot(hb, w_in_ref[:, _C_KR:_C_GC])
    krot = zkr[:, 0:HEAD_LANES] * ct + zkr[:, HEAD_LANES:2 * HEAD_LANES] * st
    kr_ref[0, 0] = krot[:, QK_NOPE:QK_NOPE + QK_ROPE]

    if prompt:
        latb = lat.astype(BF16)
        kn = _dot(latb, w_uk_ref[...])
        for hd in range(N_HEADS):
            lo = hd * HEAD_LANES
            k_ref[0, :, lo:lo + HEAD_LANES] = (kn[:, lo:lo + HEAD_LANES] + krot).astype(BF16)
        vt = _dot_nt(w_uvt_ref[...], latb)
        vrow = lax.broadcasted_iota(jnp.int32, vt.shape, 0)
        vt = jnp.where((vrow % V_ROWS) == V_DIM, 1.0, vt)
        for c in range(tile // ATT_TILE):
            blk = vt[:, c * ATT_TILE:(c + 1) * ATT_TILE].reshape(N_HEADS, V_ROWS, ATT_TILE)
            vt_ref[0, :, c] = blk.astype(BF16)

    zg = _dot(hb, w_in_ref[:, _C_GC:_C_END])
    gc_ref[0] = _silu(zg).astype(BF16)


def _mixer_in(x, mod, ctab, stab, past, wts, *, tile, chunk, prompt, layer, depth, stacked=None):
    bsz, seq, _ = x.shape
    n_t = seq // tile
    kern = functools.partial(_mixer_in_kernel, tile=tile, chunk=chunk, prompt=prompt,
                             n_stacked=0 if stacked is None else len(stacked))

    def full(a):
        return pl.BlockSpec(a.shape, lambda b, t, _n=a.ndim: (0,) * _n)

    weights = [wts["w_in"], wts["vec"], wts["q_norm"], wts["ws"], wts["bs"], wts["dwk"], wts["w_pw"], wts["w_uq"]]
    if prompt:
        weights += [wts["w_uk"], wts["w_uvt"]]
    in_specs = [
        pl.BlockSpec((1, tile, D_MODEL), lambda b, t: (b, t, 0)),
        pl.BlockSpec((1, 1, 3 * D_MODEL), lambda b, t: (b, 0, 0)),
        pl.BlockSpec((tile, HEAD_LANES), lambda b, t: (t, 0)),
        pl.BlockSpec((tile, HEAD_LANES), lambda b, t: (t, 0)),
        pl.BlockSpec((1, HIST, W_B), lambda b, t: (b, 0, 0)),
    ] + [full(w) for w in weights]
    operands = [x, mod, ctab, stab, past, *weights]
    aliases = {}
    if stacked is not None:
        for i, a in enumerate(stacked):
            aliases[len(operands)] = 1 + i
            in_specs.append(pl.BlockSpec(memory_space=pl.ANY))
            operands.append(a)
    tok = lambda width: pl.BlockSpec((1, tile, width), lambda b, t: (b, t, 0))
    lay = lambda width: pl.BlockSpec((1, 1, tile, width), lambda b, t: (layer, b, t, 0))
    out_specs = [tok(N_HEADS * HEAD_LANES), lay(KV_LORA), lay(QK_ROPE), tok(W_A + W_B), tok(W_C),
                 pl.BlockSpec((1, HIST, W_B), lambda b, t: (b, 0, 0))]
    out_shape = [
        jax.ShapeDtypeStruct((bsz, seq, N_HEADS * HEAD_LANES), BF16),
        jax.ShapeDtypeStruct((depth, bsz, seq, KV_LORA), F32),
        jax.ShapeDtypeStruct((depth, bsz, seq, QK_ROPE), F32),
        jax.ShapeDtypeStruct((bsz, seq, W_A + W_B), BF16),
        jax.ShapeDtypeStruct((bsz, seq, W_C), BF16),
        jax.ShapeDtypeStruct((bsz, HIST, W_B), F32),
    ]
    if prompt:
        n_kt = tile // ATT_TILE
        out_specs += [tok(N_HEADS * HEAD_LANES),
                      pl.BlockSpec((1, N_HEADS, n_kt, V_ROWS, ATT_TILE), lambda b, t: (b, 0, t, 0, 0))]
        out_shape += [jax.ShapeDtypeStruct((bsz, seq, N_HEADS * HEAD_LANES), BF16),
                      jax.ShapeDtypeStruct((bsz, N_HEADS, seq // ATT_TILE, V_ROWS, ATT_TILE), BF16)]
    else:
        out_specs += [tok(W_A)]
        out_shape += [jax.ShapeDtypeStruct((bsz, seq, W_A), F32)]
    return pl.pallas_call(
        kern,
        grid=(bsz, n_t),
        in_specs=in_specs,
        out_specs=out_specs,
        out_shape=out_shape,
        scratch_shapes=[pltpu.VMEM((HIST + tile, W_B), F32)],
        input_output_aliases=aliases,
        compiler_params=pltpu.CompilerParams(dimension_semantics=("arbitrary", "arbitrary"),
                                             vmem_limit_bytes=VMEM_LIMIT),
        name="mixer_in_prompt" if prompt else "mixer_in_sample",
    )(*operands)


_S_ADDR = (0, 64)
_O_ADDR = ((128, 148), (168, 188))
_FIXED_TILES = 5


def _attn_prompt_kernel(q_ref, k_ref, vt_ref, bias_ref, o_ref, qt_sc, s_sc, acc_sc, *, n_sub, n_kt):
    qi = pl.program_id(2)

    def tile_index(t):
        return jnp.minimum(t, n_kt - 1)

    class SubTile:
        def __init__(self, sub):
            self.sp = sub % 2
            self.row0 = sub * ATT_TILE
            self.g = qi * n_sub + sub
            self.n_steady = jnp.maximum((self.g - 3) // 2, 0)
            self.j0 = _FIXED_TILES + 2 * self.n_steady - 3
            q_t = q_ref[0, self.row0:self.row0 + ATT_TILE, :].astype(F32).T
            zero = jnp.zeros((HEAD_LANES, ATT_TILE), BF16)
            for hd in range(2):
                own = slice(hd * HEAD_LANES, (hd + 1) * HEAD_LANES)
                other = slice((1 - hd) * HEAD_LANES, (2 - hd) * HEAD_LANES)
                qt_sc[self.sp, hd, own] = q_t[own].astype(BF16)
                qt_sc[self.sp, hd, other] = zero
            acc_sc[self.sp] = jnp.zeros(acc_sc.shape[1:], F32)
            neg = jnp.full((1, ATT_TILE), -jnp.inf, F32)
            one = jnp.ones((1, ATT_TILE), F32)
            self.state = ([neg, neg], [one, one], [one, one], [one, one])

        def stage_qt(self):
            for hd in range(2):
                pltpu.matmul_push_rhs(qt_sc[self.sp, hd], staging_register=0, mxu_index=hd)

        def stage_a(self, t, par, restage):
            k_tile = k_ref[0, pl.ds(pl.multiple_of(tile_index(t) * ATT_TILE, ATT_TILE), ATT_TILE), :]
            for hd in range(2):
                pltpu.matmul_acc_lhs(_S_ADDR[par], k_tile, mxu_index=hd, load_staged_rhs=0)
                if restage:
                    pltpu.matmul_push_rhs(qt_sc[self.sp, hd], staging_register=0, mxu_index=hd)

        def stage_b(self, t, par, masked, m_old=None):
            m_old = self.state[0] if m_old is None else m_old
            m_new, alpha = [], []
            if masked:
                bias = bias_ref[jnp.where(t < self.g, 0, jnp.where(t == self.g, 1, 2))]
            for hd in range(2):
                s = pltpu.matmul_pop(_S_ADDR[par], (ATT_TILE, ATT_TILE), F32, mxu_index=hd)
                if masked:
                    s = s + bias
                m_hd = jnp.maximum(m_old[hd], jnp.max(s, axis=0, keepdims=True))
                alpha.append(jnp.exp2(m_old[hd] - m_hd))
                m_new.append(m_hd)
                s_sc[self.sp, hd, par] = s
            return m_new, alpha

        def stage_c(self, t, par, m_t=None):
            m_t = self.state[0] if m_t is None else m_t
            for hd in range(2):
                p = jnp.exp2(s_sc[self.sp, hd, par] - m_t[hd]).astype(BF16)
                pltpu.matmul_push_rhs(p, staging_register=1, mxu_index=hd)
                pltpu.matmul_acc_lhs(_O_ADDR[self.sp][par], vt_ref[0, hd, tile_index(t)], mxu_index=hd,
                                     load_staged_rhs=1)

        def stage_d(self, par, alpha_t=None):
            alpha_t = self.state[3] if alpha_t is None else alpha_t
            for hd in range(2):
                o = pltpu.matmul_pop(_O_ADDR[self.sp][par], (V_ROWS, ATT_TILE), F32, mxu_index=hd)
                acc_sc[self.sp, hd] = alpha_t[hd] * acc_sc[self.sp, hd] + o

        def finish(self):
            outs = []
            for hd in range(2):
                acc = acc_sc[self.sp, hd]
                outs.append(acc[0:V_DIM] * (1.0 / acc[V_DIM:V_DIM + 1]))
            o_t = jnp.concatenate(outs, axis=0)
            o_ref[0, self.row0:self.row0 + ATT_TILE, :] = o_t.T.astype(BF16)

    def run(steps):
        for st, j, par, f in steps:
            if f.get("d", True):
                st.stage_d(par)
        new = []
        for st, j, par, f in steps:
            m, al1, al2, al3 = st.state
            if f.get("b", True):
                m_next, al0 = st.stage_b(j + 1, 1 - par, f.get("masked", False))
            else:
                m_next, al0 = m, al1
            new.append((m_next, al0, al1, al2))
        for st, j, par, f in steps:
            if f.get("a", True):
                st.stage_a(j + 2, par, f.get("restage", True))
            if f.get("qt_of") is not None:
                f["qt_of"].stage_qt()
        for st, j, par, f in steps:
            if f.get("c", True):
                st.stage_c(j, par)
        for (st, j, par, f), state in zip(steps, new):
            st.state = state

    fill = [dict(b=False, c=False, d=False), dict(c=False, d=False, masked=True),
            dict(d=False, masked=True), dict(d=False, masked=True)]

    def drain(nxt):
        return [dict(masked=True, restage=False, qt_of=nxt), dict(a=False, masked=True),
                dict(a=False, b=False), dict(a=False, b=False, c=False), dict(a=False, b=False, c=False)]

    cur = SubTile(0)
    cur.stage_qt()
    for i in range(4):
        run([(cur, i - 2, i % 2, fill[i])])
    for sub in range(n_sub):
        def steady(state, j, n_iter, st=cur):
            st.state = state
            for i in range(n_iter):
                run([(st, j + i, i % 2, {})])
            return st.state

        big = cur.n_steady // 8
        cur.state = lax.fori_loop(0, big, lambda u, s: steady(s, 2 + 16 * u, 16), cur.state)
        j_next = 2 + 16 * big
        rest = cur.n_steady - 8 * big
        for units in (4, 2, 1):
            take = (rest // units) % 2 == 1
            cur.state = lax.cond(take, lambda s, j=j_next, n=2 * units: steady(s, j, n), lambda s: s, cur.state)
            j_next = j_next + jnp.where(take, 2 * units, 0)
        nxt = SubTile(sub + 1) if sub + 1 < n_sub else None
        dr = drain(nxt)
        run([(cur, cur.j0, 0, dr[0])])
        for i in range(4):
            steps = [(cur, cur.j0 + 1 + i, (i + 1) % 2, dr[i + 1])]
            if nxt is not None:
                steps.append((nxt, i - 2, i % 2, fill[i]))
            run(steps)
        cur.finish()
        cur = nxt


def _attn_prompt(q, k, vt):
    bsz, seq, _ = q.shape
    qb = min(Q_BLOCK_ROWS, seq)
    n_sub = qb // ATT_TILE
    n_kt = seq // ATT_TILE
    kern = functools.partial(_attn_prompt_kernel, n_sub=n_sub, n_kt=n_kt)
    idx = jnp.arange(ATT_TILE) // CHUNK
    diag = jnp.where(idx[:, None] <= idx[None, :], 0.0, -jnp.inf).astype(F32)
    bias = jnp.stack([jnp.zeros_like(diag), diag, jnp.full_like(diag, -jnp.inf)])
    return pl.pallas_call(
        kern,
        grid=(bsz, N_HEADS // 2, seq // qb),
        in_specs=[
            pl.BlockSpec((1, qb, 2 * HEAD_LANES), lambda b, hp, i: (b, i, hp)),
            pl.BlockSpec((1, seq, 2 * HEAD_LANES), lambda b, hp, i: (b, 0, hp)),
            pl.BlockSpec((1, 2, n_kt, V_ROWS, ATT_TILE), lambda b, hp, i: (b, hp, 0, 0, 0)),
            pl.BlockSpec((3, ATT_TILE, ATT_TILE), lambda b, hp, i: (0, 0, 0)),
        ],
        out_specs=pl.BlockSpec((1, qb, 2 * V_DIM), lambda b, hp, i: (b, i, hp)),
        out_shape=jax.ShapeDtypeStruct((bsz, seq, W_C), BF16),
        scratch_shapes=[pltpu.VMEM((2, 2, 2 * HEAD_LANES, ATT_TILE), BF16),
                        pltpu.VMEM((2, 2, 2, ATT_TILE, ATT_TILE), F32),
                        pltpu.VMEM((2, 2, V_ROWS, ATT_TILE), F32)],
        compiler_params=pltpu.CompilerParams(dimension_semantics=("arbitrary", "arbitrary", "arbitrary"),
                                             vmem_limit_bytes=VMEM_LIMIT),
        name="attn_prompt",
    )(q, k, vt, bias)


def _attn_sample_kernel(q_ref, latc_ref, krc_ref, latn_ref, krn_ref, wabs_ref, fold_ref, wuv_ref, o_ref, *, t_new):
    q = q_ref[0]
    qrep = jnp.concatenate([q] * N_HEADS, axis=0)
    r1 = lax.broadcasted_iota(jnp.int32, qrep.shape, 0)
    c1 = lax.broadcasted_iota(jnp.int32, qrep.shape, 1)
    qrep = jnp.where((r1 // t_new) == (c1 // HEAD_LANES), qrep, jnp.zeros((), BF16))
    qlat = _dot(qrep, wabs_ref[...]).astype(BF16)
    qr = _dot(qrep, fold_ref[...])[:, 0:QK_ROPE].astype(BF16)
    latc = latc_ref[0, 0].astype(BF16)
    krc = krc_ref[0, 0].astype(BF16)
    latn = latn_ref[0, 0].astype(BF16)
    krn = krn_ref[0, 0].astype(BF16)
    s_c = _dot_nt(qlat, latc) + _dot_nt(qr, krc)
    s_n = _dot_nt(qlat, latn) + _dot_nt(qr, krn)
    m = jnp.maximum(jnp.max(s_c, axis=-1, keepdims=True), jnp.max(s_n, axis=-1, keepdims=True))
    p_c = jnp.exp2(s_c - m)
    p_n = jnp.exp2(s_n - m)
    den = jnp.sum(p_c, axis=-1, keepdims=True) + jnp.sum(p_n, axis=-1, keepdims=True)
    olat = (_dot(p_c.astype(BF16), latc) + _dot(p_n.astype(BF16), latn)) * (1.0 / den)
    of = _dot(olat.astype(BF16), wuv_ref[...])
    r2 = lax.broadcasted_iota(jnp.int32, of.shape, 0)
    c2 = lax.broadcasted_iota(jnp.int32, of.shape, 1)
    of = jnp.where((r2 // t_new) == (c2 // V_DIM), of, 0.0)
    o = of[0:t_new]
    for hd in range(1, N_HEADS):
        o = o + of[hd * t_new:(hd + 1) * t_new]
    o_ref[0] = o.astype(BF16)


def _attn_sample(q, lat_cache, kr_cache, layer, lat_new, kr_new, wts):
    bsz, t_new, _ = q.shape
    past = lat_cache.shape[2]
    kern = functools.partial(_attn_sample_kernel, t_new=t_new)

    def full(a):
        return pl.BlockSpec(a.shape, lambda b, _n=a.ndim: (0,) * _n)

    return pl.pallas_call(
        kern,
        grid=(bsz,),
        in_specs=[
            pl.BlockSpec((1, t_new, N_HEADS * HEAD_LANES), lambda b: (b, 0, 0)),
            pl.BlockSpec((1, 1, past, KV_LORA), lambda b: (layer, b, 0, 0)),
            pl.BlockSpec((1, 1, past, QK_ROPE), lambda b: (layer, b, 0, 0)),
            pl.BlockSpec((1, 1, t_new, KV_LORA), lambda b: (layer, b, 0, 0)),
            pl.BlockSpec((1, 1, t_new, QK_ROPE), lambda b: (layer, b, 0, 0)),
            full(wts["wabs"]), full(wts["fold"]), full(wts["w_uv"]),
        ],
        out_specs=pl.BlockSpec((1, t_new, W_C), lambda b: (b, 0, 0)),
        out_shape=jax.ShapeDtypeStruct((bsz, t_new, W_C), BF16),
        compiler_params=pltpu.CompilerParams(dimension_semantics=("arbitrary",), vmem_limit_bytes=VMEM_LIMIT),
        name="attn_sample",
    )(q, lat_cache, kr_cache, lat_new, kr_new, wts["wabs"], wts["fold"], wts["w_uv"])


def _mixer_out_kernel(x_ref, mod_ref, yab_ref, o_ref, gc_ref, w_out_ref, ln_ref, out_ref, *, alpha):
    gate = mod_ref[0][:, 2 * D_MODEL:3 * D_MODEL]
    yc = (o_ref[0].astype(F32) * gc_ref[0].astype(F32)).astype(BF16)
    y = _dot(yab_ref[0], w_out_ref[0:W_A + W_B]) + _dot(yc, w_out_ref[W_A + W_B:W_A + W_B + W_C])
    r = alpha * x_ref[0] + gate * y
    ln = ln_ref[...]
    out_ref[0] = _norm_rows(r, LN_EPS) * ln[0:1] + ln[1:2]


def _mixer_out(x, mod, yab, o, gc, w_out, ln, *, tile, alpha):
    bsz, seq, _ = x.shape
    tok = lambda width: pl.BlockSpec((1, tile, width), lambda b, t: (b, t, 0))
    return pl.pallas_call(
        functools.partial(_mixer_out_kernel, alpha=alpha),
        grid=(bsz, seq // tile),
        in_specs=[tok(D_MODEL), pl.BlockSpec((1, 1, 3 * D_MODEL), lambda b, t: (b, 0, 0)),
                  tok(W_A + W_B), tok(W_C), tok(W_C),
                  pl.BlockSpec(w_out.shape, lambda b, t: (0, 0)),
                  pl.BlockSpec(ln.shape, lambda b, t: (0, 0))],
        out_specs=tok(D_MODEL),
        out_shape=jax.ShapeDtypeStruct(x.shape, F32),
        compiler_params=pltpu.CompilerParams(dimension_semantics=("arbitrary", "arbitrary"),
                                             vmem_limit_bytes=VMEM_LIMIT),
        name="mixer_out",
    )(x, mod, yab, o, gc, w_out, ln)


def _rope_tables(pos0, n):
    half = QK_ROPE // 2
    inv = ROPE_THETA ** (-jnp.arange(0, QK_ROPE, 2, dtype=F32) / QK_ROPE)
    per_row = HEAD_LANES // half
    lane = jnp.arange(HEAD_LANES, dtype=jnp.int32)
    pos = pos0 + jnp.arange(n // per_row, dtype=jnp.int32)[:, None] * per_row + (lane // half)[None, :]
    ang = pos.astype(F32) * inv[lane % half][None, :]
    cos, sin = lax.optimization_barrier((jnp.cos(ang), jnp.sin(ang)))
    cos, sin = cos.reshape(n, half), sin.reshape(n, half)
    ones = jnp.ones((n, QK_NOPE), F32)
    z64 = jnp.zeros((n, QK_NOPE), F32)
    z32 = jnp.zeros((n, HEAD_LANES - QK_NOPE - QK_ROPE), F32)
    return jnp.concatenate([ones, cos, cos, z32], axis=1), jnp.concatenate([z64, sin, sin, z32], axis=1)


def _gmlp_weights(ws, b_s, chunk):
    idx = jnp.arange(chunk)
    mask = (idx[None, :] // CHUNK) <= (idx[:, None] // CHUNK)
    wsm = jnp.where(mask[None], ws[:, :chunk, :chunk], 0.0)
    ws_all = jnp.transpose(wsm, (1, 0, 2)).reshape(chunk, N_HEADS_A * chunk)
    bs_tab = jnp.repeat(b_s[:, :chunk].T, HEAD_A, axis=1)
    return ws_all.astype(BF16), bs_tab.astype(F32)


def _layer_weights(l, p, chunk_p, chunk_s):
    half = QK_ROPE // 2
    w_in = p["w_in"][l]
    krw = w_in[:, 2176:2208]
    x1, x2 = krw[:, :half], krw[:, half:]
    z64 = jnp.zeros((D_MODEL, QK_NOPE), F32)
    z32 = jnp.zeros((D_MODEL, HEAD_LANES - QK_NOPE - QK_ROPE), F32)
    w_in2 = jnp.concatenate([w_in[:, :2176], z64, x1, x2, z32, z64, -x2, x1, z32, w_in[:, 2208:]], axis=1)

    wq = p["mla_w_uq"][l].reshape(Q_LORA, N_HEADS, QK_NOPE + QK_ROPE)
    qn, q1, q2 = wq[..., :QK_NOPE], wq[..., QK_NOPE:QK_NOPE + half], wq[..., QK_NOPE + half:]
    zq32 = jnp.zeros((Q_LORA, N_HEADS, HEAD_LANES - QK_NOPE - QK_ROPE), F32)
    zq64 = jnp.zeros((Q_LORA, N_HEADS, QK_NOPE), F32)
    w_uq = jnp.concatenate([
        jnp.concatenate([qn, q1, q2, zq32], axis=-1).reshape(Q_LORA, -1),
        jnp.concatenate([zq64, -q2, q1, zq32], axis=-1).reshape(Q_LORA, -1)], axis=1)

    wkv = p["mla_w_ukv"][l].reshape(KV_LORA, N_HEADS, QK_NOPE + V_DIM)
    wk, wv = wkv[..., :QK_NOPE], wkv[..., QK_NOPE:]
    w_uk = jnp.concatenate([wk, jnp.zeros_like(wk)], axis=-1).reshape(KV_LORA, -1)
    wvt = jnp.transpose(wv, (1, 2, 0))
    w_uvt = jnp.concatenate([wvt, jnp.zeros((N_HEADS, V_ROWS - V_DIM, KV_LORA), F32)], axis=1).reshape(-1, KV_LORA)
    wkt = jnp.transpose(wk, (1, 2, 0))
    wabs = jnp.concatenate([wkt, jnp.zeros((N_HEADS, HEAD_LANES - QK_NOPE, KV_LORA), F32)], axis=1).reshape(-1, KV_LORA)
    eye = jnp.eye(QK_ROPE, HEAD_LANES, dtype=F32)
    fold_h = jnp.concatenate([jnp.zeros((QK_NOPE, HEAD_LANES), F32), eye,
                              jnp.zeros((HEAD_LANES - QK_NOPE - QK_ROPE, HEAD_LANES), F32)], axis=0)
    fold = jnp.tile(fold_h, (N_HEADS, 1))

    zrow = jnp.zeros((1, W_B), F32)
    vec = jnp.stack([p["gmlp_ln_g"][l], p["gmlp_ln_b"][l], p["conv_dw_b"][l], p["conv_ln_g"][l],
                     p["conv_ln_b"][l], p["conv_b_pw"][l], p["mla_kv_norm"][l], zrow[0]], axis=0)
    common = dict(
        w_in=w_in2.astype(BF16), vec=vec, q_norm=p["mla_q_norm"][l].reshape(1, Q_LORA),
        dwk=jnp.concatenate([p["conv_dw_k"][l], zrow], axis=0), w_pw=p["conv_w_pw"][l].astype(BF16),
        w_uq=w_uq.astype(BF16), w_uk=w_uk.astype(BF16), w_uvt=w_uvt.astype(BF16),
        wabs=wabs.astype(BF16), fold=fold.astype(BF16), w_uv=wv.reshape(KV_LORA, W_C).astype(BF16),
        w_out=p["w_out"][l].astype(BF16), ln=jnp.stack([p["post_ln_g"][l], p["post_ln_b"][l]], axis=0))
    ws_p, bs_p = _gmlp_weights(p["gmlp_ws"][l], p["gmlp_bs"][l], chunk_p)
    ws_s, bs_s = _gmlp_weights(p["gmlp_ws"][l], p["gmlp_bs"][l], chunk_s)
    return dict(common, ws=ws_p, bs=bs_p), dict(common, ws=ws_s, bs=bs_s)


def kernel(x_prompt, x_sample, cache_latent, cache_krope, state_conv, c_prompt, c_sample,
           w_ada, b_ada, w_in, gmlp_ln_g, gmlp_ln_b, gmlp_ws, gmlp_bs,
           conv_dw_k, conv_dw_b, conv_ln_g, conv_ln_b, conv_w_pw, conv_b_pw,
           mla_q_norm, mla_w_uq, mla_kv_norm, mla_w_ukv, w_out, post_ln_g, post_ln_b):
    p = dict(w_in=w_in, gmlp_ln_g=gmlp_ln_g, gmlp_ln_b=gmlp_ln_b, gmlp_ws=gmlp_ws, gmlp_bs=gmlp_bs,
             conv_dw_k=conv_dw_k, conv_dw_b=conv_dw_b, conv_ln_g=conv_ln_g, conv_ln_b=conv_ln_b,
             conv_w_pw=conv_w_pw, conv_b_pw=conv_b_pw, mla_q_norm=mla_q_norm, mla_w_uq=mla_w_uq,
             mla_kv_norm=mla_kv_norm, mla_w_ukv=mla_w_ukv, w_out=w_out, post_ln_g=post_ln_g, post_ln_b=post_ln_b)
    depth = w_ada.shape[0]
    bp, seq, _ = x_prompt.shape
    bs, t_new, _ = x_sample.shape
    past_len = cache_latent.shape[2]
    alpha = (2 * depth) ** 0.25
    tile_p = min(IN_TILE, seq)
    chunk_p = min(seq, MLP_CHUNK)
    chunk_s = min(t_new, MLP_CHUNK)

    n_c = bp + bs
    c_all = jnp.concatenate([c_prompt, c_sample, jnp.zeros((-n_c % 8, D_MODEL), F32)], axis=0)
    mod_all = _ada(c_all, w_ada, b_ada)

    ct_p, st_p = _rope_tables(0, seq)
    ct_s, st_s = _rope_tables(past_len, t_new)
    zero_past = jnp.zeros((bp, HIST, W_B), F32)

    xp, xs = x_prompt, x_sample
    p_conv, s_conv, s_v = [], [], []
    p_stack = s_stack = None
    for l in range(depth):
        wp, wsm = _layer_weights(l, p, chunk_p, chunk_s)
        mod_p = mod_all[l, 0:bp].reshape(bp, 1, -1)
        mod_s = mod_all[l, bp:n_c].reshape(bs, 1, -1)

        q, lat, kr, yab, gc, conv, k, vt = _mixer_in(xp, mod_p, ct_p, st_p, zero_past, wp, tile=tile_p,
                                                      chunk=chunk_p, prompt=True, layer=l, depth=depth,
                                                      stacked=p_stack)
        p_stack = (lat, kr)
        o = _attn_prompt(q, k, vt)
        xp = _mixer_out(xp, mod_p, yab, o, gc, wp["w_out"], wp["ln"], tile=tile_p, alpha=alpha)
        p_conv.append(conv[:, HIST - (CONV_W - 1):])

        past = jnp.concatenate([jnp.zeros((bs, HIST - (CONV_W - 1), W_B), F32), state_conv[l]], axis=1)
        q, lat, kr, yab, gc, conv, vst = _mixer_in(xs, mod_s, ct_s, st_s, past, wsm, tile=t_new,
                                                   chunk=chunk_s, prompt=False, layer=l, depth=depth,
                                                   stacked=s_stack)
        s_stack = (lat, kr)
        o = _attn_sample(q, cache_latent, cache_krope, l, lat, kr, wsm)
        xs = _mixer_out(xs, mod_s, yab, o, gc, wsm["w_out"], wsm["ln"], tile=t_new, alpha=alpha)
        s_conv.append(conv[:, HIST - (CONV_W - 1):])
        s_v.append(vst)

    return (xp, xs, jnp.stack(p_conv), p_stack[0], p_stack[1],
            jnp.stack(s_conv), s_stack[0], s_stack[1], jnp.stack(s_v))
```

```python
import functools
import math

import jax
import jax.numpy as jnp
from jax import lax
from jax.experimental import pallas as pl
from jax.experimental.pallas import tpu as pltpu

F32 = jnp.float32
BF16 = jnp.bfloat16

D_MODEL = 1024
N_HEADS_A = 4
HEAD_A = 64
W_A = 256
W_B = 256
CONV_W = 31
N_HEADS = 8
QK_NOPE = 64
QK_ROPE = 32
V_DIM = 64
W_C = N_HEADS * V_DIM
Q_LORA = 384
KV_LORA = 256
CHUNK = 64
MLP_CHUNK = 128
ROPE_THETA = 10000.0
ATTN_SCALE = (QK_NOPE + QK_ROPE) ** -0.5
LN_EPS = 1e-5
RMS_EPS = 1e-6

HEAD_LANES = 128
V_ROWS = 80
HIST = 32
ATT_TILE = 256
Q_BLOCK_ROWS = 2048
IN_TILE = 1024
VMEM_LIMIT = 56 * 1024 * 1024

_C_A, _C_B, _C_Q, _C_KV, _C_KR, _C_GC, _C_END = 0, 768, 1536, 1920, 2176, 2432, 2944
_Q_SCALE = ATTN_SCALE * math.log2(math.e)


def _dot(a, b):
    return jnp.dot(a, b, preferred_element_type=F32)


def _dot_nt(a, b):
    return lax.dot_general(a, b, (((1,), (1,)), ((), ())), preferred_element_type=F32)


def _norm_rows(x, eps):
    mu = jnp.mean(x, axis=-1, keepdims=True)
    xc = x - mu
    var = jnp.mean(xc * xc, axis=-1, keepdims=True)
    return xc * lax.rsqrt(var + eps)


def _gelu_tanh(x):
    return x * (0.5 * (1.0 + jnp.tanh(math.sqrt(2.0 / math.pi) * (x + 0.044715 * (x * x * x)))))


def _silu(x):
    return x * jax.nn.sigmoid(x)


def _ada_kernel(c_ref, w_ref, b_ref, o_ref):
    c = c_ref[...]
    a = _silu(c)
    w = w_ref[0]
    a_hi = a.astype(BF16)
    a_lo = (a - a_hi.astype(F32)).astype(BF16)
    w_hi = w.astype(BF16)
    w_lo = (w - w_hi.astype(F32)).astype(BF16)
    o_ref[0] = _dot(a_hi, w_hi) + (_dot(a_hi, w_lo) + _dot(a_lo, w_hi)) + b_ref[0]


def _ada(c_all, w_ada, b_ada):
    depth = w_ada.shape[0]
    rows = c_all.shape[0]
    n_col = w_ada.shape[2] // D_MODEL
    return pl.pallas_call(
        _ada_kernel,
        grid=(depth, n_col),
        in_specs=[
            pl.BlockSpec((rows, D_MODEL), lambda l, j: (0, 0)),
            pl.BlockSpec((1, D_MODEL, D_MODEL), lambda l, j: (l, 0, j)),
            pl.BlockSpec((1, 1, D_MODEL), lambda l, j: (l, 0, j)),
        ],
        out_specs=pl.BlockSpec((1, rows, D_MODEL), lambda l, j: (l, 0, j)),
        out_shape=jax.ShapeDtypeStruct((depth, rows, w_ada.shape[2]), F32),
        compiler_params=pltpu.CompilerParams(vmem_limit_bytes=VMEM_LIMIT),
        name="ada_mod",
    )(c_all, w_ada, b_ada.reshape(depth, 1, -1))


def _mixer_in_kernel(x_ref, mod_ref, ct_ref, st_ref, past_ref, w_in_ref, vec_ref, qn_ref, ws_ref, bs_ref,
                     dwk_ref, w_pw_ref, w_uq_ref, *rest, tile, chunk, prompt, n_stacked):
    if prompt:
        w_uk_ref, w_uvt_ref = rest[:2]
        rest = rest[2:]
    rest = rest[n_stacked:]
    if prompt:
        (q_ref, lat_ref, kr_ref, yab_ref, gc_ref, conv_ref, k_ref, vt_ref, hist) = rest
    else:
        (q_ref, lat_ref, kr_ref, yab_ref, gc_ref, conv_ref, vst_ref, hist) = rest
    t = pl.program_id(1)

    mod = mod_ref[0]
    shift = mod[:, 0:D_MODEL]
    scale = mod[:, D_MODEL:2 * D_MODEL]
    h = _norm_rows(x_ref[0], LN_EPS) * (1.0 + scale) + shift
    hb = h.astype(BF16)

    vec = vec_ref[...]
    gmlp_g, gmlp_b = vec[0:1], vec[1:2]
    dw_b, conv_g, conv_b, b_pw, kv_norm = vec[2:3], vec[3:4], vec[4:5], vec[5:6], vec[6:7]

    za = _dot(hb, w_in_ref[:, _C_A:_C_B])
    u = _gelu_tanh(za[:, 0:W_A])
    v = _norm_rows(_gelu_tanh(za[:, W_A:2 * W_A]), LN_EPS) * gmlp_g + gmlp_b
    if not prompt:
        vst_ref[0] = v
    vb = v.astype(BF16)
    rows = lax.broadcasted_iota(jnp.int32, (N_HEADS_A * chunk, W_A), 0)
    cols = lax.broadcasted_iota(jnp.int32, (N_HEADS_A * chunk, W_A), 1)
    own_head = (rows // chunk) == (cols // HEAD_A)
    ws = ws_ref[...]
    bs = bs_ref[...]
    mixed = []
    for c in range(tile // chunk):
        vc = vb[c * chunk:(c + 1) * chunk]
        vbd = jnp.where(own_head, jnp.concatenate([vc] * N_HEADS_A, axis=0), jnp.zeros((), BF16))
        mixed.append(_dot(ws, vbd) + bs)
    s = mixed[0] if len(mixed) == 1 else jnp.concatenate(mixed, axis=0)
    y_a = u * s * _silu(za[:, 2 * W_A:3 * W_A])
    yab_ref[0, :, 0:W_A] = y_a.astype(BF16)

    zb = _dot(hb, w_in_ref[:, _C_B:_C_Q])
    g = zb[:, 0:W_B] * jax.nn.sigmoid(zb[:, W_B:2 * W_B])

    @pl.when(t == 0)
    def _():
        hist[0:HIST] = past_ref[0]

    hist[HIST:HIST + tile] = g
    dwk = dwk_ref[...]
    y = jnp.broadcast_to(dw_b, (tile, W_B))
    first = HIST - (CONV_W - 1)
    grouped = tile >= 8 * 8
    for r in range(8 if grouped else 0):
        rows = tile if r == 0 else tile + 8
        part = None
        for a in range(HIST // 8 + 1):
            k = 8 * a + r - first
            if 0 <= k < CONV_W:
                term = dwk[k:k + 1] * hist[8 * a:8 * a + rows]
                part = term if part is None else part + term
        y = y + (part if r == 0 else part[r:r + tile])
    for k in range(0 if grouped else CONV_W):
        y = y + dwk[k:k + 1] * hist[first + k:first + k + tile]
    new_hist = hist[tile:tile + HIST]
    hist[0:HIST] = new_hist
    conv_ref[0] = new_hist
    y = _silu(_norm_rows(y, LN_EPS) * conv_g + conv_b)
    y = _dot(y.astype(BF16), w_pw_ref[...]) + b_pw
    y_b = y * _silu(zb[:, 2 * W_B:3 * W_B])
    yab_ref[0, :, W_A:W_A + W_B] = y_b.astype(BF16)

    ct = ct_ref[...]
    st = st_ref[...]
    zc = _dot(hb, w_in_ref[:, _C_Q:_C_GC])
    zq = zc[:, 0:_C_KV - _C_Q]
    cq = zq * lax.rsqrt(jnp.mean(zq * zq, axis=-1, keepdims=True) + RMS_EPS) * qn_ref[...]
    q2 = _dot(cq.astype(BF16), w_uq_ref[...])
    ctq = ct * _Q_SCALE
    stq = st * _Q_SCALE
    for hd in range(N_HEADS):
        lo = hd * HEAD_LANES
        qh = q2[:, lo:lo + HEAD_LANES] * ctq + q2[:, N_HEADS * HEAD_LANES + lo:N_HEADS * HEAD_LANES + lo + HEAD_LANES] * stq
        q_ref[0, :, lo:lo + HEAD_LANES] = qh.astype(BF16)

    zkv = zc[:, _C_KV - _C_Q:_C_KR - _C_Q]
    lat = zkv * lax.rsqrt(jnp.mean(zkv * zkv, axis=-1, keepdims=True) + RMS_EPS) * kv_norm
    lat_ref[0, 0] = lat
    zkr = zc[:, _C_KR - _C_Q:_C_GC - _C_Q]
    krot = zkr[:, 0:HEAD_LANES] * ct + zkr[:, HEAD_LANES:2 * HEAD_LANES] * st
    kr_ref[0, 0] = krot[:, QK_NOPE:QK_NOPE + QK_ROPE]

    if prompt:
        latb = lat.astype(BF16)
        kn = _dot(latb, w_uk_ref[...])
        for hd in range(N_HEADS):
            lo = hd * HEAD_LANES
            k_ref[0, :, lo:lo + HEAD_LANES] = (kn[:, lo:lo + HEAD_LANES] + krot).astype(BF16)
        vt = _dot_nt(w_uvt_ref[...], latb)
        vrow = lax.broadcasted_iota(jnp.int32, vt.shape, 0)
        vt = jnp.where((vrow % V_ROWS) == V_DIM, 1.0, vt)
        for c in range(tile // ATT_TILE):
            blk = vt[:, c * ATT_TILE:(c + 1) * ATT_TILE].reshape(N_HEADS, V_ROWS, ATT_TILE)
            vt_ref[0, :, c] = blk.astype(BF16)

    zg = _dot(hb, w_in_ref[:, _C_GC:_C_END])
    gc_ref[0] = _silu(zg).astype(BF16)


def _mixer_in(x, mod, ctab, stab, past, wts, *, tile, chunk, prompt, layer, depth, stacked=None):
    bsz, seq, _ = x.shape
    n_t = seq // tile
    kern = functools.partial(_mixer_in_kernel, tile=tile, chunk=chunk, prompt=prompt,
                             n_stacked=0 if stacked is None else len(stacked))

    def full(a):
        return pl.BlockSpec(a.shape, lambda b, t, _n=a.ndim: (0,) * _n)

    weights = [wts["w_in"], wts["vec"], wts["q_norm"], wts["ws"], wts["bs"], wts["dwk"], wts["w_pw"], wts["w_uq"]]
    if prompt:
        weights += [wts["w_uk"], wts["w_uvt"]]
    in_specs = [
        pl.BlockSpec((1, tile, D_MODEL), lambda b, t: (b, t, 0)),
        pl.BlockSpec((1, 1, 3 * D_MODEL), lambda b, t: (b, 0, 0)),
        pl.BlockSpec((tile, HEAD_LANES), lambda b, t: (t, 0)),
        pl.BlockSpec((tile, HEAD_LANES), lambda b, t: (t, 0)),
        pl.BlockSpec((1, HIST, W_B), lambda b, t: (b, 0, 0)),
    ] + [full(w) for w in weights]
    operands = [x, mod, ctab, stab, past, *weights]
    aliases = {}
    if stacked is not None:
        for i, a in enumerate(stacked):
            aliases[len(operands)] = 1 + i
            in_specs.append(pl.BlockSpec(memory_space=pl.ANY))
            operands.append(a)
    tok = lambda width: pl.BlockSpec((1, tile, width), lambda b, t: (b, t, 0))
    lay = lambda width: pl.BlockSpec((1, 1, tile, width), lambda b, t: (layer, b, t, 0))
    out_specs = [tok(N_HEADS * HEAD_LANES), lay(KV_LORA), lay(QK_ROPE), tok(W_A + W_B), tok(W_C),
                 pl.BlockSpec((1, HIST, W_B), lambda b, t: (b, 0, 0))]
    out_shape = [
        jax.ShapeDtypeStruct((bsz, seq, N_HEADS * HEAD_LANES), BF16),
        jax.ShapeDtypeStruct((depth, bsz, seq, KV_LORA), F32),
        jax.ShapeDtypeStruct((depth, bsz, seq, QK_ROPE), F32),
        jax.ShapeDtypeStruct((bsz, seq, W_A + W_B), BF16),
        jax.ShapeDtypeStruct((bsz, seq, W_C), BF16),
        jax.ShapeDtypeStruct((bsz, HIST, W_B), F32),
    ]
    if prompt:
        n_kt = tile // ATT_TILE
        out_specs += [tok(N_HEADS * HEAD_LANES),
                      pl.BlockSpec((1, N_HEADS, n_kt, V_ROWS, ATT_TILE), lambda b, t: (b, 0, t, 0, 0))]
        out_shape += [jax.ShapeDtypeStruct((bsz, seq, N_HEADS * HEAD_LANES), BF16),
                      jax.ShapeDtypeStruct((bsz, N_HEADS, seq // ATT_TILE, V_ROWS, ATT_TILE), BF16)]
    else:
        out_specs += [tok(W_A)]
        out_shape += [jax.ShapeDtypeStruct((bsz, seq, W_A), F32)]
    return pl.pallas_call(
        kern,
        grid=(bsz, n_t),
        in_specs=in_specs,
        out_specs=out_specs,
        out_shape=out_shape,
        scratch_shapes=[pltpu.VMEM((HIST + tile, W_B), F32)],
        input_output_aliases=aliases,
        compiler_params=pltpu.CompilerParams(dimension_semantics=("arbitrary", "arbitrary"),
                                             vmem_limit_bytes=VMEM_LIMIT),
        name="mixer_in_prompt" if prompt else "mixer_in_sample",
    )(*operands)


_S_ADDR = (0, 64)
_O_ADDR = ((128, 148), (168, 188))
_FIXED_TILES = 5


def _attn_prompt_kernel(q_ref, k_ref, vt_ref, bias_ref, o_ref, qt_sc, s_sc, acc_sc, *, n_sub, n_kt):
    qi = pl.program_id(2)

    def tile_index(t):
        return jnp.minimum(t, n_kt - 1)

    class SubTile:
        def __init__(self, sub):
            self.sp = sub % 2
            self.row0 = sub * ATT_TILE
            self.g = qi * n_sub + sub
            self.n_steady = jnp.maximum((self.g - 3) // 2, 0)
            self.j0 = _FIXED_TILES + 2 * self.n_steady - 3
            q_t = q_ref[0, self.row0:self.row0 + ATT_TILE, :].astype(F32).T
            zero = jnp.zeros((HEAD_LANES, ATT_TILE), BF16)
            for hd in range(2):
                own = slice(hd * HEAD_LANES, (hd + 1) * HEAD_LANES)
                other = slice((1 - hd) * HEAD_LANES, (2 - hd) * HEAD_LANES)
                qt_sc[self.sp, hd, own] = q_t[own].astype(BF16)
                qt_sc[self.sp, hd, other] = zero
            acc_sc[self.sp] = jnp.zeros(acc_sc.shape[1:], F32)
            neg = jnp.full((1, ATT_TILE), -jnp.inf, F32)
            one = jnp.ones((1, ATT_TILE), F32)
            self.state = ([neg, neg], [one, one], [one, one], [one, one])

        def stage_qt(self):
            for hd in range(2):
                pltpu.matmul_push_rhs(qt_sc[self.sp, hd], staging_register=0, mxu_index=hd)

        def stage_a(self, t, par, restage):
            k_tile = k_ref[0, pl.ds(pl.multiple_of(tile_index(t) * ATT_TILE, ATT_TILE), ATT_TILE), :]
            for hd in range(2):
                pltpu.matmul_acc_lhs(_S_ADDR[par], k_tile, mxu_index=hd, load_staged_rhs=0)
                if restage:
                    pltpu.matmul_push_rhs(qt_sc[self.sp, hd], staging_register=0, mxu_index=hd)

        def stage_b(self, t, par, masked, m_old=None):
            m_old = self.state[0] if m_old is None else m_old
            m_new, alpha = [], []
            if masked:
                bias = bias_ref[jnp.where(t < self.g, 0, jnp.where(t == self.g, 1, 2))]
            for hd in range(2):
                s = pltpu.matmul_pop(_S_ADDR[par], (ATT_TILE, ATT_TILE), F32, mxu_index=hd)
                if masked:
                    s = s + bias
                m_hd = jnp.maximum(m_old[hd], jnp.max(s, axis=0, keepdims=True))
                alpha.append(jnp.exp2(m_old[hd] - m_hd))
                m_new.append(m_hd)
                s_sc[self.sp, hd, par] = s
            return m_new, alpha

        def stage_c(self, t, par, m_t=None):
            m_t = self.state[0] if m_t is None else m_t
            for hd in range(2):
                p = jnp.exp2(s_sc[self.sp, hd, par] - m_t[hd]).astype(BF16)
                pltpu.matmul_push_rhs(p, staging_register=1, mxu_index=hd)
                pltpu.matmul_acc_lhs(_O_ADDR[self.sp][par], vt_ref[0, hd, tile_index(t)], mxu_index=hd,
                                     load_staged_rhs=1)

        def stage_d(self, par, alpha_t=None):
            alpha_t = self.state[3] if alpha_t is None else alpha_t
            for hd in range(2):
                o = pltpu.matmul_pop(_O_ADDR[self.sp][par], (V_ROWS, ATT_TILE), F32, mxu_index=hd)
                acc_sc[self.sp, hd] = alpha_t[hd] * acc_sc[self.sp, hd] + o

        def finish(self):
            outs = []
            for hd in range(2):
                acc = acc_sc[self.sp, hd]
                outs.append(acc[0:V_DIM] * (1.0 / acc[V_DIM:V_DIM + 1]))
            o_t = jnp.concatenate(outs, axis=0)
            o_ref[0, self.row0:self.row0 + ATT_TILE, :] = o_t.T.astype(BF16)

    def run(steps):
        for st, j, par, f in steps:
            if f.get("d", True):
                st.stage_d(par)
        new = []
        for st, j, par, f in steps:
            m, al1, al2, al3 = st.state
            if f.get("b", True):
                m_next, al0 = st.stage_b(j + 1, 1 - par, f.get("masked", False))
            else:
                m_next, al0 = m, al1
            new.append((m_next, al0, al1, al2))
        for st, j, par, f in steps:
            if f.get("a", True):
                st.stage_a(j + 2, par, f.get("restage", True))
            if f.get("qt_of") is not None:
                f["qt_of"].stage_qt()
        for st, j, par, f in steps:
            if f.get("c", True):
                st.stage_c(j, par)
        for (st, j, par, f), state in zip(steps, new):
            st.state = state

    fill = [dict(b=False, c=False, d=False), dict(c=False, d=False, masked=True),
            dict(d=False, masked=True), dict(d=False, masked=True)]

    def drain(nxt):
        return [dict(masked=True, restage=False, qt_of=nxt), dict(a=False, masked=True),
                dict(a=False, b=False), dict(a=False, b=False, c=False), dict(a=False, b=False, c=False)]

    cur = SubTile(0)
    cur.stage_qt()
    for i in range(4):
        run([(cur, i - 2, i % 2, fill[i])])
    for sub in range(n_sub):
        def steady(state, j, n_iter, st=cur):
            st.state = state
            for i in range(n_iter):
                run([(st, j + i, i % 2, {})])
            return st.state

        octs = cur.n_steady // 4
        rest = cur.n_steady - 4 * octs
        cur.state = lax.fori_loop(0, octs, lambda u, s: steady(s, 2 + 8 * u, 8), cur.state)
        j_q = 2 + 8 * octs
        cur.state = lax.cond(rest >= 2, lambda s: steady(s, j_q, 4), lambda s: s, cur.state)
        j_p = j_q + jnp.where(rest >= 2, 4, 0)
        cur.state = lax.cond(rest % 2 == 1, lambda s: steady(s, j_p, 2), lambda s: s, cur.state)
        nxt = SubTile(sub + 1) if sub + 1 < n_sub else None
        dr = drain(nxt)
        run([(cur, cur.j0, 0, dr[0])])
        for i in range(4):
            steps = [(cur, cur.j0 + 1 + i, (i + 1) % 2, dr[i + 1])]
            if nxt is not None:
                steps.append((nxt, i - 2, i % 2, fill[i]))
            run(steps)
        cur.finish()
        cur = nxt


def _attn_prompt(q, k, vt):
    bsz, seq, _ = q.shape
    qb = min(Q_BLOCK_ROWS, seq)
    n_sub = qb // ATT_TILE
    n_kt = seq // ATT_TILE
    kern = functools.partial(_attn_prompt_kernel, n_sub=n_sub, n_kt=n_kt)
    idx = jnp.arange(ATT_TILE) // CHUNK
    diag = jnp.where(idx[:, None] <= idx[None, :], 0.0, -jnp.inf).astype(F32)
    bias = jnp.stack([jnp.zeros_like(diag), diag, jnp.full_like(diag, -jnp.inf)])
    return pl.pallas_call(
        kern,
        grid=(bsz, N_HEADS // 2, seq // qb),
        in_specs=[
            pl.BlockSpec((1, qb, 2 * HEAD_LANES), lambda b, hp, i: (b, i, hp)),
            pl.BlockSpec((1, seq, 2 * HEAD_LANES), lambda b, hp, i: (b, 0, hp)),
            pl.BlockSpec((1, 2, n_kt, V_ROWS, ATT_TILE), lambda b, hp, i: (b, hp, 0, 0, 0)),
            pl.BlockSpec((3, ATT_TILE, ATT_TILE), lambda b, hp, i: (0, 0, 0)),
        ],
        out_specs=pl.BlockSpec((1, qb, 2 * V_DIM), lambda b, hp, i: (b, i, hp)),
        out_shape=jax.ShapeDtypeStruct((bsz, seq, W_C), BF16),
        scratch_shapes=[pltpu.VMEM((2, 2, 2 * HEAD_LANES, ATT_TILE), BF16),
                        pltpu.VMEM((2, 2, 2, ATT_TILE, ATT_TILE), F32),
                        pltpu.VMEM((2, 2, V_ROWS, ATT_TILE), F32)],
        compiler_params=pltpu.CompilerParams(dimension_semantics=("arbitrary", "arbitrary", "arbitrary"),
                                             vmem_limit_bytes=VMEM_LIMIT),
        name="attn_prompt",
    )(q, k, vt, bias)


def _attn_sample_kernel(q_ref, latc_ref, krc_ref, latn_ref, krn_ref, wabs_ref, fold_ref, wuv_ref, o_ref, *, t_new):
    q = q_ref[0]
    qrep = jnp.concatenate([q] * N_HEADS, axis=0)
    r1 = lax.broadcasted_iota(jnp.int32, qrep.shape, 0)
    c1 = lax.broadcasted_iota(jnp.int32, qrep.shape, 1)
    qrep = jnp.where((r1 // t_new) == (c1 // HEAD_LANES), qrep, jnp.zeros((), BF16))
    qlat = _dot(qrep, wabs_ref[...]).astype(BF16)
    qr = _dot(qrep, fold_ref[...])[:, 0:QK_ROPE].astype(BF16)
    latc = latc_ref[0, 0].astype(BF16)
    krc = krc_ref[0, 0].astype(BF16)
    latn = latn_ref[0, 0].astype(BF16)
    krn = krn_ref[0, 0].astype(BF16)
    s_c = _dot_nt(qlat, latc) + _dot_nt(qr, krc)
    s_n = _dot_nt(qlat, latn) + _dot_nt(qr, krn)
    m = jnp.maximum(jnp.max(s_c, axis=-1, keepdims=True), jnp.max(s_n, axis=-1, keepdims=True))
    p_c = jnp.exp2(s_c - m)
    p_n = jnp.exp2(s_n - m)
    den = jnp.sum(p_c, axis=-1, keepdims=True) + jnp.sum(p_n, axis=-1, keepdims=True)
    olat = (_dot(p_c.astype(BF16), latc) + _dot(p_n.astype(BF16), latn)) * (1.0 / den)
    of = _dot(olat.astype(BF16), wuv_ref[...])
    r2 = lax.broadcasted_iota(jnp.int32, of.shape, 0)
    c2 = lax.broadcasted_iota(jnp.int32, of.shape, 1)
    of = jnp.where((r2 // t_new) == (c2 // V_DIM), of, 0.0)
    o = of[0:t_new]
    for hd in range(1, N_HEADS):
        o = o + of[hd * t_new:(hd + 1) * t_new]
    o_ref[0] = o.astype(BF16)


def _attn_sample(q, lat_cache, kr_cache, layer, lat_new, kr_new, wts):
    bsz, t_new, _ = q.shape
    past = lat_cache.shape[2]
    kern = functools.partial(_attn_sample_kernel, t_new=t_new)

    def full(a):
        return pl.BlockSpec(a.shape, lambda b, _n=a.ndim: (0,) * _n)

    return pl.pallas_call(
        kern,
        grid=(bsz,),
        in_specs=[
            pl.BlockSpec((1, t_new, N_HEADS * HEAD_LANES), lambda b: (b, 0, 0)),
            pl.BlockSpec((1, 1, past, KV_LORA), lambda b: (layer, b, 0, 0)),
            pl.BlockSpec((1, 1, past, QK_ROPE), lambda b: (layer, b, 0, 0)),
            pl.BlockSpec((1, 1, t_new, KV_LORA), lambda b: (layer, b, 0, 0)),
            pl.BlockSpec((1, 1, t_new, QK_ROPE), lambda b: (layer, b, 0, 0)),
            full(wts["wabs"]), full(wts["fold"]), full(wts["w_uv"]),
        ],
        out_specs=pl.BlockSpec((1, t_new, W_C), lambda b: (b, 0, 0)),
        out_shape=jax.ShapeDtypeStruct((bsz, t_new, W_C), BF16),
        compiler_params=pltpu.CompilerParams(dimension_semantics=("arbitrary",), vmem_limit_bytes=VMEM_LIMIT),
        name="attn_sample",
    )(q, lat_cache, kr_cache, lat_new, kr_new, wts["wabs"], wts["fold"], wts["w_uv"])


def _mixer_out_kernel(x_ref, mod_ref, yab_ref, o_ref, gc_ref, w_out_ref, ln_ref, out_ref, *, alpha):
    gate = mod_ref[0][:, 2 * D_MODEL:3 * D_MODEL]
    yc = (o_ref[0].astype(F32) * gc_ref[0].astype(F32)).astype(BF16)
    y = _dot(yab_ref[0], w_out_ref[0:W_A + W_B]) + _dot(yc, w_out_ref[W_A + W_B:W_A + W_B + W_C])
    r = alpha * x_ref[0] + gate * y
    ln = ln_ref[...]
    out_ref[0] = _norm_rows(r, LN_EPS) * ln[0:1] + ln[1:2]


def _mixer_out(x, mod, yab, o, gc, w_out, ln, *, tile, alpha):
    bsz, seq, _ = x.shape
    tok = lambda width: pl.BlockSpec((1, tile, width), lambda b, t: (b, t, 0))
    return pl.pallas_call(
        functools.partial(_mixer_out_kernel, alpha=alpha),
        grid=(bsz, seq // tile),
        in_specs=[tok(D_MODEL), pl.BlockSpec((1, 1, 3 * D_MODEL), lambda b, t: (b, 0, 0)),
                  tok(W_A + W_B), tok(W_C), tok(W_C),
                  pl.BlockSpec(w_out.shape, lambda b, t: (0, 0)),
                  pl.BlockSpec(ln.shape, lambda b, t: (0, 0))],
        out_specs=tok(D_MODEL),
        out_shape=jax.ShapeDtypeStruct(x.shape, F32),
        compiler_params=pltpu.CompilerParams(dimension_semantics=("arbitrary", "arbitrary"),
                                             vmem_limit_bytes=VMEM_LIMIT),
        name="mixer_out",
    )(x, mod, yab, o, gc, w_out, ln)


def _rope_tables(pos0, n):
    half = QK_ROPE // 2
    inv = ROPE_THETA ** (-jnp.arange(0, QK_ROPE, 2, dtype=F32) / QK_ROPE)
    per_row = HEAD_LANES // half
    lane = jnp.arange(HEAD_LANES, dtype=jnp.int32)
    pos = pos0 + jnp.arange(n // per_row, dtype=jnp.int32)[:, None] * per_row + (lane // half)[None, :]
    ang = pos.astype(F32) * inv[lane % half][None, :]
    cos, sin = lax.optimization_barrier((jnp.cos(ang), jnp.sin(ang)))
    cos, sin = cos.reshape(n, half), sin.reshape(n, half)
    ones = jnp.ones((n, QK_NOPE), F32)
    z64 = jnp.zeros((n, QK_NOPE), F32)
    z32 = jnp.zeros((n, HEAD_LANES - QK_NOPE - QK_ROPE), F32)
    return jnp.concatenate([ones, cos, cos, z32], axis=1), jnp.concatenate([z64, sin, sin, z32], axis=1)


def _gmlp_weights(ws, b_s, chunk):
    idx = jnp.arange(chunk)
    mask = (idx[None, :] // CHUNK) <= (idx[:, None] // CHUNK)
    wsm = jnp.where(mask[None], ws[:, :chunk, :chunk], 0.0)
    ws_all = jnp.transpose(wsm, (1, 0, 2)).reshape(chunk, N_HEADS_A * chunk)
    bs_tab = jnp.repeat(b_s[:, :chunk].T, HEAD_A, axis=1)
    return ws_all.astype(BF16), bs_tab.astype(F32)


def _layer_weights(l, p, chunk_p, chunk_s):
    half = QK_ROPE // 2
    w_in = p["w_in"][l]
    krw = w_in[:, 2176:2208]
    x1, x2 = krw[:, :half], krw[:, half:]
    z64 = jnp.zeros((D_MODEL, QK_NOPE), F32)
    z32 = jnp.zeros((D_MODEL, HEAD_LANES - QK_NOPE - QK_ROPE), F32)
    w_in2 = jnp.concatenate([w_in[:, :2176], z64, x1, x2, z32, z64, -x2, x1, z32, w_in[:, 2208:]], axis=1)

    wq = p["mla_w_uq"][l].reshape(Q_LORA, N_HEADS, QK_NOPE + QK_ROPE)
    qn, q1, q2 = wq[..., :QK_NOPE], wq[..., QK_NOPE:QK_NOPE + half], wq[..., QK_NOPE + half:]
    zq32 = jnp.zeros((Q_LORA, N_HEADS, HEAD_LANES - QK_NOPE - QK_ROPE), F32)
    zq64 = jnp.zeros((Q_LORA, N_HEADS, QK_NOPE), F32)
    w_uq = jnp.concatenate([
        jnp.concatenate([qn, q1, q2, zq32], axis=-1).reshape(Q_LORA, -1),
        jnp.concatenate([zq64, -q2, q1, zq32], axis=-1).reshape(Q_LORA, -1)], axis=1)

    wkv = p["mla_w_ukv"][l].reshape(KV_LORA, N_HEADS, QK_NOPE + V_DIM)
    wk, wv = wkv[..., :QK_NOPE], wkv[..., QK_NOPE:]
    w_uk = jnp.concatenate([wk, jnp.zeros_like(wk)], axis=-1).reshape(KV_LORA, -1)
    wvt = jnp.transpose(wv, (1, 2, 0))
    w_uvt = jnp.concatenate([wvt, jnp.zeros((N_HEADS, V_ROWS - V_DIM, KV_LORA), F32)], axis=1).reshape(-1, KV_LORA)
    wkt = jnp.transpose(wk, (1, 2, 0))
    wabs = jnp.concatenate([wkt, jnp.zeros((N_HEADS, HEAD_LANES - QK_NOPE, KV_LORA), F32)], axis=1).reshape(-1, KV_LORA)
    eye = jnp.eye(QK_ROPE, HEAD_LANES, dtype=F32)
    fold_h = jnp.concatenate([jnp.zeros((QK_NOPE, HEAD_LANES), F32), eye,
                              jnp.zeros((HEAD_LANES - QK_NOPE - QK_ROPE, HEAD_LANES), F32)], axis=0)
    fold = jnp.tile(fold_h, (N_HEADS, 1))

    zrow = jnp.zeros((1, W_B), F32)
    vec = jnp.stack([p["gmlp_ln_g"][l], p["gmlp_ln_b"][l], p["conv_dw_b"][l], p["conv_ln_g"][l],
                     p["conv_ln_b"][l], p["conv_b_pw"][l], p["mla_kv_norm"][l], zrow[0]], axis=0)
    common = dict(
        w_in=w_in2.astype(BF16), vec=vec, q_norm=p["mla_q_norm"][l].reshape(1, Q_LORA),
        dwk=jnp.concatenate([p["conv_dw_k"][l], zrow], axis=0), w_pw=p["conv_w_pw"][l].astype(BF16),
        w_uq=w_uq.astype(BF16), w_uk=w_uk.astype(BF16), w_uvt=w_uvt.astype(BF16),
        wabs=wabs.astype(BF16), fold=fold.astype(BF16), w_uv=wv.reshape(KV_LORA, W_C).astype(BF16),
        w_out=p["w_out"][l].astype(BF16), ln=jnp.stack([p["post_ln_g"][l], p["post_ln_b"][l]], axis=0))
    ws_p, bs_p = _gmlp_weights(p["gmlp_ws"][l], p["gmlp_bs"][l], chunk_p)
    ws_s, bs_s = _gmlp_weights(p["gmlp_ws"][l], p["gmlp_bs"][l], chunk_s)
    return dict(common, ws=ws_p, bs=bs_p), dict(common, ws=ws_s, bs=bs_s)


def kernel(x_prompt, x_sample, cache_latent, cache_krope, state_conv, c_prompt, c_sample,
           w_ada, b_ada, w_in, gmlp_ln_g, gmlp_ln_b, gmlp_ws, gmlp_bs,
           conv_dw_k, conv_dw_b, conv_ln_g, conv_ln_b, conv_w_pw, conv_b_pw,
           mla_q_norm, mla_w_uq, mla_kv_norm, mla_w_ukv, w_out, post_ln_g, post_ln_b):
    p = dict(w_in=w_in, gmlp_ln_g=gmlp_ln_g, gmlp_ln_b=gmlp_ln_b, gmlp_ws=gmlp_ws, gmlp_bs=gmlp_bs,
             conv_dw_k=conv_dw_k, conv_dw_b=conv_dw_b, conv_ln_g=conv_ln_g, conv_ln_b=conv_ln_b,
             conv_w_pw=conv_w_pw, conv_b_pw=conv_b_pw, mla_q_norm=mla_q_norm, mla_w_uq=mla_w_uq,
             mla_kv_norm=mla_kv_norm, mla_w_ukv=mla_w_ukv, w_out=w_out, post_ln_g=post_ln_g, post_ln_b=post_ln_b)
    depth = w_ada.shape[0]
    bp, seq, _ = x_prompt.shape
    bs, t_new, _ = x_sample.shape
    past_len = cache_latent.shape[2]
    alpha = (2 * depth) ** 0.25
    tile_p = min(IN_TILE, seq)
    chunk_p = min(seq, MLP_CHUNK)
    chunk_s = min(t_new, MLP_CHUNK)

    n_c = bp + bs
    c_all = jnp.concatenate([c_prompt, c_sample, jnp.zeros((-n_c % 8, D_MODEL), F32)], axis=0)
    mod_all = _ada(c_all, w_ada, b_ada)

    ct_p, st_p = _rope_tables(0, seq)
    ct_s, st_s = _rope_tables(past_len, t_new)
    zero_past = jnp.zeros((bp, HIST, W_B), F32)

    xp, xs = x_prompt, x_sample
    p_conv, s_conv, s_v = [], [], []
    p_stack = s_stack = None
    for l in range(depth):
        wp, wsm = _layer_weights(l, p, chunk_p, chunk_s)
        mod_p = mod_all[l, 0:bp].reshape(bp, 1, -1)
        mod_s = mod_all[l, bp:n_c].reshape(bs, 1, -1)

        q, lat, kr, yab, gc, conv, k, vt = _mixer_in(xp, mod_p, ct_p, st_p, zero_past, wp, tile=tile_p,
                                                      chunk=chunk_p, prompt=True, layer=l, depth=depth,
                                                      stacked=p_stack)
        p_stack = (lat, kr)
        o = _attn_prompt(q, k, vt)
        xp = _mixer_out(xp, mod_p, yab, o, gc, wp["w_out"], wp["ln"], tile=tile_p, alpha=alpha)
        p_conv.append(conv[:, HIST - (CONV_W - 1):])

        past = jnp.concatenate([jnp.zeros((bs, HIST - (CONV_W - 1), W_B), F32), state_conv[l]], axis=1)
        q, lat, kr, yab, gc, conv, vst = _mixer_in(xs, mod_s, ct_s, st_s, past, wsm, tile=t_new,
                                                   chunk=chunk_s, prompt=False, layer=l, depth=depth,
                                                   stacked=s_stack)
        s_stack = (lat, kr)
        o = _attn_sample(q, cache_latent, cache_krope, l, lat, kr, wsm)
        xs = _mixer_out(xs, mod_s, yab, o, gc, wsm["w_out"], wsm["ln"], tile=t_new, alpha=alpha)
        s_conv.append(conv[:, HIST - (CONV_W - 1):])
        s_v.append(vst)

    return (xp, xs, jnp.stack(p_conv), p_stack[0], p_stack[1],
            jnp.stack(s_conv), s_stack[0], s_stack[1], jnp.stack(s_v))
```
